```python
import functools
import jax, jax.numpy as jnp
from jax import lax
import numpy as np

D_MODEL = 1024
BATCH = 2
SEQ = 8192
DEPTH = 2
DEC_BATCH = 128
DEC_SEQ = 4
PAST_LEN = 16384
PAGE_SIZE = 128

HEAD_DIM = 64
MIX_WIDTH = D_MODEL
MEM_HEADS = 4
MEM_WIDTH = MEM_HEADS * HEAD_DIM
MAIN_WIDTH = MIX_WIDTH - MEM_WIDTH
RWKV_HEADS = MAIN_WIDTH // HEAD_DIM
DECAY_LORA = 64
AAA_LORA = 64
GATE_LORA = 128
RWKV_COLS = 3 * MAIN_WIDTH + DECAY_LORA + AAA_LORA + GATE_LORA
A_IN_COLS = RWKV_COLS + MEM_WIDTH
SWA_Q_HEADS = MAIN_WIDTH // HEAD_DIM
SWA_KV_HEADS = 4
SWA_GROUP = SWA_Q_HEADS // SWA_KV_HEADS
KV_WIDTH = SWA_KV_HEADS * HEAD_DIM
WINDOW = 128
BLOCK = 128
N_MEM = 256
D_FF = 4 * D_MODEL
N_A = DEPTH // 2
N_B = DEPTH - N_A
ROPE_THETA = 10000.0
NORM_EPS = 1e-6
LNX_EPS = 6.4e-4
L2_EPS = 1e-12
ATTN_SCALE = HEAD_DIM ** -0.5

kernel_name = "yoco_rwkv7_swa_sink_mem_step"


def rms_norm(x, g):
    xf = x.astype(jnp.float32)
    y = xf * lax.rsqrt(jnp.mean(xf * xf, axis=-1, keepdims=True) + NORM_EPS)
    return (y * g.astype(jnp.float32)).astype(x.dtype)


def rope(x, pos):
    half = HEAD_DIM // 2
    freqs = jnp.power(ROPE_THETA, -jnp.arange(half, dtype=jnp.float32) / half)
    ang = pos.astype(jnp.float32)[:, None] * freqs[None, :]
    cos, sin = jnp.cos(ang)[:, None, :], jnp.sin(ang)[:, None, :]
    xf = x.astype(jnp.float32)
    x1, x2 = xf[..., :half], xf[..., half:]
    return jnp.concatenate([x1 * cos - x2 * sin, x2 * cos + x1 * sin], axis=-1).astype(x.dtype)


def wkv_scan(r, w, k, v, a, b, S0):
    def step(S, inp):
        r_t, w_t, k_t, v_t, a_t, b_t = inp
        sa = jnp.einsum('bhij,bhj->bhi', S, a_t)
        S = S * w_t[:, :, None, :] + sa[..., None] * b_t[:, :, None, :] + v_t[..., None] * k_t[:, :, None, :]
        return S, jnp.einsum('bhij,bhj->bhi', S, r_t)
    xs = tuple(jnp.moveaxis(t, 1, 0) for t in (r, w, k, v, a, b))
    S, ys = lax.scan(step, S0, xs)
    return jnp.moveaxis(ys, 0, 1), S


def rwkv_time_mix(p, shift_prev, wkv_prev, mu, w_w2, w0, w_a2, a0, w_g2, k_k, k_a, r_k, lnx_w, lnx_b):
    B, T, _ = p.shape
    f32 = jnp.float32
    pf = p.astype(f32)
    prev = jnp.concatenate([shift_prev.astype(f32)[:, None], pf[:, :-1]], axis=1)
    ps = pf + (prev - pf) * mu.astype(f32)
    i1, i2, i3 = MAIN_WIDTH, 2 * MAIN_WIDTH, 3 * MAIN_WIDTH
    i4, i5 = i3 + DECAY_LORA, i3 + DECAY_LORA + AAA_LORA
    r, k, v, wd, ad, gd = jnp.split(ps, [i1, i2, i3, i4, i5], axis=-1)
    w_log = -jax.nn.softplus(-(w0.astype(f32) + jnp.tanh(wd) @ w_w2.astype(f32))) - 0.5
    decay = jnp.exp(-jnp.exp(w_log))
    a = jax.nn.sigmoid(a0.astype(f32) + ad @ w_a2.astype(f32))
    g = jax.nn.sigmoid(gd) @ w_g2.astype(f32)
    hs = lambda t: t.reshape(B, T, RWKV_HEADS, HEAD_DIM)
    kk = hs(k * k_k.astype(f32))
    kk = kk / jnp.maximum(jnp.linalg.norm(kk, axis=-1, keepdims=True), L2_EPS)
    k = k * (1.0 + (a - 1.0) * k_a.astype(f32))
    r_h, k_h, v_h, a_h = hs(r), hs(k), hs(v), hs(a)
    y, wkv = wkv_scan(r_h, hs(decay), k_h, v_h, -kk, kk * a_h, wkv_prev.astype(f32))
    mean = jnp.mean(y, axis=-1, keepdims=True)
    var = jnp.mean(jnp.square(y - mean), axis=-1, keepdims=True)
    y = ((y - mean) * lax.rsqrt(var + LNX_EPS)).reshape(B, T, MAIN_WIDTH) * lnx_w.astype(f32) + lnx_b.astype(f32)
    bonus = jnp.sum(r_h * k_h * r_k.astype(f32), axis=-1, keepdims=True) * v_h
    out = (y + bonus.reshape(B, T, MAIN_WIDTH)) * g
    return out.astype(p.dtype), p[:, -1], wkv.astype(wkv_prev.dtype)


def memory_kv(mem, g_norm, w_kv, g_knorm):
    B, M, _ = mem.shape
    kv = rms_norm(mem, g_norm) @ w_kv
    k = rms_norm(kv[..., :MEM_WIDTH].reshape(B, M, MEM_HEADS, HEAD_DIM), g_knorm)
    v = kv[..., MEM_WIDTH:].reshape(B, M, MEM_HEADS, HEAD_DIM)
    return k, v


def memory_attention(q, g_qnorm, mem_k, mem_v):
    B, T, _ = q.shape
    qh = rms_norm(q.reshape(B, T, MEM_HEADS, HEAD_DIM), g_qnorm)
    s = jnp.einsum('bthd,bmhd->bhtm', qh, mem_k, preferred_element_type=jnp.float32) * ATTN_SCALE
    p = jax.nn.softmax(s, axis=-1).astype(mem_v.dtype)
    return jnp.einsum('bhtm,bmhd->bthd', p, mem_v).reshape(B, T, MEM_WIDTH)


def sink_attention(q, k, v, mask, sinks):
    s = jnp.einsum('...qhgd,...khd->...hgqk', q, k, preferred_element_type=jnp.float32) * ATTN_SCALE
    s = jnp.where(mask[..., None, None, :, :], s, -jnp.inf)
    sink = sinks.astype(jnp.float32).reshape(SWA_KV_HEADS, SWA_GROUP, 1, 1)
    m = jnp.maximum(jnp.max(s, axis=-1, keepdims=True), sink)
    p = jnp.exp(s - m)
    p = p / (jnp.sum(p, axis=-1, keepdims=True) + jnp.exp(sink - m))
    return jnp.einsum('...hgqk,...khd->...qhgd', p.astype(v.dtype), v)


def swa_prompt_context(k, v):
    B, T = k.shape[:2]
    nb = T // BLOCK
    def band(t):
        tr = jnp.concatenate([jnp.zeros_like(t[:, :BLOCK]), t], axis=1)
        tr = tr.reshape(B, nb + 1, BLOCK, SWA_KV_HEADS, HEAD_DIM)
        return jnp.concatenate([tr[:, :-1], tr[:, 1:]], axis=2)
    kb, vb = band(k), band(v)
    blk = jnp.arange(nb)[:, None] * BLOCK
    qpos = blk + jnp.arange(BLOCK)[None, :]
    kpos = blk - BLOCK + jnp.arange(2 * BLOCK)[None, :]
    rel = qpos[:, :, None] - kpos[:, None, :]
    mask = (rel >= 0) & (rel < WINDOW) & (kpos[:, None, :] >= 0)
    def attend(q, sinks):
        qb = q.reshape(B, nb, BLOCK, SWA_KV_HEADS, SWA_GROUP, HEAD_DIM)
        return sink_attention(qb, kb, vb, mask, sinks).reshape(B, T, MAIN_WIDTH)
    win = min(WINDOW, T)
    return attend, k[:, T - win:], v[:, T - win:]


def swa_sample_context(k, v, k_buf, v_buf):
    B, T = k.shape[:2]
    W = k_buf.shape[1]
    kc = jnp.concatenate([k_buf.astype(k.dtype), k], axis=1)
    vc = jnp.concatenate([v_buf.astype(v.dtype), v], axis=1)
    qpos = PAST_LEN + jnp.arange(T)
    kpos = PAST_LEN - W + jnp.arange(W + T)
    rel = qpos[:, None] - kpos[None, :]
    mask = (rel >= 0) & (rel < WINDOW)
    def attend(q, sinks):
        qh = q.reshape(B, T, SWA_KV_HEADS, SWA_GROUP, HEAD_DIM)
        return sink_attention(qh, kc, vc, mask, sinks).reshape(B, T, MAIN_WIDTH)
    return attend, kc[:, T:], vc[:, T:]


def trunk(x, pos, mem_k, mem_v, shift0, wkv0, make_swa, *, norm_mix, norm_mlp, w_out, w_up, w_down,
          mem_qnorm, w_in_a, shift_mu, w_w2, w0, w_a2, a0, w_g2, k_k, k_a, r_k, lnx_w, lnx_b,
          w_in_b, swa_qnorm, sinks, kv_norm, w_kv, swa_knorm):
    B, T, _ = x.shape
    h = x
    new_shift, new_wkv = [], []
    attend = k_state = v_state = None
    for l in range(DEPTH):
        hn = rms_norm(h, norm_mix[l])
        if l < N_A:
            proj = hn @ w_in_a[l]
            main, sh, st = rwkv_time_mix(proj[..., :RWKV_COLS], shift0[l], wkv0[l], shift_mu[l], w_w2[l],
                                         w0[l], w_a2[l], a0[l], w_g2[l], k_k[l], k_a[l], r_k[l],
                                         lnx_w[l], lnx_b[l])
            new_shift.append(sh)
            new_wkv.append(st)
            q_mem = proj[..., RWKV_COLS:]
        else:
            j = l - N_A
            proj = hn @ w_in_b[j]
            q = rms_norm(proj[..., :MAIN_WIDTH].reshape(B, T, SWA_Q_HEADS, HEAD_DIM), swa_qnorm[j])
            main = attend(rope(q, pos), sinks[j])
            q_mem = proj[..., MAIN_WIDTH:]
        mem_o = memory_attention(q_mem, mem_qnorm[l], mem_k[l], mem_v[l])
        h = h + jnp.concatenate([main, mem_o], axis=-1) @ w_out[l]
        hn = rms_norm(h, norm_mlp[l])
        h = h + jnp.square(jax.nn.relu(hn @ w_up[l])) @ w_down[l]
        if l == N_A - 1:
            kv = rms_norm(h, kv_norm) @ w_kv
            k_sh = rope(rms_norm(kv[..., :KV_WIDTH].reshape(B, T, SWA_KV_HEADS, HEAD_DIM), swa_knorm), pos)
            v_sh = kv[..., KV_WIDTH:].reshape(B, T, SWA_KV_HEADS, HEAD_DIM)
            attend, k_state, v_state = make_swa(k_sh, v_sh)
    return h, jnp.stack(new_shift), jnp.stack(new_wkv), k_state, v_state


def setup_inputs(seed: int = 0) -> dict:
    key = jax.random.key(seed)
    ks = iter(jax.random.split(key, 48))
    def nrm(shape, scale=1.0):
        return jax.random.normal(next(ks), shape, jnp.float32) * scale
    def gain(shape):
        return 1.0 + nrm(shape, 0.05)
    def unif(shape, lo, hi):
        return jax.random.uniform(next(ks), shape, jnp.float32, lo, hi)
    win = min(WINDOW, PAST_LEN)
    return {
        "x_prompt": nrm((BATCH, SEQ, D_MODEL)),
        "x_sample": nrm((DEC_BATCH, DEC_SEQ, D_MODEL)),
        "state_rwkv_shift": nrm((N_A, DEC_BATCH, RWKV_COLS)),
        "state_rwkv_wkv": nrm((N_A, DEC_BATCH, RWKV_HEADS, HEAD_DIM, HEAD_DIM), 0.5),
        "cache_swa_k": nrm((DEC_BATCH, win, SWA_KV_HEADS, HEAD_DIM)),
        "cache_swa_v": nrm((DEC_BATCH, win, SWA_KV_HEADS, HEAD_DIM)),
        "cache_mem_k": nrm((DEPTH, DEC_BATCH, N_MEM, MEM_HEADS, HEAD_DIM)),
        "cache_mem_v": nrm((DEPTH, DEC_BATCH, N_MEM, MEM_HEADS, HEAD_DIM)),
        "mem_prompt": nrm((BATCH, N_MEM, D_MODEL)),
        "norm_mix": gain((DEPTH, D_MODEL)),
        "norm_mlp": gain((DEPTH, D_MODEL)),
        "w_out": nrm((DEPTH, MIX_WIDTH, D_MODEL), MIX_WIDTH ** -0.5),
        "w_up": nrm((DEPTH, D_MODEL, D_FF), D_MODEL ** -0.5),
        "w_down": nrm((DEPTH, D_FF, D_MODEL), D_FF ** -0.5),
        "mem_norm": gain((DEPTH, D_MODEL)),
        "w_mem_kv": nrm((DEPTH, D_MODEL, 2 * MEM_WIDTH), D_MODEL ** -0.5),
        "mem_qnorm": gain((DEPTH, HEAD_DIM)),
        "mem_knorm": gain((DEPTH, HEAD_DIM)),
        "w_in_a": nrm((N_A, D_MODEL, A_IN_COLS), D_MODEL ** -0.5),
        "shift_mu": unif((N_A, RWKV_COLS), 0.0, 1.0),
        "w_w2": nrm((N_A, DECAY_LORA, MAIN_WIDTH), 0.1),
        "w0": unif((N_A, MAIN_WIDTH), -6.5, -1.5),
        "w_a2": nrm((N_A, AAA_LORA, MAIN_WIDTH), AAA_LORA ** -0.5),
        "a0": nrm((N_A, MAIN_WIDTH), 0.1),
        "w_g2": nrm((N_A, GATE_LORA, MAIN_WIDTH), GATE_LORA ** -0.5),
        "k_k": 0.85 + nrm((N_A, MAIN_WIDTH), 0.05),
        "k_a": gain((N_A, MAIN_WIDTH)),
        "r_k": nrm((N_A, RWKV_HEADS, HEAD_DIM), 0.1),
        "lnx_w": gain((N_A, MAIN_WIDTH)),
        "lnx_b": nrm((N_A, MAIN_WIDTH), 0.02),
        "w_in_b": nrm((N_B, D_MODEL, MIX_WIDTH), D_MODEL ** -0.5),
        "swa_qnorm": gain((N_B, HEAD_DIM)),
        "sinks": nrm((N_B, SWA_Q_HEADS)),
        "kv_norm": gain((D_MODEL,)),
        "w_kv": nrm((D_MODEL, 2 * KV_WIDTH), D_MODEL ** -0.5),
        "swa_knorm": gain((HEAD_DIM,)),
    }


def reference(x_prompt, x_sample, state_rwkv_shift, state_rwkv_wkv, cache_swa_k, cache_swa_v,
              cache_mem_k, cache_mem_v, mem_prompt,
              norm_mix, norm_mlp, w_out, w_up, w_down, mem_norm, w_mem_kv, mem_qnorm, mem_knorm,
              w_in_a, shift_mu, w_w2, w0, w_a2, a0, w_g2, k_k, k_a, r_k, lnx_w, lnx_b,
              w_in_b, swa_qnorm, sinks, kv_norm, w_kv, swa_knorm):
    run = functools.partial(
        trunk, norm_mix=norm_mix, norm_mlp=norm_mlp, w_out=w_out, w_up=w_up, w_down=w_down,
        mem_qnorm=mem_qnorm, w_in_a=w_in_a, shift_mu=shift_mu, w_w2=w_w2, w0=w0, w_a2=w_a2, a0=a0,
        w_g2=w_g2, k_k=k_k, k_a=k_a, r_k=r_k, lnx_w=lnx_w, lnx_b=lnx_b, w_in_b=w_in_b,
        swa_qnorm=swa_qnorm, sinks=sinks, kv_norm=kv_norm, w_kv=w_kv, swa_knorm=swa_knorm)

    B, T, _ = x_prompt.shape
    mem_kv_p = [memory_kv(mem_prompt, mem_norm[l], w_mem_kv[l], mem_knorm[l]) for l in range(DEPTH)]
    p_mem_k = jnp.stack([kv[0] for kv in mem_kv_p])
    p_mem_v = jnp.stack([kv[1] for kv in mem_kv_p])
    shift0 = jnp.zeros((N_A, B, RWKV_COLS), x_prompt.dtype)
    wkv0 = jnp.zeros((N_A, B, RWKV_HEADS, HEAD_DIM, HEAD_DIM), x_prompt.dtype)
    y_prompt, p_shift, p_wkv, p_swa_k, p_swa_v = run(
        x_prompt, jnp.arange(T), p_mem_k, p_mem_v, shift0, wkv0, swa_prompt_context)

    Ts = x_sample.shape[1]
    sample_ctx = functools.partial(swa_sample_context, k_buf=cache_swa_k, v_buf=cache_swa_v)
    y_sample, s_shift, s_wkv, s_swa_k, s_swa_v = run(
        x_sample, PAST_LEN + jnp.arange(Ts), cache_mem_k, cache_mem_v, state_rwkv_shift, state_rwkv_wkv,
        sample_ctx)

    return (y_prompt, y_sample, p_shift, p_wkv, p_swa_k, p_swa_v, p_mem_k, p_mem_v,
            s_shift, s_wkv, s_swa_k, s_swa_v)
```

```python
import functools

import numpy as np
import jax
import jax.numpy as jnp
from jax import lax
from jax.experimental import pallas as pl
from jax.experimental.pallas import tpu as pltpu

f32 = jnp.float32
bf16 = jnp.bfloat16
HI = lax.Precision.HIGHEST

D_MODEL = 1024
HEAD_DIM = 64
MEM_HEADS = 4
MEM_WIDTH = MEM_HEADS * HEAD_DIM
MAIN_WIDTH = D_MODEL - MEM_WIDTH
RWKV_HEADS = MAIN_WIDTH // HEAD_DIM
DECAY_LORA = 64
AAA_LORA = 64
GATE_LORA = 128
RWKV_COLS = 3 * MAIN_WIDTH + DECAY_LORA + AAA_LORA + GATE_LORA
A_IN_COLS = RWKV_COLS + MEM_WIDTH
SWA_Q_HEADS = MAIN_WIDTH // HEAD_DIM
SWA_KV_HEADS = 4
SWA_GROUP = SWA_Q_HEADS // SWA_KV_HEADS
KV_WIDTH = SWA_KV_HEADS * HEAD_DIM
WINDOW = 128
BLOCK = 128
N_MEM = 256
D_FF = 4 * D_MODEL
PAST_LEN = 16384
ROPE_THETA = 10000.0
NORM_EPS = 1e-6
LNX_EPS = 6.4e-4
L2_EPS = 1e-12
ATTN_SCALE = HEAD_DIM ** -0.5

LANES = 128
SUBLANES = 8
HEADS_PER_SLAB = LANES // HEAD_DIM
N_SLABS = MAIN_WIDTH // LANES
CHUNK = 64
VMEM_LIMIT = 56 * 1024 * 1024
NEG_BIG = -1e30


def _cp(sem, vmem=VMEM_LIMIT):
    return pltpu.CompilerParams(dimension_semantics=sem, vmem_limit_bytes=vmem)


def _dot(a, b, precision=None):
    return jnp.dot(a, b, precision=precision, preferred_element_type=f32)


def _dot_nt(a, b, precision=None):
    return lax.dot_general(a, b, (((1,), (1,)), ((), ())), precision=precision,
                           preferred_element_type=f32)


def _dot_tn(a, b, precision=None):
    return lax.dot_general(a, b, (((0,), (0,)), ((), ())), precision=precision,
                           preferred_element_type=f32)


def _iota(shape, dim):
    return lax.broadcasted_iota(jnp.int32, shape, dim)


def _group_ones():
    r = _iota((LANES, LANES), 0)
    c = _iota((LANES, LANES), 1)
    return jnp.where((r < HEAD_DIM) == (c < HEAD_DIM), 1.0, 0.0).astype(f32)


def _head_sum(x, gmat):
    w = x.shape[-1]
    parts = [_dot(x[:, j * LANES:(j + 1) * LANES], gmat, HI) for j in range(w // LANES)]
    return parts[0] if len(parts) == 1 else jnp.concatenate(parts, axis=-1)


def _sigmoid(x):
    return 1.0 / (1.0 + jnp.exp(-x))


def _softplus(x):
    return jnp.maximum(x, 0.0) + jnp.log1p(jnp.exp(-jnp.abs(x)))


def _rot_half(x):
    lane = _iota(x.shape, 1)
    first = (lane & (HEAD_DIM - 1)) < (HEAD_DIM // 2)
    return jnp.where(first, pltpu.roll(x, LANES - HEAD_DIM // 2, 1), pltpu.roll(x, HEAD_DIM // 2, 1))


def _norm_matmul_kernel(x_ref, g_ref, w_ref, o_ref):
    x = x_ref[...]
    xn = x * lax.rsqrt(jnp.mean(x * x, axis=-1, keepdims=True) + NORM_EPS) * g_ref[...]
    o_ref[...] = _dot(xn.astype(bf16), w_ref[...])


def _norm_matmul(x, g, w_bf16, tm):
    n, d = x.shape
    m = w_bf16.shape[1]
    tm = min(tm, n)
    return pl.pallas_call(
        _norm_matmul_kernel,
        grid=(n // tm,),
        in_specs=[pl.BlockSpec((tm, d), lambda i: (i, 0)),
                  pl.BlockSpec((1, d), lambda i: (0, 0)),
                  pl.BlockSpec((d, m), lambda i: (0, 0))],
        out_specs=pl.BlockSpec((tm, m), lambda i: (i, 0)),
        out_shape=jax.ShapeDtypeStruct((n, m), f32),
        compiler_params=_cp(("parallel",)),
        name="norm_matmul",
    )(x, g.reshape(1, d), w_bf16)


def _kv_post_kernel(kv_ref, g_ref, cos_ref, sin_ref, k_ref, v_ref, *, use_rope):
    gmat = _group_ones()
    x = kv_ref[:, :KV_WIDTH]
    ms = _head_sum(x * x, gmat) * (1.0 / HEAD_DIM)
    xn = x * lax.rsqrt(ms + NORM_EPS) * g_ref[...]
    if use_rope:
        for j in range(KV_WIDTH // LANES):
            sl = slice(j * LANES, (j + 1) * LANES)
            xs = xn[:, sl]
            k_ref[:, sl] = xs * cos_ref[:, sl] + _rot_half(xs) * sin_ref[:, sl]
    else:
        k_ref[...] = xn
    v_ref[...] = kv_ref[:, KV_WIDTH:]


def _kv_post(kv, g_head, cos, sin, tm, use_rope):
    n = kv.shape[0]
    tm = min(tm, n)
    nt = cos.shape[0] // tm
    g = jnp.tile(g_head.reshape(1, HEAD_DIM), (1, KV_WIDTH // HEAD_DIM))
    return pl.pallas_call(
        functools.partial(_kv_post_kernel, use_rope=use_rope),
        grid=(n // tm,),
        in_specs=[pl.BlockSpec((tm, 2 * KV_WIDTH), lambda i: (i, 0)),
                  pl.BlockSpec((1, KV_WIDTH), lambda i: (0, 0)),
                  pl.BlockSpec((tm, KV_WIDTH), lambda i: (i % nt, 0)),
                  pl.BlockSpec((tm, KV_WIDTH), lambda i: (i % nt, 0))],
        out_specs=[pl.BlockSpec((tm, KV_WIDTH), lambda i: (i, 0)),
                   pl.BlockSpec((tm, KV_WIDTH), lambda i: (i, 0))],
        out_shape=[jax.ShapeDtypeStruct((n, KV_WIDTH), f32)] * 2,
        compiler_params=_cp(("parallel",)),
        name="kv_post",
    )(kv, g, cos, sin)


def _rwkv_prep_kernel(p_ref, pprev_ref, shift_ref, mu_ref, ww2_ref, w0_ref, wa2_ref, a0_ref, wg2_ref,
                      kk_ref, ka_ref, rk_ref,
                      r_o, lw_o, k_o, v_o, a_o, b_o, bonus_o, g_o):
    i = pl.program_id(1)
    gmat = _group_ones()
    pf = p_ref[0]
    first_prev = jnp.where(i == 0, shift_ref[0], pprev_ref[0][SUBLANES - 1:SUBLANES, :])
    row = _iota(pf.shape, 0)
    prev = jnp.where(row == 0, first_prev, pltpu.roll(pf, 1, 0))
    ps = pf + (prev - pf) * mu_ref[...]
    i1, i2, i3 = MAIN_WIDTH, 2 * MAIN_WIDTH, 3 * MAIN_WIDTH
    i4, i5 = i3 + DECAY_LORA, i3 + DECAY_LORA + AAA_LORA
    r, k, v = ps[:, :i1], ps[:, i1:i2], ps[:, i2:i3]
    wd, ad, gd = ps[:, i3:i4], ps[:, i4:i5], ps[:, i5:]
    w_log = -_softplus(-(w0_ref[...] + _dot(jnp.tanh(wd), ww2_ref[...], HI))) - 0.5
    lw = -jnp.exp(w_log)
    a_sig = _sigmoid(a0_ref[...] + _dot(ad, wa2_ref[...], HI))
    g = _dot(_sigmoid(gd), wg2_ref[...], HI)
    kk = k * kk_ref[...]
    kk = kk / jnp.maximum(jnp.sqrt(_head_sum(kk * kk, gmat)), L2_EPS)
    k2 = k * (1.0 + (a_sig - 1.0) * ka_ref[...])
    r_o[0] = r
    lw_o[0] = lw
    k_o[0] = k2
    v_o[0] = v
    a_o[0] = -kk
    b_o[0] = kk * a_sig
    bonus_o[0] = _head_sum(r * k2 * rk_ref[...], gmat) * v
    g_o[0] = g


def _rwkv_prep(proj, shift_prev, mu, w_w2, w0, w_a2, a0, w_g2, k_k, k_a, r_k, tt):
    bsz, t, _ = proj.shape
    tt = min(tt, t)
    row = lambda x: x.reshape(1, -1)
    full = lambda a: pl.BlockSpec(a.shape, lambda b, i: (0,) * a.ndim)
    consts = [row(mu), w_w2, row(w0), w_a2, row(a0), w_g2, row(k_k), row(k_a), row(r_k)]
    out_spec = pl.BlockSpec((1, tt, MAIN_WIDTH), lambda b, i: (b, i, 0))
    return pl.pallas_call(
        _rwkv_prep_kernel,
        grid=(bsz, t // tt),
        in_specs=[pl.BlockSpec((1, tt, RWKV_COLS), lambda b, i: (b, i, 0)),
                  pl.BlockSpec((1, SUBLANES, RWKV_COLS),
                               lambda b, i: (b, jnp.maximum(i * (tt // SUBLANES) - 1, 0), 0)),
                  pl.BlockSpec((1, 1, RWKV_COLS), lambda b, i: (b, 0, 0))]
                 + [full(c) for c in consts],
        out_specs=[out_spec] * 8,
        out_shape=[jax.ShapeDtypeStruct((bsz, t, MAIN_WIDTH), f32)] * 8,
        compiler_params=_cp(("parallel", "arbitrary")),
        name="rwkv_prep",
    )(proj, proj, shift_prev.reshape(bsz, 1, RWKV_COLS), *consts)


def _group_norm_gate(y, bonus, g, lnw, lnb, gmat):
    mean = _head_sum(y, gmat) * (1.0 / HEAD_DIM)
    yc = y - mean
    var = _head_sum(yc * yc, gmat) * (1.0 / HEAD_DIM)
    return (yc * lax.rsqrt(var + LNX_EPS) * lnw + lnb + bonus) * g


def _wkv_chunk_kernel(r_ref, lw_ref, k_ref, v_ref, a_ref, b_ref, bonus_ref, g_ref, lnw_ref, lnb_ref,
                      o_ref, s_ref, h_ref):
    i = pl.program_id(2)
    c = CHUNK
    tt = r_ref.shape[1]

    @pl.when(i == 0)
    def _():
        h_ref[...] = jnp.zeros_like(h_ref)

    gmat = _group_ones()
    ltri = jnp.where(_iota((c, c), 1) <= _iota((c, c), 0), 1.0, 0.0).astype(f32)
    row2 = _iota((c, 2 * c), 0)
    col2 = _iota((c, 2 * c), 1)
    cc = col2 & (c - 1)
    m_strict = cc < row2
    m_incl = cc <= row2
    k_cols = col2 >= c
    lane_lo = _iota((1, LANES), 1) < HEAD_DIM
    eye = _iota((LANES, LANES), 0) == _iota((LANES, LANES), 1)
    lnw = lnw_ref[...]
    lnb = lnb_ref[...]

    def body(ci, carry):
        sl = pl.ds(pl.multiple_of(ci * c, c), c)
        r, lw, k, v = r_ref[0, sl, :], lw_ref[0, sl, :], k_ref[0, sl, :], v_ref[0, sl, :]
        a, b = a_ref[0, sl, :], b_ref[0, sl, :]
        h = h_ref[...]
        cum = _dot(ltri, lw, HI)
        tot = cum[c - 1:c, :]
        e_neg = jnp.exp(-cum)
        e_rem = jnp.exp(tot - cum)
        at = a * jnp.exp(cum - lw)
        rt = r * jnp.exp(cum)
        bk = jnp.concatenate([b * e_neg, k * e_neg], axis=0)
        vv = jnp.concatenate([v, v], axis=0)
        us, arks = [], []
        for hh in range(HEADS_PER_SLAB):
            mh = lane_lo if hh == 0 else jnp.logical_not(lane_lo)
            ah = jnp.where(mh, at, 0.0)
            rh = jnp.where(mh, rt, 0.0)
            gm = _dot_nt(jnp.concatenate([ah, rh], axis=0), bk, HI)
            aak = jnp.where(m_strict, gm[:c], 0.0)
            arks.append(jnp.where(m_incl, gm[c:], 0.0))
            u = _dot(ah, h, HI) + _dot(jnp.where(k_cols, aak, 0.0), vv, HI)
            m = aak[:, :c]
            for lvl in range(6):
                if lvl < 5:
                    z = _dot(m, jnp.concatenate([u, m], axis=1), HI)
                    u = u + z[:, :LANES]
                    m = z[:, LANES:]
                else:
                    u = u + _dot(m, u, HI)
            us.append(u)
        u = jnp.where(lane_lo, us[0], us[1])
        uv = jnp.concatenate([u, v], axis=0)
        y = _dot(rt, h, HI) + jnp.where(lane_lo, _dot(arks[0], uv, HI), _dot(arks[1], uv, HI))
        tot_col = jnp.sum(jnp.where(eye, jnp.exp(tot), 0.0), axis=1, keepdims=True)
        hn = _dot_tn(jnp.concatenate([b * e_rem, k * e_rem], axis=0), uv, HI)
        h_ref[...] = h * tot_col + jnp.where(gmat > 0.5, hn, 0.0)
        o_ref[0, sl, :] = _group_norm_gate(y, bonus_ref[0, sl, :], g_ref[0, sl, :], lnw, lnb, gmat)
        return carry

    lax.fori_loop(0, tt // c, body, 0)

    @pl.when(i == pl.num_programs(2) - 1)
    def _():
        s_ref[0, 0] = h_ref[...].T


def _wkv_chunked(r, lw, k, v, a, b, bonus, g, lnx_w, lnx_b, tt):
    bsz, t, _ = r.shape
    tt = min(tt, t)
    spec = pl.BlockSpec((1, tt, LANES), lambda bb, hp, i: (bb, i, hp))
    vec = pl.BlockSpec((1, LANES), lambda bb, hp, i: (0, hp))
    out, st = pl.pallas_call(
        _wkv_chunk_kernel,
        grid=(bsz, N_SLABS, t // tt),
        in_specs=[spec] * 8 + [vec, vec],
        out_specs=[spec, pl.BlockSpec((1, 1, LANES, LANES), lambda bb, hp, i: (bb, hp, 0, 0))],
        out_shape=[jax.ShapeDtypeStruct((bsz, t, MAIN_WIDTH), f32),
                   jax.ShapeDtypeStruct((bsz, N_SLABS, LANES, LANES), f32)],
        scratch_shapes=[pltpu.VMEM((LANES, LANES), f32)],
        compiler_params=_cp(("parallel", "parallel", "arbitrary")),
        name="wkv_chunked",
    )(r, lw, k, v, a, b, bonus, g, lnx_w.reshape(1, -1), lnx_b.reshape(1, -1))
    st = jnp.stack([st[:, :, :HEAD_DIM, :HEAD_DIM], st[:, :, HEAD_DIM:, HEAD_DIM:]], axis=2)
    return out, st.reshape(bsz, RWKV_HEADS, HEAD_DIM, HEAD_DIM)


def _wkv_steps_kernel(s_ref, r_ref, lw_ref, k_ref, v_ref, a_ref, b_ref, bonus_ref, g_ref, lnw_ref, lnb_ref,
                      o_ref, so_ref, *, n_steps):
    s = s_ref[0]
    eye = (_iota((HEAD_DIM, HEAD_DIM), 0) == _iota((HEAD_DIM, HEAD_DIM), 1))[None]
    lnw = lnw_ref[...]
    lnb = lnb_ref[...]
    o_ref[...] = jnp.zeros_like(o_ref)
    for t in range(n_steps):
        sl = slice(t, t + 1)
        r, k, v = r_ref[0, :, sl, :], k_ref[0, :, sl, :], v_ref[0, :, sl, :]
        a, b = a_ref[0, :, sl, :], b_ref[0, :, sl, :]
        w = jnp.exp(lw_ref[0, :, sl, :])
        sa = jnp.sum(s * a, axis=2, keepdims=True)
        v_col = jnp.sum(jnp.where(eye, v, 0.0), axis=2, keepdims=True)
        s = s * w + sa * b + v_col * k
        y_col = jnp.sum(s * r, axis=2, keepdims=True)
        y = jnp.sum(jnp.where(eye, y_col, 0.0), axis=1, keepdims=True)
        mean = jnp.mean(y, axis=2, keepdims=True)
        yc = y - mean
        var = jnp.mean(yc * yc, axis=2, keepdims=True)
        yn = yc * lax.rsqrt(var + LNX_EPS) * lnw + lnb
        o_ref[0, :, sl, :] = (yn + bonus_ref[0, :, sl, :]) * g_ref[0, :, sl, :]
    so_ref[0] = s


def _wkv_steps(state, seqs, lnx_w, lnx_b, n_steps):
    bsz, tp, _ = seqs[0].shape
    hm = lambda x: x.reshape(bsz, tp, RWKV_HEADS, HEAD_DIM).transpose(0, 2, 1, 3)
    spec = pl.BlockSpec((1, RWKV_HEADS, tp, HEAD_DIM), lambda bb: (bb, 0, 0, 0))
    sspec = pl.BlockSpec((1, RWKV_HEADS, HEAD_DIM, HEAD_DIM), lambda bb: (bb, 0, 0, 0))
    vec = pl.BlockSpec((RWKV_HEADS, 1, HEAD_DIM), lambda bb: (0, 0, 0))
    out, new_state = pl.pallas_call(
        functools.partial(_wkv_steps_kernel, n_steps=n_steps),
        grid=(bsz,),
        in_specs=[sspec] + [spec] * 8 + [vec, vec],
        out_specs=[spec, sspec],
        out_shape=[jax.ShapeDtypeStruct((bsz, RWKV_HEADS, tp, HEAD_DIM), f32),
                   jax.ShapeDtypeStruct(state.shape, f32)],
        compiler_params=_cp(("parallel",)),
        name="wkv_steps",
    )(state, *[hm(x) for x in seqs],
      lnx_w.reshape(RWKV_HEADS, 1, HEAD_DIM), lnx_b.reshape(RWKV_HEADS, 1, HEAD_DIM))
    return out.transpose(0, 2, 1, 3).reshape(bsz, tp, MAIN_WIDTH), new_state


def _mem_attn_kernel(q_ref, g_ref, k_ref, v_ref, o_ref):
    gmat = _group_ones()
    q = q_ref[0]
    tq = q.shape[0]
    qn = q * lax.rsqrt(_head_sum(q * q, gmat) * (1.0 / HEAD_DIM) + NORM_EPS) * g_ref[...]
    head = _iota((1, MEM_WIDTH), 1) // HEAD_DIM
    qs = jnp.concatenate([jnp.where(head == h, qn, 0.0) for h in range(MEM_HEADS)], axis=0)
    s = _dot_nt(qs.astype(bf16), k_ref[0].astype(bf16)) * ATTN_SCALE
    e = jnp.exp(s - jnp.max(s, axis=-1, keepdims=True))
    o4 = _dot(e.astype(bf16), v_ref[0].astype(bf16)) / jnp.sum(e, axis=-1, keepdims=True)
    o = jnp.zeros((tq, MEM_WIDTH), f32)
    for h in range(MEM_HEADS):
        o = o + jnp.where(head == h, o4[h * tq:(h + 1) * tq], 0.0)
    o_ref[0] = o


def _mem_attn(proj, col_block, g_qnorm, mem_k, mem_v, tq):
    bsz, t, _ = proj.shape
    tq = min(tq, t)
    g = jnp.tile(g_qnorm.reshape(1, HEAD_DIM), (1, MEM_HEADS))
    return pl.pallas_call(
        _mem_attn_kernel,
        grid=(bsz, t // tq),
        in_specs=[pl.BlockSpec((1, tq, MEM_WIDTH), lambda b, i: (b, i, col_block)),
                  pl.BlockSpec((1, MEM_WIDTH), lambda b, i: (0, 0)),
                  pl.BlockSpec((1, N_MEM, MEM_WIDTH), lambda b, i: (b, 0, 0)),
                  pl.BlockSpec((1, N_MEM, MEM_WIDTH), lambda b, i: (b, 0, 0))],
        out_specs=pl.BlockSpec((1, tq, MEM_WIDTH), lambda b, i: (b, i, 0)),
        out_shape=jax.ShapeDtypeStruct((bsz, t, MEM_WIDTH), f32),
        compiler_params=_cp(("parallel", "parallel")),
        name="mem_attn",
    )(proj, g, mem_k, mem_v)


def _swa_kernel(sink_ref, q_ref, g_ref, cos_ref, sin_ref, kp_ref, kc_ref, vp_ref, vc_ref, o_ref, qn_ref,
                *, first_block_has_no_past):
    i = pl.program_id(1)
    gmat = _group_ones()
    tq = q_ref.shape[1]
    for j in range(N_SLABS):
        sl = slice(j * LANES, (j + 1) * LANES)
        x = q_ref[0, :, sl]
        xn = x * lax.rsqrt(_head_sum(x * x, gmat) * (1.0 / HEAD_DIM) + NORM_EPS) * g_ref[...]
        qn_ref[:, sl] = xn * cos_ref[...] + _rot_half(xn) * sin_ref[...]
    kc, vc = kc_ref[0], vc_ref[0]
    if tq < BLOCK:
        pad = jnp.zeros((BLOCK - tq, KV_WIDTH), f32)
        kc = jnp.concatenate([kc, pad], axis=0)
        vc = jnp.concatenate([vc, pad], axis=0)
    keys = jnp.concatenate([kp_ref[0], kc], axis=0).astype(bf16)
    vals = jnp.concatenate([vp_ref[0], vc], axis=0).astype(bf16)
    row = _iota((tq, 2 * BLOCK), 0)
    col = _iota((tq, 2 * BLOCK), 1)
    mask = (col > row) & (col <= row + WINDOW)
    if first_block_has_no_past:
        mask = mask & ((i > 0) | (col >= BLOCK))
    for hk in range(SWA_KV_HEADS):
        k_h = keys[:, hk * HEAD_DIM:(hk + 1) * HEAD_DIM]
        v_h = vals[:, hk * HEAD_DIM:(hk + 1) * HEAD_DIM]
        for gq in range(SWA_GROUP):
            hq = hk * SWA_GROUP + gq
            sl = slice(hq * HEAD_DIM, (hq + 1) * HEAD_DIM)
            s = _dot_nt(qn_ref[:, sl].astype(bf16), k_h) * ATTN_SCALE
            s = jnp.where(mask, s, NEG_BIG)
            sink = sink_ref[hq]
            m = jnp.maximum(jnp.max(s, axis=-1, keepdims=True), sink)
            e = jnp.exp(s - m)
            denom = jnp.sum(e, axis=-1, keepdims=True) + jnp.exp(sink - m)
            o_ref[0, :, sl] = _dot(e.astype(bf16), v_h) / denom


def _swa(proj, g_qnorm, sinks, cos, sin, k_prev, k_cur, v_prev, v_cur, tq, prev_index, first_block_has_no_past):
    bsz, t, _ = proj.shape
    g = jnp.tile(g_qnorm.reshape(1, HEAD_DIM), (1, HEADS_PER_SLAB))
    nq = t // tq
    return pl.pallas_call(
        functools.partial(_swa_kernel, first_block_has_no_past=first_block_has_no_past),
        grid=(bsz, nq),
        in_specs=[pl.BlockSpec(memory_space=pltpu.SMEM),
                  pl.BlockSpec((1, tq, MAIN_WIDTH), lambda b, i: (b, i, 0)),
                  pl.BlockSpec((1, LANES), lambda b, i: (0, 0)),
                  pl.BlockSpec((tq, LANES), lambda b, i: (i, 0)),
                  pl.BlockSpec((tq, LANES), lambda b, i: (i, 0)),
                  pl.BlockSpec((1, BLOCK, KV_WIDTH), lambda b, i: (b, prev_index(i), 0)),
                  pl.BlockSpec((1, tq, KV_WIDTH), lambda b, i: (b, i, 0)),
                  pl.BlockSpec((1, BLOCK, KV_WIDTH), lambda b, i: (b, prev_index(i), 0)),
                  pl.BlockSpec((1, tq, KV_WIDTH), lambda b, i: (b, i, 0))],
        out_specs=pl.BlockSpec((1, tq, MAIN_WIDTH), lambda b, i: (b, i, 0)),
        out_shape=jax.ShapeDtypeStruct((bsz, t, MAIN_WIDTH), f32),
        scratch_shapes=[pltpu.VMEM((tq, MAIN_WIDTH), f32)],
        compiler_params=_cp(("parallel", "parallel")),
        name="swa_attn",
    )(sinks, proj, g, cos, sin, k_prev, k_cur, v_prev, v_cur)


def _out_mlp_kernel(h_ref, main_ref, mem_ref, woa_ref, wob_ref, g_ref, wup_ref, wdn_ref, o_ref, *, tf):
    h1 = (h_ref[...] + _dot(main_ref[...].astype(bf16), woa_ref[...])
          + _dot(mem_ref[...].astype(bf16), wob_ref[...]))
    hn = (h1 * lax.rsqrt(jnp.mean(h1 * h1, axis=-1, keepdims=True) + NORM_EPS) * g_ref[...]).astype(bf16)
    acc = None
    for j in range(D_FF // tf):
        u = _dot(hn, wup_ref[:, j * tf:(j + 1) * tf])
        u = jnp.square(jnp.maximum(u, 0.0)).astype(bf16)
        d = _dot(u, wdn_ref[j * tf:(j + 1) * tf, :])
        acc = d if acc is None else acc + d
    o_ref[...] = h1 + acc


def _out_mlp(h, main, mem_o, w_out_bf16, g_mlp, w_up_bf16, w_down_bf16, tm, tf=1024):
    n = h.shape[0]
    tm = min(tm, n)
    rows = lambda w: pl.BlockSpec((tm, w), lambda i: (i, 0))
    resident = lambda a: pl.BlockSpec(a.shape, lambda i: (0, 0))
    woa, wob = w_out_bf16[:MAIN_WIDTH], w_out_bf16[MAIN_WIDTH:]
    g = g_mlp.reshape(1, D_MODEL)
    return pl.pallas_call(
        functools.partial(_out_mlp_kernel, tf=tf),
        grid=(n // tm,),
        in_specs=[rows(D_MODEL), rows(MAIN_WIDTH), rows(MEM_WIDTH),
                  resident(woa), resident(wob), resident(g), resident(w_up_bf16), resident(w_down_bf16)],
        out_specs=rows(D_MODEL),
        out_shape=jax.ShapeDtypeStruct((n, D_MODEL), f32),
        compiler_params=_cp(("parallel",)),
        name="out_mlp",
    )(h, main, mem_o, woa, wob, g, w_up_bf16, w_down_bf16)


def _rope_tables(pos, reps):
    half = HEAD_DIM // 2
    freqs = jnp.power(ROPE_THETA, -jnp.arange(half, dtype=f32) / half)
    ang = pos.astype(f32)[:, None] * freqs[None, :]
    cos, sin = jnp.cos(ang), jnp.sin(ang)
    cos_h = jnp.concatenate([cos, cos], axis=-1)
    sin_h = jnp.concatenate([-sin, sin], axis=-1)
    return jnp.tile(cos_h, (1, reps)), jnp.tile(sin_h, (1, reps))


def _trunk(x, pos, mem_k, mem_v, shift0, wkv0, swa_cache, t_real, W):
    bsz, tp, _ = x.shape
    n = bsz * tp
    prompt = swa_cache is None
    tm = 512
    flat = lambda a: a.reshape(n, a.shape[-1])
    unflat = lambda a: a.reshape(bsz, tp, a.shape[-1])
    cos4, sin4 = _rope_tables(pos, KV_WIDTH // HEAD_DIM)
    cos2, sin2 = cos4[:, :LANES], sin4[:, :LANES]

    proj = unflat(_norm_matmul(flat(x), W["norm_mix"][0], W["w_in_a"], tm))
    new_shift = proj[:, t_real - 1, :RWKV_COLS]
    seqs = _rwkv_prep(proj, shift0, W["shift_mu"], W["w_w2"], W["w0"], W["w_a2"], W["a0"], W["w_g2"],
                      W["k_k"], W["k_a"], W["r_k"], tt=256)
    if prompt:
        main, new_wkv = _wkv_chunked(*seqs, W["lnx_w"], W["lnx_b"], tt=512)
    else:
        main, new_wkv = _wkv_steps(wkv0, seqs, W["lnx_w"], W["lnx_b"], t_real)
    mem_o = _mem_attn(proj, RWKV_COLS // MEM_WIDTH, W["mem_qnorm"][0], mem_k[0], mem_v[0], tq=256)
    h = _out_mlp(flat(x), flat(main), flat(mem_o), W["w_out"][0], W["norm_mlp"][0], W["w_up"][0],
                 W["w_down"][0], tm)

    kv = _norm_matmul(h, W["kv_norm"], W["w_kv"], tm)
    if prompt:
        k_sh, v_sh = _kv_post(kv, W["swa_knorm"], cos4, sin4, tm, True)
    else:
        k_sh, v_sh = _kv_post(kv, W["swa_knorm"], jnp.tile(cos4, (tm // tp, 1)), jnp.tile(sin4, (tm // tp, 1)),
                              tm, True)
    k_sh, v_sh = unflat(k_sh), unflat(v_sh)

    proj = unflat(_norm_matmul(h, W["norm_mix"][1], W["w_in_b"], tm))
    if prompt:
        main = _swa(proj, W["swa_qnorm"], W["sinks"], cos2, sin2, k_sh, k_sh, v_sh, v_sh, BLOCK,
                    lambda i: jnp.maximum(i - 1, 0), True)
        win = min(WINDOW, tp)
        k_state, v_state = k_sh[:, tp - win:], v_sh[:, tp - win:]
    else:
        ck, cv = swa_cache
        main = _swa(proj, W["swa_qnorm"], W["sinks"], cos2, sin2, ck, k_sh, cv, v_sh, tp, lambda i: 0, False)
        k_state = jnp.concatenate([ck[:, t_real:], k_sh[:, :t_real]], axis=1)
        v_state = jnp.concatenate([cv[:, t_real:], v_sh[:, :t_real]], axis=1)
    mem_o = _mem_attn(proj, MAIN_WIDTH // MEM_WIDTH, W["mem_qnorm"][1], mem_k[1], mem_v[1], tq=256)
    y = _out_mlp(h, flat(main), flat(mem_o), W["w_out"][1], W["norm_mlp"][1], W["w_up"][1], W["w_down"][1], tm)
    heads = lambda a: a.reshape(bsz, a.shape[1], SWA_KV_HEADS, HEAD_DIM)
    return unflat(y)[:, :t_real], new_shift[None], new_wkv[None], heads(k_state), heads(v_state)


def kernel(x_prompt, x_sample, state_rwkv_shift, state_rwkv_wkv, cache_swa_k, cache_swa_v, cache_mem_k,
           cache_mem_v, mem_prompt, norm_mix, norm_mlp, w_out, w_up, w_down, mem_norm, w_mem_kv, mem_qnorm,
           mem_knorm, w_in_a, shift_mu, w_w2, w0, w_a2, a0, w_g2, k_k, k_a, r_k, lnx_w, lnx_b, w_in_b,
           swa_qnorm, sinks, kv_norm, w_kv, swa_knorm):
    W = dict(norm_mix=norm_mix, norm_mlp=norm_mlp, w_out=w_out.astype(bf16), w_up=w_up.astype(bf16),
             w_down=w_down.astype(bf16), mem_qnorm=mem_qnorm, w_in_a=w_in_a[0].astype(bf16),
             shift_mu=shift_mu[0], w_w2=w_w2[0], w0=w0[0], w_a2=w_a2[0], a0=a0[0], w_g2=w_g2[0],
             k_k=k_k[0], k_a=k_a[0], r_k=r_k[0].reshape(-1), lnx_w=lnx_w[0], lnx_b=lnx_b[0],
             w_in_b=w_in_b[0].astype(bf16), swa_qnorm=swa_qnorm[0], sinks=sinks[0], kv_norm=kv_norm,
             w_kv=w_kv.astype(bf16), swa_knorm=swa_knorm)
    bp, tp, _ = x_prompt.shape
    bs, ts, _ = x_sample.shape
    depth = norm_mix.shape[0]

    mem_flat = mem_prompt.reshape(bp * N_MEM, D_MODEL)
    ones = jnp.ones((N_MEM, KV_WIDTH), f32)
    p_mem_k, p_mem_v = [], []
    for l in range(depth):
        kv = _norm_matmul(mem_flat, mem_norm[l], w_mem_kv[l].astype(bf16), 512)
        mk, mv = _kv_post(kv, mem_knorm[l], ones, ones, N_MEM, False)
        p_mem_k.append(mk.reshape(bp, N_MEM, MEM_WIDTH))
        p_mem_v.append(mv.reshape(bp, N_MEM, MEM_WIDTH))
    y_p, p_shift, p_wkv, p_k, p_v = _trunk(
        x_prompt, jnp.arange(tp), p_mem_k, p_mem_v, jnp.zeros((bp, RWKV_COLS), f32), None, None, tp, W)
    mem_heads = lambda a: jnp.stack(a).reshape(depth, bp, N_MEM, MEM_HEADS, HEAD_DIM)

    tpad = -(-ts // SUBLANES) * SUBLANES
    x_s = jnp.pad(x_sample, ((0, 0), (0, tpad - ts), (0, 0)))
    win = cache_swa_k.shape[1]
    y_s, s_shift, s_wkv, s_k, s_v = _trunk(
        x_s, PAST_LEN + jnp.arange(tpad), cache_mem_k.reshape(depth, bs, N_MEM, MEM_WIDTH),
        cache_mem_v.reshape(depth, bs, N_MEM, MEM_WIDTH), state_rwkv_shift[0], state_rwkv_wkv[0],
        (cache_swa_k.reshape(bs, win, KV_WIDTH), cache_swa_v.reshape(bs, win, KV_WIDTH)), ts, W)

    return (y_p, y_s, p_shift, p_wkv, p_k, p_v, mem_heads(p_mem_k), mem_heads(p_mem_v),
            s_shift, s_wkv, s_k, s_v)
```

```python
import functools

import numpy as np
import jax
import jax.numpy as jnp
from jax import lax
from jax.experimental import pallas as pl
from jax.experimental.pallas import tpu as pltpu

f32 = jnp.float32
bf16 = jnp.bfloat16
HI = lax.Precision.HIGHEST

D_MODEL = 1024
HEAD_DIM = 64
MEM_HEADS = 4
MEM_WIDTH = MEM_HEADS * HEAD_DIM
MAIN_WIDTH = D_MODEL - MEM_WIDTH
RWKV_HEADS = MAIN_WIDTH // HEAD_DIM
DECAY_LORA = 64
AAA_LORA = 64
GATE_LORA = 128
RWKV_COLS = 3 * MAIN_WIDTH + DECAY_LORA + AAA_LORA + GATE_LORA
A_IN_COLS = RWKV_COLS + MEM_WIDTH
SWA_Q_HEADS = MAIN_WIDTH // HEAD_DIM
SWA_KV_HEADS = 4
SWA_GROUP = SWA_Q_HEADS // SWA_KV_HEADS
KV_WIDTH = SWA_KV_HEADS * HEAD_DIM
WINDOW = 128
BLOCK = 128
N_MEM = 256
D_FF = 4 * D_MODEL
PAST_LEN = 16384
ROPE_THETA = 10000.0
NORM_EPS = 1e-6
LNX_EPS = 6.4e-4
L2_EPS = 1e-12
ATTN_SCALE = HEAD_DIM ** -0.5

LANES = 128
SUBLANES = 8
HEADS_PER_SLAB = LANES // HEAD_DIM
N_SLABS = MAIN_WIDTH // LANES
CHUNK = 64
VMEM_LIMIT = 56 * 1024 * 1024
NEG_BIG = -1e30


def _cp(sem, vmem=VMEM_LIMIT):
    return pltpu.CompilerParams(dimension_semantics=sem, vmem_limit_bytes=vmem)


def _dot(a, b, precision=None):
    return jnp.dot(a, b, precision=precision, preferred_element_type=f32)


def _dot_nt(a, b, precision=None):
    return lax.dot_general(a, b, (((1,), (1,)), ((), ())), precision=precision,
                           preferred_element_type=f32)


def _dot_tn(a, b, precision=None):
    return lax.dot_general(a, b, (((0,), (0,)), ((), ())), precision=precision,
                           preferred_element_type=f32)


def _iota(shape, dim):
    return lax.broadcasted_iota(jnp.int32, shape, dim)


def _group_ones():
    r = _iota((LANES, LANES), 0)
    c = _iota((LANES, LANES), 1)
    return jnp.where((r < HEAD_DIM) == (c < HEAD_DIM), 1.0, 0.0).astype(f32)


def _head_sum(x, gmat):
    w = x.shape[-1]
    gb = gmat.astype(bf16)
    hi = x.astype(bf16)
    lo = (x - hi.astype(f32)).astype(bf16)
    parts = [_dot(hi[:, j * LANES:(j + 1) * LANES], gb) + _dot(lo[:, j * LANES:(j + 1) * LANES], gb)
             for j in range(w // LANES)]
    return parts[0] if len(parts) == 1 else jnp.concatenate(parts, axis=-1)


def _sigmoid(x):
    return 1.0 / (1.0 + jnp.exp(-x))


def _softplus(x):
    return jnp.maximum(x, 0.0) + jnp.log1p(jnp.exp(-jnp.abs(x)))


def _rot_half(x):
    lane = _iota(x.shape, 1)
    first = (lane & (HEAD_DIM - 1)) < (HEAD_DIM // 2)
    return jnp.where(first, pltpu.roll(x, LANES - HEAD_DIM // 2, 1), pltpu.roll(x, HEAD_DIM // 2, 1))


def _norm_matmul_kernel(x_ref, g_ref, w_ref, o_ref):
    x = x_ref[...]
    xn = x * lax.rsqrt(jnp.mean(x * x, axis=-1, keepdims=True) + NORM_EPS) * g_ref[...]
    o_ref[...] = _dot(xn.astype(bf16), w_ref[...])


def _norm_matmul(x, g, w_bf16, tm):
    n, d = x.shape
    m = w_bf16.shape[1]
    tm = min(tm, n)
    return pl.pallas_call(
        _norm_matmul_kernel,
        grid=(n // tm,),
        in_specs=[pl.BlockSpec((tm, d), lambda i: (i, 0)),
                  pl.BlockSpec((1, d), lambda i: (0, 0)),
                  pl.BlockSpec((d, m), lambda i: (0, 0))],
        out_specs=pl.BlockSpec((tm, m), lambda i: (i, 0)),
        out_shape=jax.ShapeDtypeStruct((n, m), f32),
        compiler_params=_cp(("parallel",)),
        name="norm_matmul",
    )(x, g.reshape(1, d), w_bf16)


def _kv_post_kernel(kv_ref, g_ref, cos_ref, sin_ref, k_ref, v_ref, *, use_rope):
    gmat = _group_ones()
    x = kv_ref[:, :KV_WIDTH]
    ms = _head_sum(x * x, gmat) * (1.0 / HEAD_DIM)
    xn = x * lax.rsqrt(ms + NORM_EPS) * g_ref[...]
    if use_rope:
        for j in range(KV_WIDTH // LANES):
            sl = slice(j * LANES, (j + 1) * LANES)
            xs = xn[:, sl]
            k_ref[:, sl] = xs * cos_ref[:, sl] + _rot_half(xs) * sin_ref[:, sl]
    else:
        k_ref[...] = xn
    v_ref[...] = kv_ref[:, KV_WIDTH:]


def _kv_post(kv, g_head, cos, sin, tm, use_rope):
    n = kv.shape[0]
    tm = min(tm, n)
    nt = cos.shape[0] // tm
    g = jnp.tile(g_head.reshape(1, HEAD_DIM), (1, KV_WIDTH // HEAD_DIM))
    return pl.pallas_call(
        functools.partial(_kv_post_kernel, use_rope=use_rope),
        grid=(n // tm,),
        in_specs=[pl.BlockSpec((tm, 2 * KV_WIDTH), lambda i: (i, 0)),
                  pl.BlockSpec((1, KV_WIDTH), lambda i: (0, 0)),
                  pl.BlockSpec((tm, KV_WIDTH), lambda i: (i % nt, 0)),
                  pl.BlockSpec((tm, KV_WIDTH), lambda i: (i % nt, 0))],
        out_specs=[pl.BlockSpec((tm, KV_WIDTH), lambda i: (i, 0)),
                   pl.BlockSpec((tm, KV_WIDTH), lambda i: (i, 0))],
        out_shape=[jax.ShapeDtypeStruct((n, KV_WIDTH), f32)] * 2,
        compiler_params=_cp(("parallel",)),
        name="kv_post",
    )(kv, g, cos, sin)


def _rwkv_prep_kernel(p_ref, pprev_ref, shift_ref, mu_ref, ww2_ref, w0_ref, wa2_ref, a0_ref, wg2_ref,
                      kk_ref, ka_ref, rk_ref,
                      r_o, lw_o, k_o, v_o, a_o, b_o, bonus_o, g_o):
    i = pl.program_id(1)
    gmat = _group_ones()
    pf = p_ref[0]
    first_prev = jnp.where(i == 0, shift_ref[0], pprev_ref[0][SUBLANES - 1:SUBLANES, :])
    row = _iota(pf.shape, 0)
    prev = jnp.where(row == 0, first_prev, pltpu.roll(pf, 1, 0))
    ps = pf + (prev - pf) * mu_ref[...]
    i1, i2, i3 = MAIN_WIDTH, 2 * MAIN_WIDTH, 3 * MAIN_WIDTH
    i4, i5 = i3 + DECAY_LORA, i3 + DECAY_LORA + AAA_LORA
    r, k, v = ps[:, :i1], ps[:, i1:i2], ps[:, i2:i3]
    wd, ad, gd = ps[:, i3:i4], ps[:, i4:i5], ps[:, i5:]
    w_log = -_softplus(-(w0_ref[...] + _dot(jnp.tanh(wd), ww2_ref[...], HI))) - 0.5
    lw = -jnp.exp(w_log)
    a_sig = _sigmoid(a0_ref[...] + _dot(ad, wa2_ref[...], HI))
    g = _dot(_sigmoid(gd), wg2_ref[...], HI)
    kk = k * kk_ref[...]
    kk = kk / jnp.maximum(jnp.sqrt(_head_sum(kk * kk, gmat)), L2_EPS)
    k2 = k * (1.0 + (a_sig - 1.0) * ka_ref[...])
    r_o[0] = r
    lw_o[0] = lw
    k_o[0] = k2
    v_o[0] = v
    a_o[0] = -kk
    b_o[0] = kk * a_sig
    bonus_o[0] = _head_sum(r * k2 * rk_ref[...], gmat) * v
    g_o[0] = g


def _rwkv_prep(proj, shift_prev, mu, w_w2, w0, w_a2, a0, w_g2, k_k, k_a, r_k, tt):
    bsz, t, _ = proj.shape
    tt = min(tt, t)
    row = lambda x: x.reshape(1, -1)
    full = lambda a: pl.BlockSpec(a.shape, lambda b, i: (0,) * a.ndim)
    consts = [row(mu), w_w2, row(w0), w_a2, row(a0), w_g2, row(k_k), row(k_a), row(r_k)]
    out_spec = pl.BlockSpec((1, tt, MAIN_WIDTH), lambda b, i: (b, i, 0))
    return pl.pallas_call(
        _rwkv_prep_kernel,
        grid=(bsz, t // tt),
        in_specs=[pl.BlockSpec((1, tt, RWKV_COLS), lambda b, i: (b, i, 0)),
                  pl.BlockSpec((1, SUBLANES, RWKV_COLS),
                               lambda b, i: (b, jnp.maximum(i * (tt // SUBLANES) - 1, 0), 0)),
                  pl.BlockSpec((1, 1, RWKV_COLS), lambda b, i: (b, 0, 0))]
                 + [full(c) for c in consts],
        out_specs=[out_spec] * 8,
        out_shape=[jax.ShapeDtypeStruct((bsz, t, MAIN_WIDTH), f32)] * 8,
        compiler_params=_cp(("parallel", "arbitrary")),
        name="rwkv_prep",
    )(proj, proj, shift_prev.reshape(bsz, 1, RWKV_COLS), *consts)


def _group_norm_gate(y, bonus, g, lnw, lnb, gmat):
    mean = _head_sum(y, gmat) * (1.0 / HEAD_DIM)
    yc = y - mean
    var = _head_sum(yc * yc, gmat) * (1.0 / HEAD_DIM)
    return (yc * lax.rsqrt(var + LNX_EPS) * lnw + lnb + bonus) * g


def _wkv_chunk_kernel(r_ref, lw_ref, k_ref, v_ref, a_ref, b_ref, bonus_ref, g_ref, lnw_ref, lnb_ref,
                      o_ref, s_ref, h_ref, *, chunks_per_iter):
    i = pl.program_id(2)
    c = CHUNK
    tt = r_ref.shape[1]

    @pl.when(i == 0)
    def _():
        h_ref[...] = jnp.zeros_like(h_ref)

    gmat = _group_ones()
    bd = gmat > 0.5
    ltri = jnp.where(_iota((c, c), 1) <= _iota((c, c), 0), 1.0, 0.0).astype(f32)
    row = _iota((LANES, LANES), 0)
    col = _iota((LANES, LANES), 1)
    t_row = row & (c - 1)
    s_col = col & (c - 1)
    m_strict = s_col < t_row
    m_incl = s_col <= t_row
    top = row < c
    lane_lo = _iota((1, LANES), 1) < HEAD_DIM
    eye = row == col
    lnw = lnw_ref[...]
    lnb = lnb_ref[...]
    zeros = jnp.zeros((c, LANES), f32)
    mm = lambda x: x.astype(bf16)

    def body(ci, carry):
        sls = [pl.ds(pl.multiple_of((ci * chunks_per_iter + j) * c, c), c) for j in range(chunks_per_iter)]
        n = range(chunks_per_iter)
        vs = [v_ref[0, sl, :] for sl in sls]
        cums = [_dot(ltri, lw_ref[0, sl, :], HI) for sl in sls]
        tots = [cum[c - 1:c, :] for cum in cums]
        ats = [a_ref[0, sl, :] * jnp.exp(cum - lw_ref[0, sl, :]) for sl, cum in zip(sls, cums)]
        rts = [r_ref[0, sl, :] * jnp.exp(cum) for sl, cum in zip(sls, cums)]
        bks, bkts = [], []
        for sl, cum, tot in zip(sls, cums, tots):
            b, k = b_ref[0, sl, :], k_ref[0, sl, :]
            e_neg, e_rem = jnp.exp(-cum), jnp.exp(tot - cum)
            bks.append(mm(jnp.concatenate([b * e_neg, k * e_neg], axis=0)))
            bkts.append(mm(jnp.concatenate([b * e_rem, k * e_rem], axis=0)))
        lhss = [mm(jnp.concatenate([jnp.where(lane_lo, at, 0.0), jnp.where(lane_lo, 0.0, at),
                                    jnp.where(lane_lo, rt, 0.0), jnp.where(lane_lo, 0.0, rt)], axis=0))
                for at, rt in zip(ats, rts)]
        gms = [_dot_nt(lhss[j], bks[j]) for j in n]
        aaks = [jnp.where(m_strict, gm[:LANES], 0.0) for gm in gms]
        arks = [mm(jnp.where(m_incl, gm[LANES:], 0.0)) for gm in gms]
        aak_sws = [pltpu.roll(aak, HEAD_DIM, 1) for aak in aaks]
        ms = [jnp.where(bd, jnp.where(top, aaks[j], aak_sws[j]), 0.0) for j in n]
        aks = [mm(jnp.where(bd, jnp.where(top, aak_sws[j], aaks[j]), 0.0)) for j in n]
        vvs = [mm(jnp.concatenate([pltpu.roll(v, HEAD_DIM, 1)] * 2, axis=0)) for v in vs]
        akvs = [_dot(aks[j], vvs[j]) for j in n]
        xs = [jnp.where(bd, jnp.concatenate([ats[j], ats[j]], axis=0), akvs[j]) for j in n]
        for lvl in range(6):
            if lvl < 5:
                zs = [_dot(mm(ms[j]), mm(jnp.concatenate([xs[j], ms[j]], axis=1))) for j in n]
                xs = [xs[j] + zs[j][:, :LANES] for j in n]
                ms = [zs[j][:, LANES:] for j in n]
            else:
                zs = [_dot(mm(ms[j]), mm(xs[j])) for j in n]
                xs = [xs[j] + zs[j] for j in n]
        rhss = []
        for j in n:
            x = xs[j]
            ta = jnp.where(lane_lo, x[:c], x[c:])
            uv = pltpu.roll(jnp.where(lane_lo, x[c:], x[:c]), HEAD_DIM, 1)
            rhss.append(mm(jnp.concatenate([jnp.concatenate([ta, uv], axis=1),
                                            jnp.concatenate([zeros, vs[j]], axis=1)], axis=0)))
        z2s = [_dot(arks[j], rhss[j]) for j in n]
        pzs = [_dot_tn(bkts[j], rhss[j]) for j in n]
        qts = [mm(rts[j] + jnp.where(lane_lo, z2s[j][:c, :LANES], z2s[j][c:, :LANES])) for j in n]
        yvs = [jnp.where(lane_lo, z2s[j][:c, LANES:], z2s[j][c:, LANES:]) for j in n]
        pps = [mm(jnp.where(bd, pzs[j][:, :LANES], 0.0)) for j in n]
        hvs = [jnp.where(bd, pzs[j][:, LANES:], 0.0) for j in n]
        tot_cols = [jnp.sum(jnp.where(eye, jnp.exp(tots[j]), 0.0), axis=1, keepdims=True) for j in n]
        h = h_ref[...]
        ys = []
        for j in n:
            hb = mm(h)
            ys.append(_dot(qts[j], hb) + yvs[j])
            h = h * tot_cols[j] + _dot(pps[j], hb) + hvs[j]
        h_ref[...] = h
        means = [_head_sum(y, gmat) * (1.0 / HEAD_DIM) for y in ys]
        ycs = [ys[j] - means[j] for j in n]
        vrs = [_head_sum(yc * yc, gmat) * (1.0 / HEAD_DIM) for yc in ycs]
        for j in n:
            yn = ycs[j] * lax.rsqrt(vrs[j] + LNX_EPS) * lnw + lnb
            o_ref[0, sls[j], :] = (yn + bonus_ref[0, sls[j], :]) * g_ref[0, sls[j], :]
        return carry

    lax.fori_loop(0, tt // (c * chunks_per_iter), body, 0)

    @pl.when(i == pl.num_programs(2) - 1)
    def _():
        s_ref[0, 0] = h_ref[...].T


def _wkv_chunked(r, lw, k, v, a, b, bonus, g, lnx_w, lnx_b, tt):
    bsz, t, _ = r.shape
    tt = min(tt, t)
    spec = pl.BlockSpec((1, tt, LANES), lambda bb, hp, i: (bb, i, hp))
    vec = pl.BlockSpec((1, LANES), lambda bb, hp, i: (0, hp))
    out, st = pl.pallas_call(
        functools.partial(_wkv_chunk_kernel, chunks_per_iter=min(8, tt // CHUNK)),
        grid=(bsz, N_SLABS, t // tt),
        in_specs=[spec] * 8 + [vec, vec],
        out_specs=[spec, pl.BlockSpec((1, 1, LANES, LANES), lambda bb, hp, i: (bb, hp, 0, 0))],
        out_shape=[jax.ShapeDtypeStruct((bsz, t, MAIN_WIDTH), f32),
                   jax.ShapeDtypeStruct((bsz, N_SLABS, LANES, LANES), f32)],
        scratch_shapes=[pltpu.VMEM((LANES, LANES), f32)],
        compiler_params=_cp(("parallel", "parallel", "arbitrary")),
        name="wkv_chunked",
    )(r, lw, k, v, a, b, bonus, g, lnx_w.reshape(1, -1), lnx_b.reshape(1, -1))
    st = jnp.stack([st[:, :, :HEAD_DIM, :HEAD_DIM], st[:, :, HEAD_DIM:, HEAD_DIM:]], axis=2)
    return out, st.reshape(bsz, RWKV_HEADS, HEAD_DIM, HEAD_DIM)


def _wkv_steps_kernel(s_ref, r_ref, lw_ref, k_ref, v_ref, a_ref, b_ref, bonus_ref, g_ref, lnw_ref, lnb_ref,
                      o_ref, so_ref, *, n_steps):
    s = s_ref[0]
    eye = (_iota((HEAD_DIM, HEAD_DIM), 0) == _iota((HEAD_DIM, HEAD_DIM), 1))[None]
    lnw = lnw_ref[...]
    lnb = lnb_ref[...]
    o_ref[...] = jnp.zeros_like(o_ref)
    for t in range(n_steps):
        sl = slice(t, t + 1)
        r, k, v = r_ref[0, :, sl, :], k_ref[0, :, sl, :], v_ref[0, :, sl, :]
        a, b = a_ref[0, :, sl, :], b_ref[0, :, sl, :]
        w = jnp.exp(lw_ref[0, :, sl, :])
        sa = jnp.sum(s * a, axis=2, keepdims=True)
        v_col = jnp.sum(jnp.where(eye, v, 0.0), axis=2, keepdims=True)
        s = s * w + sa * b + v_col * k
        y_col = jnp.sum(s * r, axis=2, keepdims=True)
        y = jnp.sum(jnp.where(eye, y_col, 0.0), axis=1, keepdims=True)
        mean = jnp.mean(y, axis=2, keepdims=True)
        yc = y - mean
        var = jnp.mean(yc * yc, axis=2, keepdims=True)
        yn = yc * lax.rsqrt(var + LNX_EPS) * lnw + lnb
        o_ref[0, :, sl, :] = (yn + bonus_ref[0, :, sl, :]) * g_ref[0, :, sl, :]
    so_ref[0] = s


def _wkv_steps(state, seqs, lnx_w, lnx_b, n_steps):
    bsz, tp, _ = seqs[0].shape
    hm = lambda x: x.reshape(bsz, tp, RWKV_HEADS, HEAD_DIM).transpose(0, 2, 1, 3)
    spec = pl.BlockSpec((1, RWKV_HEADS, tp, HEAD_DIM), lambda bb: (bb, 0, 0, 0))
    sspec = pl.BlockSpec((1, RWKV_HEADS, HEAD_DIM, HEAD_DIM), lambda bb: (bb, 0, 0, 0))
    vec = pl.BlockSpec((RWKV_HEADS, 1, HEAD_DIM), lambda bb: (0, 0, 0))
    out, new_state = pl.pallas_call(
        functools.partial(_wkv_steps_kernel, n_steps=n_steps),
        grid=(bsz,),
        in_specs=[sspec] + [spec] * 8 + [vec, vec],
        out_specs=[spec, sspec],
        out_shape=[jax.ShapeDtypeStruct((bsz, RWKV_HEADS, tp, HEAD_DIM), f32),
                   jax.ShapeDtypeStruct(state.shape, f32)],
        compiler_params=_cp(("parallel",)),
        name="wkv_steps",
    )(state, *[hm(x) for x in seqs],
      lnx_w.reshape(RWKV_HEADS, 1, HEAD_DIM), lnx_b.reshape(RWKV_HEADS, 1, HEAD_DIM))
    return out.transpose(0, 2, 1, 3).reshape(bsz, tp, MAIN_WIDTH), new_state


def _mem_attn_kernel(q_ref, g_ref, k_ref, v_ref, o_ref):
    gmat = _group_ones()
    q = q_ref[0]
    tq = q.shape[0]
    qn = q * lax.rsqrt(_head_sum(q * q, gmat) * (1.0 / HEAD_DIM) + NORM_EPS) * g_ref[...]
    head = _iota((1, MEM_WIDTH), 1) // HEAD_DIM
    qs = jnp.concatenate([jnp.where(head == h, qn, 0.0) for h in range(MEM_HEADS)], axis=0)
    s = _dot_nt(qs.astype(bf16), k_ref[0].astype(bf16)) * ATTN_SCALE
    e = jnp.exp(s - jnp.max(s, axis=-1, keepdims=True))
    o4 = _dot(e.astype(bf16), v_ref[0].astype(bf16)) / jnp.sum(e, axis=-1, keepdims=True)
    o = jnp.zeros((tq, MEM_WIDTH), f32)
    for h in range(MEM_HEADS):
        o = o + jnp.where(head == h, o4[h * tq:(h + 1) * tq], 0.0)
    o_ref[0] = o


def _mem_attn(proj, col_block, g_qnorm, mem_k, mem_v, tq):
    bsz, t, _ = proj.shape
    tq = min(tq, t)
    g = jnp.tile(g_qnorm.reshape(1, HEAD_DIM), (1, MEM_HEADS))
    return pl.pallas_call(
        _mem_attn_kernel,
        grid=(bsz, t // tq),
        in_specs=[pl.BlockSpec((1, tq, MEM_WIDTH), lambda b, i: (b, i, col_block)),
                  pl.BlockSpec((1, MEM_WIDTH), lambda b, i: (0, 0)),
                  pl.BlockSpec((1, N_MEM, MEM_WIDTH), lambda b, i: (b, 0, 0)),
                  pl.BlockSpec((1, N_MEM, MEM_WIDTH), lambda b, i: (b, 0, 0))],
        out_specs=pl.BlockSpec((1, tq, MEM_WIDTH), lambda b, i: (b, i, 0)),
        out_shape=jax.ShapeDtypeStruct((bsz, t, MEM_WIDTH), f32),
        compiler_params=_cp(("parallel", "parallel")),
        name="mem_attn",
    )(proj, g, mem_k, mem_v)


def _swa_kernel(sink_ref, q_ref, g_ref, cos_ref, sin_ref, kp_ref, kc_ref, vp_ref, vc_ref, o_ref, qn_ref,
                *, first_block_has_no_past):
    i = pl.program_id(1)
    gmat = _group_ones()
    tq = q_ref.shape[1]
    for j in range(N_SLABS):
        sl = slice(j * LANES, (j + 1) * LANES)
        x = q_ref[0, :, sl]
        xn = x * lax.rsqrt(_head_sum(x * x, gmat) * (1.0 / HEAD_DIM) + NORM_EPS) * g_ref[...]
        qn_ref[:, sl] = xn * cos_ref[...] + _rot_half(xn) * sin_ref[...]
    kc, vc = kc_ref[0], vc_ref[0]
    if tq < BLOCK:
        pad = jnp.zeros((BLOCK - tq, KV_WIDTH), f32)
        kc = jnp.concatenate([kc, pad], axis=0)
        vc = jnp.concatenate([vc, pad], axis=0)
    keys = jnp.concatenate([kp_ref[0], kc], axis=0).astype(bf16)
    vals = jnp.concatenate([vp_ref[0], vc], axis=0).astype(bf16)
    row = _iota((tq, 2 * BLOCK), 0)
    col = _iota((tq, 2 * BLOCK), 1)
    mask = (col > row) & (col <= row + WINDOW)
    if first_block_has_no_past:
        mask = mask & ((i > 0) | (col >= BLOCK))
    for hk in range(SWA_KV_HEADS):
        k_h = keys[:, hk * HEAD_DIM:(hk + 1) * HEAD_DIM]
        v_h = vals[:, hk * HEAD_DIM:(hk + 1) * HEAD_DIM]
        for gq in range(SWA_GROUP):
            hq = hk * SWA_GROUP + gq
            sl = slice(hq * HEAD_DIM, (hq + 1) * HEAD_DIM)
            s = _dot_nt(qn_ref[:, sl].astype(bf16), k_h) * ATTN_SCALE
            s = jnp.where(mask, s, NEG_BIG)
            sink = sink_ref[hq]
            m = jnp.maximum(jnp.max(s, axis=-1, keepdims=True), sink)
            e = jnp.exp(s - m)
            denom = jnp.sum(e, axis=-1, keepdims=True) + jnp.exp(sink - m)
            o_ref[0, :, sl] = _dot(e.astype(bf16), v_h) / denom


def _swa(proj, g_qnorm, sinks, cos, sin, k_prev, k_cur, v_prev, v_cur, tq, prev_index, first_block_has_no_past):
    bsz, t, _ = proj.shape
    g = jnp.tile(g_qnorm.reshape(1, HEAD_DIM), (1, HEADS_PER_SLAB))
    nq = t // tq
    return pl.pallas_call(
        functools.partial(_swa_kernel, first_block_has_no_past=first_block_has_no_past),
        grid=(bsz, nq),
        in_specs=[pl.BlockSpec(memory_space=pltpu.SMEM),
                  pl.BlockSpec((1, tq, MAIN_WIDTH), lambda b, i: (b, i, 0)),
                  pl.BlockSpec((1, LANES), lambda b, i: (0, 0)),
                  pl.BlockSpec((tq, LANES), lambda b, i: (i, 0)),
                  pl.BlockSpec((tq, LANES), lambda b, i: (i, 0)),
                  pl.BlockSpec((1, BLOCK, KV_WIDTH), lambda b, i: (b, prev_index(i), 0)),
                  pl.BlockSpec((1, tq, KV_WIDTH), lambda b, i: (b, i, 0)),
                  pl.BlockSpec((1, BLOCK, KV_WIDTH), lambda b, i: (b, prev_index(i), 0)),
                  pl.BlockSpec((1, tq, KV_WIDTH), lambda b, i: (b, i, 0))],
        out_specs=pl.BlockSpec((1, tq, MAIN_WIDTH), lambda b, i: (b, i, 0)),
        out_shape=jax.ShapeDtypeStruct((bsz, t, MAIN_WIDTH), f32),
        scratch_shapes=[pltpu.VMEM((tq, MAIN_WIDTH), f32)],
        compiler_params=_cp(("parallel", "parallel")),
        name="swa_attn",
    )(sinks, proj, g, cos, sin, k_prev, k_cur, v_prev, v_cur)


def _out_mlp_kernel(h_ref, main_ref, mem_ref, woa_ref, wob_ref, g_ref, wup_ref, wdn_ref, o_ref, *, tf):
    h1 = (h_ref[...] + _dot(main_ref[...].astype(bf16), woa_ref[...])
          + _dot(mem_ref[...].astype(bf16), wob_ref[...]))
    hn = (h1 * lax.rsqrt(jnp.mean(h1 * h1, axis=-1, keepdims=True) + NORM_EPS) * g_ref[...]).astype(bf16)
    acc = None
    for j in range(D_FF // tf):
        u = _dot(hn, wup_ref[:, j * tf:(j + 1) * tf])
        u = jnp.square(jnp.maximum(u, 0.0)).astype(bf16)
        d = _dot(u, wdn_ref[j * tf:(j + 1) * tf, :])
        acc = d if acc is None else acc + d
    o_ref[...] = h1 + acc


def _out_mlp(h, main, mem_o, w_out_bf16, g_mlp, w_up_bf16, w_down_bf16, tm, tf=1024):
    n = h.shape[0]
    tm = min(tm, n)
    rows = lambda w: pl.BlockSpec((tm, w), lambda i: (i, 0))
    resident = lambda a: pl.BlockSpec(a.shape, lambda i: (0, 0))
    woa, wob = w_out_bf16[:MAIN_WIDTH], w_out_bf16[MAIN_WIDTH:]
    g = g_mlp.reshape(1, D_MODEL)
    return pl.pallas_call(
        functools.partial(_out_mlp_kernel, tf=tf),
        grid=(n // tm,),
        in_specs=[rows(D_MODEL), rows(MAIN_WIDTH), rows(MEM_WIDTH),
                  resident(woa), resident(wob), resident(g), resident(w_up_bf16), resident(w_down_bf16)],
        out_specs=rows(D_MODEL),
        out_shape=jax.ShapeDtypeStruct((n, D_MODEL), f32),
        compiler_params=_cp(("parallel",)),
        name="out_mlp",
    )(h, main, mem_o, woa, wob, g, w_up_bf16, w_down_bf16)


def _rope_tables(pos, reps):
    half = HEAD_DIM // 2
    freqs = jnp.power(ROPE_THETA, -jnp.arange(half, dtype=f32) / half)
    ang = pos.astype(f32)[:, None] * freqs[None, :]
    cos, sin = jnp.cos(ang), jnp.sin(ang)
    cos_h = jnp.concatenate([cos, cos], axis=-1)
    sin_h = jnp.concatenate([-sin, sin], axis=-1)
    return jnp.tile(cos_h, (1, reps)), jnp.tile(sin_h, (1, reps))


def _trunk(x, pos, mem_k, mem_v, shift0, wkv0, swa_cache, t_real, W):
    bsz, tp, _ = x.shape
    n = bsz * tp
    prompt = swa_cache is None
    tm = 512
    flat = lambda a: a.reshape(n, a.shape[-1])
    unflat = lambda a: a.reshape(bsz, tp, a.shape[-1])
    cos4, sin4 = _rope_tables(pos, KV_WIDTH // HEAD_DIM)
    cos2, sin2 = cos4[:, :LANES], sin4[:, :LANES]

    proj = unflat(_norm_matmul(flat(x), W["norm_mix"][0], W["w_in_a"], tm))
    new_shift = proj[:, t_real - 1, :RWKV_COLS]
    seqs = _rwkv_prep(proj, shift0, W["shift_mu"], W["w_w2"], W["w0"], W["w_a2"], W["a0"], W["w_g2"],
                      W["k_k"], W["k_a"], W["r_k"], tt=256)
    if prompt:
        main, new_wkv = _wkv_chunked(*seqs, W["lnx_w"], W["lnx_b"], tt=512)
    else:
        main, new_wkv = _wkv_steps(wkv0, seqs, W["lnx_w"], W["lnx_b"], t_real)
    mem_o = _mem_attn(proj, RWKV_COLS // MEM_WIDTH, W["mem_qnorm"][0], mem_k[0], mem_v[0], tq=256)
    h = _out_mlp(flat(x), flat(main), flat(mem_o), W["w_out"][0], W["norm_mlp"][0], W["w_up"][0],
                 W["w_down"][0], tm)

    kv = _norm_matmul(h, W["kv_norm"], W["w_kv"], tm)
    if prompt:
        k_sh, v_sh = _kv_post(kv, W["swa_knorm"], cos4, sin4, tm, True)
    else:
        k_sh, v_sh = _kv_post(kv, W["swa_knorm"], jnp.tile(cos4, (tm // tp, 1)), jnp.tile(sin4, (tm // tp, 1)),
                              tm, True)
    k_sh, v_sh = unflat(k_sh), unflat(v_sh)

    proj = unflat(_norm_matmul(h, W["norm_mix"][1], W["w_in_b"], tm))
    if prompt:
        main = _swa(proj, W["swa_qnorm"], W["sinks"], cos2, sin2, k_sh, k_sh, v_sh, v_sh, BLOCK,
                    lambda i: jnp.maximum(i - 1, 0), True)
        win = min(WINDOW, tp)
        k_state, v_state = k_sh[:, tp - win:], v_sh[:, tp - win:]
    else:
        ck, cv = swa_cache
        main = _swa(proj, W["swa_qnorm"], W["sinks"], cos2, sin2, ck, k_sh, cv, v_sh, tp, lambda i: 0, False)
        k_state = jnp.concatenate([ck[:, t_real:], k_sh[:, :t_real]], axis=1)
        v_state = jnp.concatenate([cv[:, t_real:], v_sh[:, :t_real]], axis=1)
    mem_o = _mem_attn(proj, MAIN_WIDTH // MEM_WIDTH, W["mem_qnorm"][1], mem_k[1], mem_v[1], tq=256)
    y = _out_mlp(h, flat(main), flat(mem_o), W["w_out"][1], W["norm_mlp"][1], W["w_up"][1], W["w_down"][1], tm)
    heads = lambda a: a.reshape(bsz, a.shape[1], SWA_KV_HEADS, HEAD_DIM)
    return unflat(y)[:, :t_real], new_shift[None], new_wkv[None], heads(k_state), heads(v_state)


def kernel(x_prompt, x_sample, state_rwkv_shift, state_rwkv_wkv, cache_swa_k, cache_swa_v, cache_mem_k,
           cache_mem_v, mem_prompt, norm_mix, norm_mlp, w_out, w_up, w_down, mem_norm, w_mem_kv, mem_qnorm,
           mem_knorm, w_in_a, shift_mu, w_w2, w0, w_a2, a0, w_g2, k_k, k_a, r_k, lnx_w, lnx_b, w_in_b,
           swa_qnorm, sinks, kv_norm, w_kv, swa_knorm):
    W = dict(norm_mix=norm_mix, norm_mlp=norm_mlp, w_out=w_out.astype(bf16), w_up=w_up.astype(bf16),
             w_down=w_down.astype(bf16), mem_qnorm=mem_qnorm, w_in_a=w_in_a[0].astype(bf16),
             shift_mu=shift_mu[0], w_w2=w_w2[0], w0=w0[0], w_a2=w_a2[0], a0=a0[0], w_g2=w_g2[0],
             k_k=k_k[0], k_a=k_a[0], r_k=r_k[0].reshape(-1), lnx_w=lnx_w[0], lnx_b=lnx_b[0],
             w_in_b=w_in_b[0].astype(bf16), swa_qnorm=swa_qnorm[0], sinks=sinks[0], kv_norm=kv_norm,
             w_kv=w_kv.astype(bf16), swa_knorm=swa_knorm)
    bp, tp, _ = x_prompt.shape
    bs, ts, _ = x_sample.shape
    depth = norm_mix.shape[0]

    mem_flat = mem_prompt.reshape(bp * N_MEM, D_MODEL)
    ones = jnp.ones((N_MEM, KV_WIDTH), f32)
    p_mem_k, p_mem_v = [], []
    for l in range(depth):
        kv = _norm_matmul(mem_flat, mem_norm[l], w_mem_kv[l].astype(bf16), 512)
        mk, mv = _kv_post(kv, mem_knorm[l], ones, ones, N_MEM, False)
        p_mem_k.append(mk.reshape(bp, N_MEM, MEM_WIDTH))
        p_mem_v.append(mv.reshape(bp, N_MEM, MEM_WIDTH))
    y_p, p_shift, p_wkv, p_k, p_v = _trunk(
        x_prompt, jnp.arange(tp), p_mem_k, p_mem_v, jnp.zeros((bp, RWKV_COLS), f32), None, None, tp, W)
    mem_heads = lambda a: jnp.stack(a).reshape(depth, bp, N_MEM, MEM_HEADS, HEAD_DIM)

    tpad = -(-ts // SUBLANES) * SUBLANES
    x_s = jnp.pad(x_sample, ((0, 0), (0, tpad - ts), (0, 0)))
    win = cache_swa_k.shape[1]
    y_s, s_shift, s_wkv, s_k, s_v = _trunk(
        x_s, PAST_LEN + jnp.arange(tpad), cache_mem_k.reshape(depth, bs, N_MEM, MEM_WIDTH),
        cache_mem_v.reshape(depth, bs, N_MEM, MEM_WIDTH), state_rwkv_shift[0], state_rwkv_wkv[0],
        (cache_swa_k.reshape(bs, win, KV_WIDTH), cache_swa_v.reshape(bs, win, KV_WIDTH)), ts, W)

    return (y_p, y_s, p_shift, p_wkv, p_k, p_v, mem_heads(p_mem_k), mem_heads(p_mem_v),
            s_shift, s_wkv, s_k, s_v)
```

```python
import functools

import numpy as np
import jax
import jax.numpy as jnp
from jax import lax
from jax.experimental import pallas as pl
from jax.experimental.pallas import tpu as pltpu

f32 = jnp.float32
bf16 = jnp.bfloat16
HI = lax.Precision.HIGHEST

D_MODEL = 1024
HEAD_DIM = 64
MEM_HEADS = 4
MEM_WIDTH = MEM_HEADS * HEAD_DIM
MAIN_WIDTH = D_MODEL - MEM_WIDTH
RWKV_HEADS = MAIN_WIDTH // HEAD_DIM
DECAY_LORA = 64
AAA_LORA = 64
GATE_LORA = 128
RWKV_COLS = 3 * MAIN_WIDTH + DECAY_LORA + AAA_LORA + GATE_LORA
A_IN_COLS = RWKV_COLS + MEM_WIDTH
SWA_Q_HEADS = MAIN_WIDTH // HEAD_DIM
SWA_KV_HEADS = 4
SWA_GROUP = SWA_Q_HEADS // SWA_KV_HEADS
KV_WIDTH = SWA_KV_HEADS * HEAD_DIM
WINDOW = 128
BLOCK = 128
N_MEM = 256
D_FF = 4 * D_MODEL
PAST_LEN = 16384
ROPE_THETA = 10000.0
NORM_EPS = 1e-6
LNX_EPS = 6.4e-4
L2_EPS = 1e-12
ATTN_SCALE = HEAD_DIM ** -0.5

LANES = 128
SUBLANES = 8
HEADS_PER_SLAB = LANES // HEAD_DIM
N_SLABS = MAIN_WIDTH // LANES
CHUNK = 64
VMEM_LIMIT = 56 * 1024 * 1024
NEG_BIG = -1e30


def _cp(sem, vmem=VMEM_LIMIT):
    return pltpu.CompilerParams(dimension_semantics=sem, vmem_limit_bytes=vmem)


def _dot(a, b, precision=None):
    return jnp.dot(a, b, precision=precision, preferred_element_type=f32)


def _dot_nt(a, b, precision=None):
    return lax.dot_general(a, b, (((1,), (1,)), ((), ())), precision=precision,
                           preferred_element_type=f32)


def _dot_tn(a, b, precision=None):
    return lax.dot_general(a, b, (((0,), (0,)), ((), ())), precision=precision,
                           preferred_element_type=f32)


def _iota(shape, dim):
    return lax.broadcasted_iota(jnp.int32, shape, dim)


def _group_ones():
    r = _iota((LANES, LANES), 0)
    c = _iota((LANES, LANES), 1)
    return jnp.where((r < HEAD_DIM) == (c < HEAD_DIM), 1.0, 0.0).astype(f32)


def _head_sum(x, gmat):
    w = x.shape[-1]
    gb = gmat.astype(bf16)
    hi = x.astype(bf16)
    lo = (x - hi.astype(f32)).astype(bf16)
    parts = [_dot(hi[:, j * LANES:(j + 1) * LANES], gb) + _dot(lo[:, j * LANES:(j + 1) * LANES], gb)
             for j in range(w // LANES)]
    return parts[0] if len(parts) == 1 else jnp.concatenate(parts, axis=-1)


def _sigmoid(x):
    return 1.0 / (1.0 + jnp.exp(-x))


def _softplus(x):
    return jnp.maximum(x, 0.0) + jnp.log1p(jnp.exp(-jnp.abs(x)))


def _rot_half(x):
    lane = _iota(x.shape, 1)
    first = (lane & (HEAD_DIM - 1)) < (HEAD_DIM // 2)
    return jnp.where(first, pltpu.roll(x, LANES - HEAD_DIM // 2, 1), pltpu.roll(x, HEAD_DIM // 2, 1))


def _norm_matmul_kernel(x_ref, g_ref, w_ref, o_ref):
    x = x_ref[...]
    xn = x * lax.rsqrt(jnp.mean(x * x, axis=-1, keepdims=True) + NORM_EPS) * g_ref[...]
    o_ref[...] = _dot(xn.astype(bf16), w_ref[...])


def _norm_matmul(x, g, w_bf16, tm):
    n, d = x.shape
    m = w_bf16.shape[1]
    tm = min(tm, n)
    return pl.pallas_call(
        _norm_matmul_kernel,
        grid=(n // tm,),
        in_specs=[pl.BlockSpec((tm, d), lambda i: (i, 0)),
                  pl.BlockSpec((1, d), lambda i: (0, 0)),
                  pl.BlockSpec((d, m), lambda i: (0, 0))],
        out_specs=pl.BlockSpec((tm, m), lambda i: (i, 0)),
        out_shape=jax.ShapeDtypeStruct((n, m), f32),
        compiler_params=_cp(("parallel",)),
        name="norm_matmul",
    )(x, g.reshape(1, d), w_bf16)


def _kv_post_kernel(kv_ref, g_ref, cos_ref, sin_ref, k_ref, v_ref, *, use_rope):
    gmat = _group_ones()
    x = kv_ref[:, :KV_WIDTH]
    ms = _head_sum(x * x, gmat) * (1.0 / HEAD_DIM)
    xn = x * lax.rsqrt(ms + NORM_EPS) * g_ref[...]
    if use_rope:
        for j in range(KV_WIDTH // LANES):
            sl = slice(j * LANES, (j + 1) * LANES)
            xs = xn[:, sl]
            k_ref[:, sl] = xs * cos_ref[:, sl] + _rot_half(xs) * sin_ref[:, sl]
    else:
        k_ref[...] = xn
    v_ref[...] = kv_ref[:, KV_WIDTH:]


def _kv_post(kv, g_head, cos, sin, tm, use_rope):
    n = kv.shape[0]
    tm = min(tm, n)
    nt = cos.shape[0] // tm
    g = jnp.tile(g_head.reshape(1, HEAD_DIM), (1, KV_WIDTH // HEAD_DIM))
    return pl.pallas_call(
        functools.partial(_kv_post_kernel, use_rope=use_rope),
        grid=(n // tm,),
        in_specs=[pl.BlockSpec((tm, 2 * KV_WIDTH), lambda i: (i, 0)),
                  pl.BlockSpec((1, KV_WIDTH), lambda i: (0, 0)),
                  pl.BlockSpec((tm, KV_WIDTH), lambda i: (i % nt, 0)),
                  pl.BlockSpec((tm, KV_WIDTH), lambda i: (i % nt, 0))],
        out_specs=[pl.BlockSpec((tm, KV_WIDTH), lambda i: (i, 0)),
                   pl.BlockSpec((tm, KV_WIDTH), lambda i: (i, 0))],
        out_shape=[jax.ShapeDtypeStruct((n, KV_WIDTH), f32)] * 2,
        compiler_params=_cp(("parallel",)),
        name="kv_post",
    )(kv, g, cos, sin)


def _rwkv_prep_kernel(p_ref, pprev_ref, shift_ref, mu_ref, ww2_ref, w0_ref, wa2_ref, a0_ref, wg2_ref,
                      kk_ref, ka_ref, rk_ref,
                      r_o, lw_o, k_o, v_o, a_o, b_o, bonus_o, g_o, *, packed_len):
    i = pl.program_id(1)
    gmat = _group_ones()
    pf = p_ref[0]
    row = _iota(pf.shape, 0)
    if packed_len is None:
        first_prev = jnp.where(i == 0, shift_ref[0], pprev_ref[0][SUBLANES - 1:SUBLANES, :])
        prev = jnp.where(row == 0, first_prev, pltpu.roll(pf, 1, 0))
    else:
        prev = jnp.where((row & (packed_len - 1)) == 0, pltpu.roll(pf, pf.shape[0] - (packed_len - 1), 0),
                         pltpu.roll(pf, 1, 0))
    ps = pf + (prev - pf) * mu_ref[...]
    i1, i2, i3 = MAIN_WIDTH, 2 * MAIN_WIDTH, 3 * MAIN_WIDTH
    i4, i5 = i3 + DECAY_LORA, i3 + DECAY_LORA + AAA_LORA
    r, k, v = ps[:, :i1], ps[:, i1:i2], ps[:, i2:i3]
    wd, ad, gd = ps[:, i3:i4], ps[:, i4:i5], ps[:, i5:]
    w_log = -_softplus(-(w0_ref[...] + _dot(jnp.tanh(wd), ww2_ref[...], HI))) - 0.5
    lw = -jnp.exp(w_log)
    a_sig = _sigmoid(a0_ref[...] + _dot(ad, wa2_ref[...], HI))
    g = _dot(_sigmoid(gd), wg2_ref[...], HI)
    kk = k * kk_ref[...]
    kk = kk / jnp.maximum(jnp.sqrt(_head_sum(kk * kk, gmat)), L2_EPS)
    k2 = k * (1.0 + (a_sig - 1.0) * ka_ref[...])
    r_o[0] = r
    lw_o[0] = lw
    k_o[0] = k2
    v_o[0] = v
    a_o[0] = -kk
    b_o[0] = kk * a_sig
    bonus_o[0] = _head_sum(r * k2 * rk_ref[...], gmat) * v
    g_o[0] = g


def _rwkv_prep(proj, shift_prev, mu, w_w2, w0, w_a2, a0, w_g2, k_k, k_a, r_k, tt, packed_len=None):
    bsz, t, _ = proj.shape
    tt = min(tt, t)
    assert packed_len is None or (packed_len & (packed_len - 1) == 0 and tt % packed_len == 0)
    row = lambda x: x.reshape(1, -1)
    full = lambda a: pl.BlockSpec(a.shape, lambda b, i: (0,) * a.ndim)
    consts = [row(mu), w_w2, row(w0), w_a2, row(a0), w_g2, row(k_k), row(k_a), row(r_k)]
    out_spec = pl.BlockSpec((1, tt, MAIN_WIDTH), lambda b, i: (b, i, 0))
    return pl.pallas_call(
        functools.partial(_rwkv_prep_kernel, packed_len=packed_len),
        grid=(bsz, t // tt),
        in_specs=[pl.BlockSpec((1, tt, RWKV_COLS), lambda b, i: (b, i, 0)),
                  pl.BlockSpec((1, SUBLANES, RWKV_COLS),
                               lambda b, i: (b, jnp.maximum(i * (tt // SUBLANES) - 1, 0), 0)),
                  pl.BlockSpec((1, 1, RWKV_COLS), lambda b, i: (b, 0, 0))]
                 + [full(c) for c in consts],
        out_specs=[out_spec] * 8,
        out_shape=[jax.ShapeDtypeStruct((bsz, t, MAIN_WIDTH), f32)] * 8,
        compiler_params=_cp(("parallel", "arbitrary")),
        name="rwkv_prep",
    )(proj, proj, shift_prev.reshape(bsz, 1, RWKV_COLS), *consts)


def _group_norm_gate(y, bonus, g, lnw, lnb, gmat):
    mean = _head_sum(y, gmat) * (1.0 / HEAD_DIM)
    yc = y - mean
    var = _head_sum(yc * yc, gmat) * (1.0 / HEAD_DIM)
    return (yc * lax.rsqrt(var + LNX_EPS) * lnw + lnb + bonus) * g


def _wkv_chunk_kernel(r_ref, lw_ref, k_ref, v_ref, a_ref, b_ref, bonus_ref, g_ref, lnw_ref, lnb_ref,
                      o_ref, s_ref, h_ref, *, chunks_per_iter):
    i = pl.program_id(2)
    c = CHUNK
    tt = r_ref.shape[1]

    @pl.when(i == 0)
    def _():
        h_ref[...] = jnp.zeros_like(h_ref)

    gmat = _group_ones()
    bd = gmat > 0.5
    ltri = jnp.where(_iota((c, c), 1) <= _iota((c, c), 0), 1.0, 0.0).astype(f32)
    row = _iota((LANES, LANES), 0)
    col = _iota((LANES, LANES), 1)
    t_row = row & (c - 1)
    s_col = col & (c - 1)
    m_strict = s_col < t_row
    m_incl = s_col <= t_row
    top = row < c
    lane_lo = _iota((1, LANES), 1) < HEAD_DIM
    eye = row == col
    lnw = lnw_ref[...]
    lnb = lnb_ref[...]
    zeros = jnp.zeros((c, LANES), f32)
    mm = lambda x: x.astype(bf16)

    def body(ci, carry):
        sls = [pl.ds(pl.multiple_of((ci * chunks_per_iter + j) * c, c), c) for j in range(chunks_per_iter)]
        n = range(chunks_per_iter)
        vs = [v_ref[0, sl, :] for sl in sls]
        cums = [_dot(ltri, lw_ref[0, sl, :], HI) for sl in sls]
        tots = [cum[c - 1:c, :] for cum in cums]
        ats = [a_ref[0, sl, :] * jnp.exp(cum - lw_ref[0, sl, :]) for sl, cum in zip(sls, cums)]
        rts = [r_ref[0, sl, :] * jnp.exp(cum) for sl, cum in zip(sls, cums)]
        bks, bkts = [], []
        for sl, cum, tot in zip(sls, cums, tots):
            b, k = b_ref[0, sl, :], k_ref[0, sl, :]
            e_neg, e_rem = jnp.exp(-cum), jnp.exp(tot - cum)
            bks.append(mm(jnp.concatenate([b * e_neg, k * e_neg], axis=0)))
            bkts.append(mm(jnp.concatenate([b * e_rem, k * e_rem], axis=0)))
        lhss = [mm(jnp.concatenate([jnp.where(lane_lo, at, 0.0), jnp.where(lane_lo, 0.0, at),
                                    jnp.where(lane_lo, rt, 0.0), jnp.where(lane_lo, 0.0, rt)], axis=0))
                for at, rt in zip(ats, rts)]
        gms = [_dot_nt(lhss[j], bks[j]) for j in n]
        aaks = [jnp.where(m_strict, gm[:LANES], 0.0) for gm in gms]
        arks = [mm(jnp.where(m_incl, gm[LANES:], 0.0)) for gm in gms]
        aak_sws = [pltpu.roll(aak, HEAD_DIM, 1) for aak in aaks]
        ms = [jnp.where(bd, jnp.where(top, aaks[j], aak_sws[j]), 0.0) for j in n]
        aks = [mm(jnp.where(bd, jnp.where(top, aak_sws[j], aaks[j]), 0.0)) for j in n]
        vvs = [mm(jnp.concatenate([pltpu.roll(v, HEAD_DIM, 1)] * 2, axis=0)) for v in vs]
        akvs = [_dot(aks[j], vvs[j]) for j in n]
        xs = [jnp.where(bd, jnp.concatenate([ats[j], ats[j]], axis=0), akvs[j]) for j in n]
        for lvl in range(6):
            if lvl < 5:
                zs = [_dot(mm(ms[j]), mm(jnp.concatenate([xs[j], ms[j]], axis=1))) for j in n]
                xs = [xs[j] + zs[j][:, :LANES] for j in n]
                ms = [zs[j][:, LANES:] for j in n]
            else:
                zs = [_dot(mm(ms[j]), mm(xs[j])) for j in n]
                xs = [xs[j] + zs[j] for j in n]
        rhss = []
        for j in n:
            x = xs[j]
            ta = jnp.where(lane_lo, x[:c], x[c:])
            uv = pltpu.roll(jnp.where(lane_lo, x[c:], x[:c]), HEAD_DIM, 1)
            rhss.append(mm(jnp.concatenate([jnp.concatenate([ta, uv], axis=1),
                                            jnp.concatenate([zeros, vs[j]], axis=1)], axis=0)))
        z2s = [_dot(arks[j], rhss[j]) for j in n]
        pzs = [_dot_tn(bkts[j], rhss[j]) for j in n]
        qts = [mm(rts[j] + jnp.where(lane_lo, z2s[j][:c, :LANES], z2s[j][c:, :LANES])) for j in n]
        yvs = [jnp.where(lane_lo, z2s[j][:c, LANES:], z2s[j][c:, LANES:]) for j in n]
        pps = [mm(jnp.where(bd, pzs[j][:, :LANES], 0.0)) for j in n]
        hvs = [jnp.where(bd, pzs[j][:, LANES:], 0.0) for j in n]
        tot_cols = [jnp.sum(jnp.where(eye, jnp.exp(tots[j]), 0.0), axis=1, keepdims=True) for j in n]
        h = h_ref[...]
        ys = []
        for j in n:
            hb = mm(h)
            ys.append(_dot(qts[j], hb) + yvs[j])
            h = h * tot_cols[j] + _dot(pps[j], hb) + hvs[j]
        h_ref[...] = h
        means = [_head_sum(y, gmat) * (1.0 / HEAD_DIM) for y in ys]
        ycs = [ys[j] - means[j] for j in n]
        vrs = [_head_sum(yc * yc, gmat) * (1.0 / HEAD_DIM) for yc in ycs]
        for j in n:
            yn = ycs[j] * lax.rsqrt(vrs[j] + LNX_EPS) * lnw + lnb
            o_ref[0, sls[j], :] = (yn + bonus_ref[0, sls[j], :]) * g_ref[0, sls[j], :]
        return carry

    lax.fori_loop(0, tt // (c * chunks_per_iter), body, 0)

    @pl.when(i == pl.num_programs(2) - 1)
    def _():
        s_ref[0, 0] = h_ref[...].T


def _wkv_chunked(r, lw, k, v, a, b, bonus, g, lnx_w, lnx_b, tt):
    bsz, t, _ = r.shape
    tt = min(tt, t)
    spec = pl.BlockSpec((1, tt, LANES), lambda bb, hp, i: (bb, i, hp))
    vec = pl.BlockSpec((1, LANES), lambda bb, hp, i: (0, hp))
    out, st = pl.pallas_call(
        functools.partial(_wkv_chunk_kernel, chunks_per_iter=min(8, tt // CHUNK)),
        grid=(bsz, N_SLABS, t // tt),
        in_specs=[spec] * 8 + [vec, vec],
        out_specs=[spec, pl.BlockSpec((1, 1, LANES, LANES), lambda bb, hp, i: (bb, hp, 0, 0))],
        out_shape=[jax.ShapeDtypeStruct((bsz, t, MAIN_WIDTH), f32),
                   jax.ShapeDtypeStruct((bsz, N_SLABS, LANES, LANES), f32)],
        scratch_shapes=[pltpu.VMEM((LANES, LANES), f32)],
        compiler_params=_cp(("parallel", "parallel", "arbitrary")),
        name="wkv_chunked",
    )(r, lw, k, v, a, b, bonus, g, lnx_w.reshape(1, -1), lnx_b.reshape(1, -1))
    st = jnp.stack([st[:, :, :HEAD_DIM, :HEAD_DIM], st[:, :, HEAD_DIM:, HEAD_DIM:]], axis=2)
    return out, st.reshape(bsz, RWKV_HEADS, HEAD_DIM, HEAD_DIM)


def _wkv_steps_kernel(s_ref, seq_ref, lnw_ref, lnb_ref, o_ref, so_ref, w_ref, y_ref, *, n_steps):
    i_r, i_lw, i_k, i_v, i_a, i_b, i_bonus, i_g = range(8)
    w_ref[...] = jnp.exp(seq_ref[i_lw])
    sub = _iota((SUBLANES, LANES), 0)

    def group(vg, carry):
        v0 = pl.multiple_of(vg * SUBLANES, SUBLANES)
        v_rows = [seq_ref[i_v, t, pl.ds(v0, SUBLANES), :] for t in range(n_steps)]
        ys = [jnp.zeros((SUBLANES, LANES), f32) for _ in range(n_steps)]
        for j in range(SUBLANES):
            sv = s_ref[0, v0 + j]
            for t in range(n_steps):
                sa = jnp.sum(sv * seq_ref[i_a, t], axis=0, keepdims=True)
                sv = sv * w_ref[t] + sa * seq_ref[i_b, t] + v_rows[t][j:j + 1, :] * seq_ref[i_k, t]
                y = jnp.sum(sv * seq_ref[i_r, t], axis=0, keepdims=True)
                ys[t] = jnp.where(sub == j, y, ys[t])
            so_ref[0, v0 + j] = sv
        for t in range(n_steps):
            y_ref[t, pl.ds(v0, SUBLANES), :] = ys[t]
        return carry

    lax.fori_loop(0, HEAD_DIM // SUBLANES, group, 0)
    y = y_ref[...]
    mean = jnp.mean(y, axis=1, keepdims=True)
    yc = y - mean
    var = jnp.mean(yc * yc, axis=1, keepdims=True)
    yn = yc * lax.rsqrt(var + LNX_EPS) * lnw_ref[...] + lnb_ref[...]
    o_ref[...] = (yn + seq_ref[i_bonus]) * seq_ref[i_g]


def _wkv_steps(state, seqs, lnx_w, lnx_b, n_steps):
    bsz, tp, _ = seqs[0].shape
    assert bsz % LANES == 0
    state_t = jnp.transpose(state, (1, 2, 3, 0))
    seq_t = jnp.transpose(jnp.stack(seqs)[:, :, :n_steps], (0, 2, 3, 1))
    lanes = lambda x: jnp.broadcast_to(x.reshape(MAIN_WIDTH, 1), (MAIN_WIDTH, LANES))
    sspec = pl.BlockSpec((1, HEAD_DIM, HEAD_DIM, LANES), lambda h, bi: (h, 0, 0, bi))
    vec = pl.BlockSpec((HEAD_DIM, LANES), lambda h, bi: (h, 0))
    ospec = pl.BlockSpec((n_steps, HEAD_DIM, LANES), lambda h, bi: (0, h, bi))
    out, new_state = pl.pallas_call(
        functools.partial(_wkv_steps_kernel, n_steps=n_steps),
        grid=(RWKV_HEADS, bsz // LANES),
        in_specs=[sspec, pl.BlockSpec((8, n_steps, HEAD_DIM, LANES), lambda h, bi: (0, 0, h, bi)), vec, vec],
        out_specs=[ospec, sspec],
        out_shape=[jax.ShapeDtypeStruct((n_steps, MAIN_WIDTH, bsz), f32),
                   jax.ShapeDtypeStruct(state_t.shape, f32)],
        scratch_shapes=[pltpu.VMEM((n_steps, HEAD_DIM, LANES), f32), pltpu.VMEM((n_steps, HEAD_DIM, LANES), f32)],
        compiler_params=_cp(("parallel", "parallel")),
        name="wkv_steps",
    )(state_t, seq_t, lanes(lnx_w), lanes(lnx_b))
    out = jnp.pad(jnp.transpose(out, (2, 0, 1)), ((0, 0), (0, tp - n_steps), (0, 0)))
    return out, jnp.transpose(new_state, (3, 0, 1, 2))


def _mem_attn_kernel(q_ref, g_ref, kt_ref, vt_ref, o_ref):
    gmat = _group_ones()
    bb, tq, _ = q_ref.shape
    head = _iota((1, MEM_WIDTH), 1) // HEAD_DIM
    for b in range(bb):
        q = q_ref[b]
        qn = q * lax.rsqrt(_head_sum(q * q, gmat) * (1.0 / HEAD_DIM) + NORM_EPS) * g_ref[...]
        qs = jnp.concatenate([jnp.where(head == h, qn, 0.0) for h in range(MEM_HEADS)], axis=0)
        s = _dot(qs.astype(bf16), kt_ref[0, b].astype(bf16)) * ATTN_SCALE
        e = jnp.exp(s - jnp.max(s, axis=-1, keepdims=True))
        o4 = _dot_nt(e.astype(bf16), vt_ref[0, b].astype(bf16)) / jnp.sum(e, axis=-1, keepdims=True)
        o = jnp.zeros((tq, MEM_WIDTH), f32)
        for h in range(MEM_HEADS):
            o = o + jnp.where(head == h, o4[h * tq:(h + 1) * tq], 0.0)
        o_ref[b] = o


def _mem_attn(proj, col_block, g_qnorm, mem_kt, mem_vt, layer, tq, bb):
    bsz, t, _ = proj.shape
    tq = min(tq, t)
    g = jnp.tile(g_qnorm.reshape(1, HEAD_DIM), (1, MEM_HEADS))
    kv_spec = pl.BlockSpec((1, bb, MEM_WIDTH, N_MEM), lambda b, i: (layer, b, 0, 0))
    return pl.pallas_call(
        _mem_attn_kernel,
        grid=(bsz // bb, t // tq),
        in_specs=[pl.BlockSpec((bb, tq, MEM_WIDTH), lambda b, i: (b, i, col_block)),
                  pl.BlockSpec((1, MEM_WIDTH), lambda b, i: (0, 0)),
                  kv_spec, kv_spec],
        out_specs=pl.BlockSpec((bb, tq, MEM_WIDTH), lambda b, i: (b, i, 0)),
        out_shape=jax.ShapeDtypeStruct((bsz, t, MEM_WIDTH), f32),
        compiler_params=_cp(("parallel", "parallel")),
        name="mem_attn",
    )(proj, g, mem_kt, mem_vt)


def _swa_kernel(sink_ref, q_ref, g_ref, cos_ref, sin_ref, kp_ref, kc_ref, vp_ref, vc_ref, o_ref, qn_ref,
                *, first_block_has_no_past):
    i = pl.program_id(1)
    gmat = _group_ones()
    tq = q_ref.shape[1]
    for j in range(N_SLABS):
        sl = slice(j * LANES, (j + 1) * LANES)
        x = q_ref[0, :, sl]
        xn = x * lax.rsqrt(_head_sum(x * x, gmat) * (1.0 / HEAD_DIM) + NORM_EPS) * g_ref[...]
        qn_ref[:, sl] = xn * cos_ref[...] + _rot_half(xn) * sin_ref[...]
    kc, vc = kc_ref[0], vc_ref[0]
    if tq < BLOCK:
        pad = jnp.zeros((BLOCK - tq, KV_WIDTH), f32)
        kc = jnp.concatenate([kc, pad], axis=0)
        vc = jnp.concatenate([vc, pad], axis=0)
    keys = jnp.concatenate([kp_ref[0], kc], axis=0).astype(bf16)
    vals = jnp.concatenate([vp_ref[0], vc], axis=0).astype(bf16)
    row = _iota((tq, 2 * BLOCK), 0)
    col = _iota((tq, 2 * BLOCK), 1)
    mask = (col > row) & (col <= row + WINDOW)
    if first_block_has_no_past:
        mask = mask & ((i > 0) | (col >= BLOCK))
    for hk in range(SWA_KV_HEADS):
        k_h = keys[:, hk * HEAD_DIM:(hk + 1) * HEAD_DIM]
        v_h = vals[:, hk * HEAD_DIM:(hk + 1) * HEAD_DIM]
        for gq in range(SWA_GROUP):
            hq = hk * SWA_GROUP + gq
            sl = slice(hq * HEAD_DIM, (hq + 1) * HEAD_DIM)
            s = _dot_nt(qn_ref[:, sl].astype(bf16), k_h) * ATTN_SCALE
            s = jnp.where(mask, s, NEG_BIG)
            sink = sink_ref[hq]
            m = jnp.maximum(jnp.max(s, axis=-1, keepdims=True), sink)
            e = jnp.exp(s - m)
            denom = jnp.sum(e, axis=-1, keepdims=True) + jnp.exp(sink - m)
            o_ref[0, :, sl] = _dot(e.astype(bf16), v_h) / denom


def _swa(proj, g_qnorm, sinks, cos, sin, k_prev, k_cur, v_prev, v_cur, tq, prev_index, first_block_has_no_past):
    bsz, t, _ = proj.shape
    g = jnp.tile(g_qnorm.reshape(1, HEAD_DIM), (1, HEADS_PER_SLAB))
    nq = t // tq
    return pl.pallas_call(
        functools.partial(_swa_kernel, first_block_has_no_past=first_block_has_no_past),
        grid=(bsz, nq),
        in_specs=[pl.BlockSpec(memory_space=pltpu.SMEM),
                  pl.BlockSpec((1, tq, MAIN_WIDTH), lambda b, i: (b, i, 0)),
                  pl.BlockSpec((1, LANES), lambda b, i: (0, 0)),
                  pl.BlockSpec((tq, LANES), lambda b, i: (i, 0)),
                  pl.BlockSpec((tq, LANES), lambda b, i: (i, 0)),
                  pl.BlockSpec((1, BLOCK, KV_WIDTH), lambda b, i: (b, prev_index(i), 0)),
                  pl.BlockSpec((1, tq, KV_WIDTH), lambda b, i: (b, i, 0)),
                  pl.BlockSpec((1, BLOCK, KV_WIDTH), lambda b, i: (b, prev_index(i), 0)),
                  pl.BlockSpec((1, tq, KV_WIDTH), lambda b, i: (b, i, 0))],
        out_specs=pl.BlockSpec((1, tq, MAIN_WIDTH), lambda b, i: (b, i, 0)),
        out_shape=jax.ShapeDtypeStruct((bsz, t, MAIN_WIDTH), f32),
        scratch_shapes=[pltpu.VMEM((tq, MAIN_WIDTH), f32)],
        compiler_params=_cp(("parallel", "parallel")),
        name="swa_attn",
    )(sinks, proj, g, cos, sin, k_prev, k_cur, v_prev, v_cur)


def _swa_step_kernel(sink_ref, q_ref, g_ref, cos_ref, sin_ref, kt_ref, vt_ref, kn_ref, vn_ref,
                     o_ref, kto_ref, vto_ref, qn_ref, *, n_new):
    gmat = _group_ones()
    bb, tq, _ = q_ref.shape
    row = _iota((SWA_GROUP * tq, 2 * BLOCK), 0) & (tq - 1)
    col = _iota((SWA_GROUP * tq, 2 * BLOCK), 1)
    mask = (col > row) & (col <= row + WINDOW)
    grp = _iota((SWA_GROUP * tq, 1), 0) // tq
    lane = _iota((KV_WIDTH, BLOCK), 1)
    pad = jnp.zeros((BLOCK - tq, KV_WIDTH), f32)
    for b in range(bb):
        for j in range(N_SLABS):
            sl = slice(j * LANES, (j + 1) * LANES)
            x = q_ref[b, :, sl]
            xn = x * lax.rsqrt(_head_sum(x * x, gmat) * (1.0 / HEAD_DIM) + NORM_EPS) * g_ref[...]
            qn_ref[:, sl] = xn * cos_ref[...] + _rot_half(xn) * sin_ref[...]
        kt, vt = kt_ref[b], vt_ref[b]
        kn = jnp.concatenate([kn_ref[b], pad], axis=0)
        vn = jnp.concatenate([vn_ref[b], pad], axis=0)
        keep = lane < BLOCK - n_new
        kto_ref[b] = jnp.where(keep, pltpu.roll(kt, BLOCK - n_new, 1), pltpu.roll(kn.T, BLOCK - n_new, 1))
        vto_ref[b] = jnp.where(keep, pltpu.roll(vt, BLOCK - n_new, 1), pltpu.roll(vn.T, BLOCK - n_new, 1))
        ktb, vtb, knb, vnb = kt.astype(bf16), vt.astype(bf16), kn.astype(bf16), vn.astype(bf16)
        for hk in range(SWA_KV_HEADS):
            hs = slice(hk * HEAD_DIM, (hk + 1) * HEAD_DIM)
            q3 = jnp.concatenate([qn_ref[:, (hk * SWA_GROUP + gq) * HEAD_DIM:(hk * SWA_GROUP + gq + 1) * HEAD_DIM]
                                  for gq in range(SWA_GROUP)], axis=0).astype(bf16)
            s = jnp.concatenate([_dot(q3, ktb[hs, :]), _dot_nt(q3, knb[:, hs])], axis=1) * ATTN_SCALE
            s = jnp.where(mask, s, NEG_BIG)
            sink = jnp.where(grp == 0, sink_ref[hk * SWA_GROUP],
                             jnp.where(grp == 1, sink_ref[hk * SWA_GROUP + 1], sink_ref[hk * SWA_GROUP + 2]))
            m = jnp.maximum(jnp.max(s, axis=-1, keepdims=True), sink)
            e = jnp.exp(s - m)
            denom = jnp.sum(e, axis=-1, keepdims=True) + jnp.exp(sink - m)
            eb = e.astype(bf16)
            o = (_dot_nt(eb[:, :BLOCK], vtb[hs, :]) + _dot(eb[:, BLOCK:], vnb[:, hs])) / denom
            for gq in range(SWA_GROUP):
                hq = hk * SWA_GROUP + gq
                o_ref[b, :, hq * HEAD_DIM:(hq + 1) * HEAD_DIM] = o[gq * tq:(gq + 1) * tq]


def _swa_step(proj, g_qnorm, sinks, cos, sin, cache_kt, cache_vt, k_new, v_new, n_new, bb):
    bsz, tq, _ = proj.shape
    assert SWA_GROUP == 3 and tq & (tq - 1) == 0
    g = jnp.tile(g_qnorm.reshape(1, HEAD_DIM), (1, HEADS_PER_SLAB))
    cache_spec = pl.BlockSpec((bb, KV_WIDTH, BLOCK), lambda b: (b, 0, 0))
    new_spec = pl.BlockSpec((bb, tq, KV_WIDTH), lambda b: (b, 0, 0))
    tab_spec = pl.BlockSpec((tq, LANES), lambda b: (0, 0))
    return pl.pallas_call(
        functools.partial(_swa_step_kernel, n_new=n_new),
        grid=(bsz // bb,),
        in_specs=[pl.BlockSpec(memory_space=pltpu.SMEM),
                  pl.BlockSpec((bb, tq, MAIN_WIDTH), lambda b: (b, 0, 0)),
                  pl.BlockSpec((1, LANES), lambda b: (0, 0)), tab_spec, tab_spec,
                  cache_spec, cache_spec, new_spec, new_spec],
        out_specs=[pl.BlockSpec((bb, tq, MAIN_WIDTH), lambda b: (b, 0, 0)), cache_spec, cache_spec],
        out_shape=[jax.ShapeDtypeStruct((bsz, tq, MAIN_WIDTH), f32),
                   jax.ShapeDtypeStruct(cache_kt.shape, f32), jax.ShapeDtypeStruct(cache_vt.shape, f32)],
        scratch_shapes=[pltpu.VMEM((tq, MAIN_WIDTH), f32)],
        compiler_params=_cp(("parallel",)),
        name="swa_step",
    )(sinks, proj, g, cos, sin, cache_kt, cache_vt, k_new, v_new)


def _out_mlp_kernel(h_ref, main_ref, mem_ref, woa_ref, wob_ref, g_ref, wup_ref, wdn_ref, o_ref, *, tf):
    h1 = (h_ref[...] + _dot(main_ref[...].astype(bf16), woa_ref[...])
          + _dot(mem_ref[...].astype(bf16), wob_ref[...]))
    hn = (h1 * lax.rsqrt(jnp.mean(h1 * h1, axis=-1, keepdims=True) + NORM_EPS) * g_ref[...]).astype(bf16)
    acc = None
    for j in range(D_FF // tf):
        u = _dot(hn, wup_ref[:, j * tf:(j + 1) * tf])
        u = jnp.square(jnp.maximum(u, 0.0)).astype(bf16)
        d = _dot(u, wdn_ref[j * tf:(j + 1) * tf, :])
        acc = d if acc is None else acc + d
    o_ref[...] = h1 + acc


def _out_mlp(h, main, mem_o, w_out_bf16, g_mlp, w_up_bf16, w_down_bf16, tm, tf=1024):
    n = h.shape[0]
    tm = min(tm, n)
    rows = lambda w: pl.BlockSpec((tm, w), lambda i: (i, 0))
    resident = lambda a: pl.BlockSpec(a.shape, lambda i: (0, 0))
    woa, wob = w_out_bf16[:MAIN_WIDTH], w_out_bf16[MAIN_WIDTH:]
    g = g_mlp.reshape(1, D_MODEL)
    return pl.pallas_call(
        functools.partial(_out_mlp_kernel, tf=tf),
        grid=(n // tm,),
        in_specs=[rows(D_MODEL), rows(MAIN_WIDTH), rows(MEM_WIDTH),
                  resident(woa), resident(wob), resident(g), resident(w_up_bf16), resident(w_down_bf16)],
        out_specs=rows(D_MODEL),
        out_shape=jax.ShapeDtypeStruct((n, D_MODEL), f32),
        compiler_params=_cp(("parallel",)),
        name="out_mlp",
    )(h, main, mem_o, woa, wob, g, w_up_bf16, w_down_bf16)


def _rope_tables(pos, reps):
    half = HEAD_DIM // 2
    freqs = jnp.power(ROPE_THETA, -jnp.arange(half, dtype=f32) / half)
    ang = pos.astype(f32)[:, None] * freqs[None, :]
    cos, sin = jnp.cos(ang), jnp.sin(ang)
    cos_h = jnp.concatenate([cos, cos], axis=-1)
    sin_h = jnp.concatenate([-sin, sin], axis=-1)
    return jnp.tile(cos_h, (1, reps)), jnp.tile(sin_h, (1, reps))


def _trunk(x, pos, mem_kt, mem_vt, shift0, wkv0, swa_cache, t_real, W):
    bsz, tp, _ = x.shape
    n = bsz * tp
    prompt = swa_cache is None
    mem_bb = 1 if prompt else 8
    tm = 512
    flat = lambda a: a.reshape(n, a.shape[-1])
    unflat = lambda a: a.reshape(bsz, tp, a.shape[-1])
    cos4, sin4 = _rope_tables(pos, KV_WIDTH // HEAD_DIM)
    cos2, sin2 = cos4[:, :LANES], sin4[:, :LANES]

    proj = unflat(_norm_matmul(flat(x), W["norm_mix"][0], W["w_in_a"], tm))
    new_shift = proj[:, t_real - 1, :RWKV_COLS]
    prep_w = (W["shift_mu"], W["w_w2"], W["w0"], W["w_a2"], W["a0"], W["w_g2"], W["k_k"], W["k_a"], W["r_k"])
    if prompt:
        seqs = _rwkv_prep(proj, shift0, *prep_w, tt=256)
        main, new_wkv = _wkv_chunked(*seqs, W["lnx_w"], W["lnx_b"], tt=512)
    else:
        assert tp > t_real
        packed = proj.at[:, tp - 1, :RWKV_COLS].set(shift0).reshape(1, n, A_IN_COLS)
        seqs = _rwkv_prep(packed, jnp.zeros((1, RWKV_COLS), f32), *prep_w, tt=256, packed_len=tp)
        seqs = [s.reshape(bsz, tp, MAIN_WIDTH) for s in seqs]
        main, new_wkv = _wkv_steps(wkv0, seqs, W["lnx_w"], W["lnx_b"], t_real)
    mem_o = _mem_attn(proj, RWKV_COLS // MEM_WIDTH, W["mem_qnorm"][0], mem_kt, mem_vt, 0, 256, mem_bb)
    h = _out_mlp(flat(x), flat(main), flat(mem_o), W["w_out"][0], W["norm_mlp"][0], W["w_up"][0],
                 W["w_down"][0], tm)

    kv = _norm_matmul(h, W["kv_norm"], W["w_kv"], tm)
    if prompt:
        k_sh, v_sh = _kv_post(kv, W["swa_knorm"], cos4, sin4, tm, True)
    else:
        k_sh, v_sh = _kv_post(kv, W["swa_knorm"], jnp.tile(cos4, (tm // tp, 1)), jnp.tile(sin4, (tm // tp, 1)),
                              tm, True)
    k_sh, v_sh = unflat(k_sh), unflat(v_sh)

    proj = unflat(_norm_matmul(h, W["norm_mix"][1], W["w_in_b"], tm))
    if prompt:
        main = _swa(proj, W["swa_qnorm"], W["sinks"], cos2, sin2, k_sh, k_sh, v_sh, v_sh, BLOCK,
                    lambda i: jnp.maximum(i - 1, 0), True)
        win = min(WINDOW, tp)
        heads = lambda a: a.reshape(bsz, a.shape[1], SWA_KV_HEADS, HEAD_DIM)
        k_state, v_state = heads(k_sh[:, tp - win:]), heads(v_sh[:, tp - win:])
    else:
        ckt, cvt = swa_cache
        main, kt_new, vt_new = _swa_step(proj, W["swa_qnorm"], W["sinks"], cos2, sin2, ckt, cvt, k_sh, v_sh,
                                         t_real, bb=4)
        untransposed = lambda a: jnp.transpose(a.reshape(bsz, SWA_KV_HEADS, HEAD_DIM, -1), (0, 3, 1, 2))
        k_state, v_state = untransposed(kt_new), untransposed(vt_new)
    mem_o = _mem_attn(proj, MAIN_WIDTH // MEM_WIDTH, W["mem_qnorm"][1], mem_kt, mem_vt, 1, 256, mem_bb)
    y = _out_mlp(h, flat(main), flat(mem_o), W["w_out"][1], W["norm_mlp"][1], W["w_up"][1], W["w_down"][1], tm)
    return unflat(y)[:, :t_real], new_shift[None], new_wkv[None], k_state, v_state


def kernel(x_prompt, x_sample, state_rwkv_shift, state_rwkv_wkv, cache_swa_k, cache_swa_v, cache_mem_k,
           cache_mem_v, mem_prompt, norm_mix, norm_mlp, w_out, w_up, w_down, mem_norm, w_mem_kv, mem_qnorm,
           mem_knorm, w_in_a, shift_mu, w_w2, w0, w_a2, a0, w_g2, k_k, k_a, r_k, lnx_w, lnx_b, w_in_b,
           swa_qnorm, sinks, kv_norm, w_kv, swa_knorm):
    W = dict(norm_mix=norm_mix, norm_mlp=norm_mlp, w_out=w_out.astype(bf16), w_up=w_up.astype(bf16),
             w_down=w_down.astype(bf16), mem_qnorm=mem_qnorm, w_in_a=w_in_a[0].astype(bf16),
             shift_mu=shift_mu[0], w_w2=w_w2[0], w0=w0[0], w_a2=w_a2[0], a0=a0[0], w_g2=w_g2[0],
             k_k=k_k[0], k_a=k_a[0], r_k=r_k[0].reshape(-1), lnx_w=lnx_w[0], lnx_b=lnx_b[0],
             w_in_b=w_in_b[0].astype(bf16), swa_qnorm=swa_qnorm[0], sinks=sinks[0], kv_norm=kv_norm,
             w_kv=w_kv.astype(bf16), swa_knorm=swa_knorm)
    bp, tp, _ = x_prompt.shape
    bs, ts, _ = x_sample.shape
    depth = norm_mix.shape[0]

    mem_flat = mem_prompt.reshape(bp * N_MEM, D_MODEL)
    ones = jnp.ones((N_MEM, KV_WIDTH), f32)
    p_mem_k, p_mem_v = [], []
    for l in range(depth):
        kv = _norm_matmul(mem_flat, mem_norm[l], w_mem_kv[l].astype(bf16), 512)
        mk, mv = _kv_post(kv, mem_knorm[l], ones, ones, N_MEM, False)
        p_mem_k.append(mk.reshape(bp, N_MEM, MEM_WIDTH))
        p_mem_v.append(mv.reshape(bp, N_MEM, MEM_WIDTH))
    transposed = lambda a: jnp.swapaxes(jnp.stack(a), 2, 3)
    y_p, p_shift, p_wkv, p_k, p_v = _trunk(
        x_prompt, jnp.arange(tp), transposed(p_mem_k), transposed(p_mem_v), jnp.zeros((bp, RWKV_COLS), f32),
        None, None, tp, W)
    mem_heads = lambda a: jnp.stack(a).reshape(depth, bp, N_MEM, MEM_HEADS, HEAD_DIM)
    cache_t = lambda a: jnp.transpose(a, (0, 1, 3, 4, 2)).reshape(depth, bs, MEM_WIDTH, N_MEM)

    tpad = -(-ts // SUBLANES) * SUBLANES
    x_s = jnp.pad(x_sample, ((0, 0), (0, tpad - ts), (0, 0)))
    win = cache_swa_k.shape[1]
    assert win == WINDOW
    swa_t = lambda a: jnp.transpose(a, (0, 2, 3, 1)).reshape(bs, KV_WIDTH, win)
    y_s, s_shift, s_wkv, s_k, s_v = _trunk(
        x_s, PAST_LEN + jnp.arange(tpad), cache_t(cache_mem_k), cache_t(cache_mem_v),
        state_rwkv_shift[0], state_rwkv_wkv[0],
        (swa_t(cache_swa_k), swa_t(cache_swa_v)), ts, W)

    return (y_p, y_s, p_shift, p_wkv, p_k, p_v, mem_heads(p_mem_k), mem_heads(p_mem_v),
            s_shift, s_wkv, s_k, s_v)
```

```python
import functools

import numpy as np
import jax
import jax.numpy as jnp
from jax import lax
from jax.experimental import pallas as pl
from jax.experimental.pallas import tpu as pltpu

f32 = jnp.float32
bf16 = jnp.bfloat16
HI = lax.Precision.HIGHEST

D_MODEL = 1024
HEAD_DIM = 64
MEM_HEADS = 4
MEM_WIDTH = MEM_HEADS * HEAD_DIM
MAIN_WIDTH = D_MODEL - MEM_WIDTH
RWKV_HEADS = MAIN_WIDTH // HEAD_DIM
DECAY_LORA = 64
AAA_LORA = 64
GATE_LORA = 128
RWKV_COLS = 3 * MAIN_WIDTH + DECAY_LORA + AAA_LORA + GATE_LORA
A_IN_COLS = RWKV_COLS + MEM_WIDTH
SWA_Q_HEADS = MAIN_WIDTH // HEAD_DIM
SWA_KV_HEADS = 4
SWA_GROUP = SWA_Q_HEADS // SWA_KV_HEADS
KV_WIDTH = SWA_KV_HEADS * HEAD_DIM
WINDOW = 128
BLOCK = 128
N_MEM = 256
D_FF = 4 * D_MODEL
PAST_LEN = 16384
ROPE_THETA = 10000.0
NORM_EPS = 1e-6
LNX_EPS = 6.4e-4
L2_EPS = 1e-12
ATTN_SCALE = HEAD_DIM ** -0.5

LANES = 128
SUBLANES = 8
HEADS_PER_SLAB = LANES // HEAD_DIM
N_SLABS = MAIN_WIDTH // LANES
CHUNK = 64
VMEM_LIMIT = 56 * 1024 * 1024
NEG_BIG = -1e30


def _cp(sem, vmem=VMEM_LIMIT):
    return pltpu.CompilerParams(dimension_semantics=sem, vmem_limit_bytes=vmem)


def _dot(a, b, precision=None):
    return jnp.dot(a, b, precision=precision, preferred_element_type=f32)


def _dot_nt(a, b, precision=None):
    return lax.dot_general(a, b, (((1,), (1,)), ((), ())), precision=precision,
                           preferred_element_type=f32)


def _dot_tn(a, b, precision=None):
    return lax.dot_general(a, b, (((0,), (0,)), ((), ())), precision=precision,
                           preferred_element_type=f32)


def _iota(shape, dim):
    return lax.broadcasted_iota(jnp.int32, shape, dim)


def _group_ones():
    r = _iota((LANES, LANES), 0)
    c = _iota((LANES, LANES), 1)
    return jnp.where((r < HEAD_DIM) == (c < HEAD_DIM), 1.0, 0.0).astype(f32)


def _head_sum(x, gmat):
    w = x.shape[-1]
    gb = gmat.astype(bf16)
    hi = x.astype(bf16)
    lo = (x - hi.astype(f32)).astype(bf16)
    parts = [_dot(hi[:, j * LANES:(j + 1) * LANES], gb) + _dot(lo[:, j * LANES:(j + 1) * LANES], gb)
             for j in range(w // LANES)]
    return parts[0] if len(parts) == 1 else jnp.concatenate(parts, axis=-1)


def _sigmoid(x):
    return 1.0 / (1.0 + jnp.exp(-x))


def _softplus(x):
    return jnp.maximum(x, 0.0) + jnp.log1p(jnp.exp(-jnp.abs(x)))


def _rot_half(x):
    lane = _iota(x.shape, 1)
    first = (lane & (HEAD_DIM - 1)) < (HEAD_DIM // 2)
    return jnp.where(first, pltpu.roll(x, LANES - HEAD_DIM // 2, 1), pltpu.roll(x, HEAD_DIM // 2, 1))


def _norm_matmul_kernel(x_ref, g_ref, w_ref, o_ref):
    x = x_ref[...]
    xn = x * lax.rsqrt(jnp.mean(x * x, axis=-1, keepdims=True) + NORM_EPS) * g_ref[...]
    o_ref[...] = _dot(xn.astype(bf16), w_ref[...])


def _norm_matmul(x, g, w_bf16, tm):
    n, d = x.shape
    m = w_bf16.shape[1]
    tm = min(tm, n)
    return pl.pallas_call(
        _norm_matmul_kernel,
        grid=(n // tm,),
        in_specs=[pl.BlockSpec((tm, d), lambda i: (i, 0)),
                  pl.BlockSpec((1, d), lambda i: (0, 0)),
                  pl.BlockSpec((d, m), lambda i: (0, 0))],
        out_specs=pl.BlockSpec((tm, m), lambda i: (i, 0)),
        out_shape=jax.ShapeDtypeStruct((n, m), f32),
        compiler_params=_cp(("parallel",)),
        name="norm_matmul",
    )(x, g.reshape(1, d), w_bf16)


def _kv_post_kernel(kv_ref, g_ref, cos_ref, sin_ref, k_ref, v_ref, *, use_rope):
    gmat = _group_ones()
    x = kv_ref[:, :KV_WIDTH]
    ms = _head_sum(x * x, gmat) * (1.0 / HEAD_DIM)
    xn = x * lax.rsqrt(ms + NORM_EPS) * g_ref[...]
    if use_rope:
        for j in range(KV_WIDTH // LANES):
            sl = slice(j * LANES, (j + 1) * LANES)
            xs = xn[:, sl]
            k_ref[:, sl] = xs * cos_ref[:, sl] + _rot_half(xs) * sin_ref[:, sl]
    else:
        k_ref[...] = xn
    v_ref[...] = kv_ref[:, KV_WIDTH:]


def _kv_post(kv, g_head, cos, sin, tm, use_rope):
    n = kv.shape[0]
    tm = min(tm, n)
    nt = cos.shape[0] // tm
    g = jnp.tile(g_head.reshape(1, HEAD_DIM), (1, KV_WIDTH // HEAD_DIM))
    return pl.pallas_call(
        functools.partial(_kv_post_kernel, use_rope=use_rope),
        grid=(n // tm,),
        in_specs=[pl.BlockSpec((tm, 2 * KV_WIDTH), lambda i: (i, 0)),
                  pl.BlockSpec((1, KV_WIDTH), lambda i: (0, 0)),
                  pl.BlockSpec((tm, KV_WIDTH), lambda i: (i % nt, 0)),
                  pl.BlockSpec((tm, KV_WIDTH), lambda i: (i % nt, 0))],
        out_specs=[pl.BlockSpec((tm, KV_WIDTH), lambda i: (i, 0)),
                   pl.BlockSpec((tm, KV_WIDTH), lambda i: (i, 0))],
        out_shape=[jax.ShapeDtypeStruct((n, KV_WIDTH), f32)] * 2,
        compiler_params=_cp(("parallel",)),
        name="kv_post",
    )(kv, g, cos, sin)


def _rwkv_prep_kernel(p_ref, pprev_ref, shift_ref, mu_ref, ww2_ref, w0_ref, wa2_ref, a0_ref, wg2_ref,
                      kk_ref, ka_ref, rk_ref,
                      r_o, lw_o, k_o, v_o, a_o, b_o, bonus_o, g_o, *, step_major):
    i = pl.program_id(1)
    gmat = _group_ones()
    pf = p_ref[0]
    if not step_major:
        first_prev = jnp.where(i == 0, shift_ref[0], pprev_ref[0][SUBLANES - 1:SUBLANES, :])
        prev = jnp.where(_iota(pf.shape, 0) == 0, first_prev, pltpu.roll(pf, 1, 0))
    else:
        prev = jnp.concatenate([shift_ref[0], pf[:pf.shape[0] - shift_ref.shape[1]]], axis=0)
    ps = pf + (prev - pf) * mu_ref[...]
    i1, i2, i3 = MAIN_WIDTH, 2 * MAIN_WIDTH, 3 * MAIN_WIDTH
    i4, i5 = i3 + DECAY_LORA, i3 + DECAY_LORA + AAA_LORA
    r, k, v = ps[:, :i1], ps[:, i1:i2], ps[:, i2:i3]
    wd, ad, gd = ps[:, i3:i4], ps[:, i4:i5], ps[:, i5:]
    w_log = -_softplus(-(w0_ref[...] + _dot(jnp.tanh(wd), ww2_ref[...], HI))) - 0.5
    lw = -jnp.exp(w_log)
    a_sig = _sigmoid(a0_ref[...] + _dot(ad, wa2_ref[...], HI))
    g = _dot(_sigmoid(gd), wg2_ref[...], HI)
    kk = k * kk_ref[...]
    kk = kk / jnp.maximum(jnp.sqrt(_head_sum(kk * kk, gmat)), L2_EPS)
    k2 = k * (1.0 + (a_sig - 1.0) * ka_ref[...])
    r_o[0] = r.astype(r_o.dtype)
    lw_o[0] = lw
    k_o[0] = k2.astype(k_o.dtype)
    v_o[0] = v.astype(v_o.dtype)
    a_o[0] = (-kk).astype(a_o.dtype)
    b_o[0] = (kk * a_sig).astype(b_o.dtype)
    bonus_o[0] = (_head_sum(r * k2 * rk_ref[...], gmat) * v).astype(bonus_o.dtype)
    g_o[0] = g.astype(g_o.dtype)


def _rwkv_prep(proj, shift_prev, mu, w_w2, w0, w_a2, a0, w_g2, k_k, k_a, r_k, tt, step_major=False,
               seq_dtype=f32):
    bsz, t, _ = proj.shape
    tt = t if step_major else min(tt, t)
    shift_rows = shift_prev.reshape(bsz, -1, RWKV_COLS)
    row = lambda x: x.reshape(1, -1)
    full = lambda a: pl.BlockSpec(a.shape, lambda b, i: (0,) * a.ndim)
    consts = [row(mu), w_w2, row(w0), w_a2, row(a0), w_g2, row(k_k), row(k_a), row(r_k)]
    out_spec = pl.BlockSpec((1, tt, MAIN_WIDTH), lambda b, i: (b, i, 0))
    return pl.pallas_call(
        functools.partial(_rwkv_prep_kernel, step_major=step_major),
        grid=(bsz, t // tt),
        in_specs=[pl.BlockSpec((1, tt, RWKV_COLS), lambda b, i: (b, i, 0)),
                  pl.BlockSpec((1, SUBLANES, RWKV_COLS),
                               lambda b, i: (b, jnp.maximum(i * (tt // SUBLANES) - 1, 0), 0)),
                  pl.BlockSpec((1,) + shift_rows.shape[1:], lambda b, i: (b, 0, 0))]
                 + [full(c) for c in consts],
        out_specs=[out_spec] * 8,
        out_shape=[jax.ShapeDtypeStruct((bsz, t, MAIN_WIDTH), f32 if j == 1 else seq_dtype) for j in range(8)],
        compiler_params=_cp(("parallel", "arbitrary")),
        name="rwkv_prep",
    )(proj, proj, shift_rows, *consts)


def _group_norm_gate(y, bonus, g, lnw, lnb, gmat):
    mean = _head_sum(y, gmat) * (1.0 / HEAD_DIM)
    yc = y - mean
    var = _head_sum(yc * yc, gmat) * (1.0 / HEAD_DIM)
    return (yc * lax.rsqrt(var + LNX_EPS) * lnw + lnb + bonus) * g


def _wkv_chunk_kernel(r_ref, lw_ref, k_ref, v_ref, a_ref, b_ref, bonus_ref, g_ref, lnw_ref, lnb_ref,
                      o_ref, s_ref, h_ref, *, chunks_per_iter):
    i = pl.program_id(2)
    c = CHUNK
    tt = r_ref.shape[1]

    @pl.when(i == 0)
    def _():
        h_ref[...] = jnp.zeros_like(h_ref)

    gmat = _group_ones()
    bd = gmat > 0.5
    ltri = jnp.where(_iota((c, c), 1) <= _iota((c, c), 0), 1.0, 0.0).astype(f32)
    row = _iota((LANES, LANES), 0)
    col = _iota((LANES, LANES), 1)
    t_row = row & (c - 1)
    s_col = col & (c - 1)
    m_strict = s_col < t_row
    m_incl = s_col <= t_row
    top = row < c
    lane_lo = _iota((1, LANES), 1) < HEAD_DIM
    eye = row == col
    lnw = lnw_ref[...]
    lnb = lnb_ref[...]
    zeros = jnp.zeros((c, LANES), f32)
    mm = lambda x: x.astype(bf16)

    def body(ci, carry):
        sls = [pl.ds(pl.multiple_of((ci * chunks_per_iter + j) * c, c), c) for j in range(chunks_per_iter)]
        n = range(chunks_per_iter)
        vs = [v_ref[0, sl, :].astype(f32) for sl in sls]
        cums = [_dot(ltri, lw_ref[0, sl, :], HI) for sl in sls]
        tots = [cum[c - 1:c, :] for cum in cums]
        ats = [a_ref[0, sl, :].astype(f32) * jnp.exp(cum - lw_ref[0, sl, :]) for sl, cum in zip(sls, cums)]
        rts = [r_ref[0, sl, :].astype(f32) * jnp.exp(cum) for sl, cum in zip(sls, cums)]
        bks, bkts = [], []
        for sl, cum, tot in zip(sls, cums, tots):
            b, k = b_ref[0, sl, :].astype(f32), k_ref[0, sl, :].astype(f32)
            e_neg, e_rem = jnp.exp(-cum), jnp.exp(tot - cum)
            bks.append(mm(jnp.concatenate([b * e_neg, k * e_neg], axis=0)))
            bkts.append(mm(jnp.concatenate([b * e_rem, k * e_rem], axis=0)))
        lhss = [mm(jnp.concatenate([jnp.where(lane_lo, at, 0.0), jnp.where(lane_lo, 0.0, at),
                                    jnp.where(lane_lo, rt, 0.0), jnp.where(lane_lo, 0.0, rt)], axis=0))
                for at, rt in zip(ats, rts)]
        gms = [_dot_nt(lhss[j], bks[j]) for j in n]
        aaks = [jnp.where(m_strict, gm[:LANES], 0.0) for gm in gms]
        arks = [mm(jnp.where(m_incl, gm[LANES:], 0.0)) for gm in gms]
        aak_sws = [pltpu.roll(aak, HEAD_DIM, 1) for aak in aaks]
        ms = [jnp.where(bd, jnp.where(top, aaks[j], aak_sws[j]), 0.0) for j in n]
        aks = [mm(jnp.where(bd, jnp.where(top, aak_sws[j], aaks[j]), 0.0)) for j in n]
        vvs = [mm(jnp.concatenate([pltpu.roll(v, HEAD_DIM, 1)] * 2, axis=0)) for v in vs]
        akvs = [_dot(aks[j], vvs[j]) for j in n]
        xs = [jnp.where(bd, jnp.concatenate([ats[j], ats[j]], axis=0), akvs[j]) for j in n]
        for lvl in range(6):
            if lvl < 5:
                zs = [_dot(mm(ms[j]), mm(jnp.concatenate([xs[j], ms[j]], axis=1))) for j in n]
                xs = [xs[j] + zs[j][:, :LANES] for j in n]
                ms = [zs[j][:, LANES:] for j in n]
            else:
                zs = [_dot(mm(ms[j]), mm(xs[j])) for j in n]
                xs = [xs[j] + zs[j] for j in n]
        rhss = []
        for j in n:
            x = xs[j]
            ta = jnp.where(lane_lo, x[:c], x[c:])
            uv = pltpu.roll(jnp.where(lane_lo, x[c:], x[:c]), HEAD_DIM, 1)
            rhss.append(mm(jnp.concatenate([jnp.concatenate([ta, uv], axis=1),
                                            jnp.concatenate([zeros, vs[j]], axis=1)], axis=0)))
        z2s = [_dot(arks[j], rhss[j]) for j in n]
        pzs = [_dot_tn(bkts[j], rhss[j]) for j in n]
        qts = [mm(rts[j] + jnp.where(lane_lo, z2s[j][:c, :LANES], z2s[j][c:, :LANES])) for j in n]
        yvs = [jnp.where(lane_lo, z2s[j][:c, LANES:], z2s[j][c:, LANES:]) for j in n]
        pps = [mm(jnp.where(bd, pzs[j][:, :LANES], 0.0)) for j in n]
        hvs = [jnp.where(bd, pzs[j][:, LANES:], 0.0) for j in n]
        tot_cols = [jnp.sum(jnp.where(eye, jnp.exp(tots[j]), 0.0), axis=1, keepdims=True) for j in n]
        h = h_ref[...]
        ys = []
        for j in n:
            hb = mm(h)
            ys.append(_dot(qts[j], hb) + yvs[j])
            h = h * tot_cols[j] + _dot(pps[j], hb) + hvs[j]
        h_ref[...] = h
        means = [_head_sum(y, gmat) * (1.0 / HEAD_DIM) for y in ys]
        ycs = [ys[j] - means[j] for j in n]
        vrs = [_head_sum(yc * yc, gmat) * (1.0 / HEAD_DIM) for yc in ycs]
        for j in n:
            yn = ycs[j] * lax.rsqrt(vrs[j] + LNX_EPS) * lnw + lnb
            o_ref[0, sls[j], :] = ((yn + bonus_ref[0, sls[j], :].astype(f32))
                                   * g_ref[0, sls[j], :].astype(f32)).astype(o_ref.dtype)
        return carry

    lax.fori_loop(0, tt // (c * chunks_per_iter), body, 0)

    @pl.when(i == pl.num_programs(2) - 1)
    def _():
        s_ref[0, 0] = h_ref[...].T


def _wkv_chunked(r, lw, k, v, a, b, bonus, g, lnx_w, lnx_b, tt):
    bsz, t, _ = r.shape
    tt = min(tt, t)
    spec = pl.BlockSpec((1, tt, LANES), lambda bb, hp, i: (bb, i, hp))
    vec = pl.BlockSpec((1, LANES), lambda bb, hp, i: (0, hp))
    out, st = pl.pallas_call(
        functools.partial(_wkv_chunk_kernel, chunks_per_iter=min(8, tt // CHUNK)),
        grid=(bsz, N_SLABS, t // tt),
        in_specs=[spec] * 8 + [vec, vec],
        out_specs=[spec, pl.BlockSpec((1, 1, LANES, LANES), lambda bb, hp, i: (bb, hp, 0, 0))],
        out_shape=[jax.ShapeDtypeStruct((bsz, t, MAIN_WIDTH), bf16),
                   jax.ShapeDtypeStruct((bsz, N_SLABS, LANES, LANES), f32)],
        scratch_shapes=[pltpu.VMEM((LANES, LANES), f32)],
        compiler_params=_cp(("parallel", "parallel", "arbitrary")),
        name="wkv_chunked",
    )(r, lw, k, v, a, b, bonus, g, lnx_w.reshape(1, -1), lnx_b.reshape(1, -1))
    st = jnp.stack([st[:, :, :HEAD_DIM, :HEAD_DIM], st[:, :, HEAD_DIM:, HEAD_DIM:]], axis=2)
    return out, st.reshape(bsz, RWKV_HEADS, HEAD_DIM, HEAD_DIM)


def _wkv_steps_kernel(s_ref, seq_ref, lnw_ref, lnb_ref, o_ref, so_ref, w_ref, y_ref, *, n_steps):
    i_r, i_lw, i_k, i_v, i_a, i_b, i_bonus, i_g = range(8)
    w_ref[...] = jnp.exp(seq_ref[i_lw])
    sub = _iota((SUBLANES, LANES), 0)

    def group(vg, carry):
        v0 = pl.multiple_of(vg * SUBLANES, SUBLANES)
        v_rows = [seq_ref[i_v, t, pl.ds(v0, SUBLANES), :] for t in range(n_steps)]
        ys = [jnp.zeros((SUBLANES, LANES), f32) for _ in range(n_steps)]
        for j in range(SUBLANES):
            sv = s_ref[0, v0 + j]
            for t in range(n_steps):
                sa = jnp.sum(sv * seq_ref[i_a, t], axis=0, keepdims=True)
                sv = sv * w_ref[t] + sa * seq_ref[i_b, t] + v_rows[t][j:j + 1, :] * seq_ref[i_k, t]
                y = jnp.sum(sv * seq_ref[i_r, t], axis=0, keepdims=True)
                ys[t] = jnp.where(sub == j, y, ys[t])
            so_ref[0, v0 + j] = sv
        for t in range(n_steps):
            y_ref[t, pl.ds(v0, SUBLANES), :] = ys[t]
        return carry

    lax.fori_loop(0, HEAD_DIM // SUBLANES, group, 0)
    y = y_ref[...]
    mean = jnp.mean(y, axis=1, keepdims=True)
    yc = y - mean
    var = jnp.mean(yc * yc, axis=1, keepdims=True)
    yn = yc * lax.rsqrt(var + LNX_EPS) * lnw_ref[...] + lnb_ref[...]
    o_ref[...] = (yn + seq_ref[i_bonus]) * seq_ref[i_g]


def _wkv_steps(state, seqs, lnx_w, lnx_b, tp):
    n_steps, bsz, _ = seqs[0].shape
    assert bsz % LANES == 0
    state_t = jnp.transpose(state, (1, 2, 3, 0))
    seq_t = jnp.transpose(jnp.stack(seqs), (0, 1, 3, 2))
    lanes = lambda x: jnp.broadcast_to(x.reshape(MAIN_WIDTH, 1), (MAIN_WIDTH, LANES))
    sspec = pl.BlockSpec((1, HEAD_DIM, HEAD_DIM, LANES), lambda h, bi: (h, 0, 0, bi))
    vec = pl.BlockSpec((HEAD_DIM, LANES), lambda h, bi: (h, 0))
    ospec = pl.BlockSpec((n_steps, HEAD_DIM, LANES), lambda h, bi: (0, h, bi))
    out, new_state = pl.pallas_call(
        functools.partial(_wkv_steps_kernel, n_steps=n_steps),
        grid=(RWKV_HEADS, bsz // LANES),
        in_specs=[sspec, pl.BlockSpec((8, n_steps, HEAD_DIM, LANES), lambda h, bi: (0, 0, h, bi)), vec, vec],
        out_specs=[ospec, sspec],
        out_shape=[jax.ShapeDtypeStruct((n_steps, MAIN_WIDTH, bsz), f32),
                   jax.ShapeDtypeStruct(state_t.shape, f32)],
        scratch_shapes=[pltpu.VMEM((n_steps, HEAD_DIM, LANES), f32), pltpu.VMEM((n_steps, HEAD_DIM, LANES), f32)],
        compiler_params=_cp(("parallel", "parallel")),
        name="wkv_steps",
    )(state_t, seq_t, lanes(lnx_w), lanes(lnx_b))
    out = jnp.pad(jnp.transpose(out, (2, 0, 1)), ((0, 0), (0, tp - n_steps), (0, 0)))
    return out, jnp.transpose(new_state, (3, 0, 1, 2))


def _mem_attn_kernel(q_ref, g_ref, kt_ref, vt_ref, o_ref):
    gmat = _group_ones()
    bb, tq, _ = q_ref.shape
    head = _iota((1, MEM_WIDTH), 1) // HEAD_DIM
    for b in range(bb):
        q = q_ref[b]
        qn = q * lax.rsqrt(_head_sum(q * q, gmat) * (1.0 / HEAD_DIM) + NORM_EPS) * g_ref[...]
        qs = jnp.concatenate([jnp.where(head == h, qn, 0.0) for h in range(MEM_HEADS)], axis=0)
        s = _dot(qs.astype(bf16), kt_ref[0, b].astype(bf16)) * ATTN_SCALE
        e = jnp.exp(s - jnp.max(s, axis=-1, keepdims=True))
        o4 = _dot_nt(e.astype(bf16), vt_ref[0, b].astype(bf16)) / jnp.sum(e, axis=-1, keepdims=True)
        o = jnp.zeros((tq, MEM_WIDTH), f32)
        for h in range(MEM_HEADS):
            o = o + jnp.where(head == h, o4[h * tq:(h + 1) * tq], 0.0)
        o_ref[b] = o.astype(o_ref.dtype)


def _mem_attn(proj, col_block, g_qnorm, mem_kt, mem_vt, layer, tq, bb, out_dtype):
    bsz, t, _ = proj.shape
    tq = min(tq, t)
    g = jnp.tile(g_qnorm.reshape(1, HEAD_DIM), (1, MEM_HEADS))
    kv_spec = pl.BlockSpec((1, bb, MEM_WIDTH, N_MEM), lambda b, i: (layer, b, 0, 0))
    return pl.pallas_call(
        _mem_attn_kernel,
        grid=(bsz // bb, t // tq),
        in_specs=[pl.BlockSpec((bb, tq, MEM_WIDTH), lambda b, i: (b, i, col_block)),
                  pl.BlockSpec((1, MEM_WIDTH), lambda b, i: (0, 0)),
                  kv_spec, kv_spec],
        out_specs=pl.BlockSpec((bb, tq, MEM_WIDTH), lambda b, i: (b, i, 0)),
        out_shape=jax.ShapeDtypeStruct((bsz, t, MEM_WIDTH), out_dtype),
        compiler_params=_cp(("parallel", "parallel")),
        name="mem_attn",
    )(proj, g, mem_kt, mem_vt)


def _swa_kernel(sink_ref, q_ref, g_ref, cos_ref, sin_ref, kp_ref, kc_ref, vp_ref, vc_ref, o_ref, qn_ref,
                *, first_block_has_no_past):
    i = pl.program_id(1)
    gmat = _group_ones()
    tq = q_ref.shape[1]
    for j in range(N_SLABS):
        sl = slice(j * LANES, (j + 1) * LANES)
        x = q_ref[0, :, sl]
        xn = x * lax.rsqrt(_head_sum(x * x, gmat) * (1.0 / HEAD_DIM) + NORM_EPS) * g_ref[...]
        qn_ref[:, sl] = xn * cos_ref[...] + _rot_half(xn) * sin_ref[...]
    kc, vc = kc_ref[0], vc_ref[0]
    if tq < BLOCK:
        pad = jnp.zeros((BLOCK - tq, KV_WIDTH), f32)
        kc = jnp.concatenate([kc, pad], axis=0)
        vc = jnp.concatenate([vc, pad], axis=0)
    keys = jnp.concatenate([kp_ref[0], kc], axis=0).astype(bf16)
    vals = jnp.concatenate([vp_ref[0], vc], axis=0).astype(bf16)
    row = _iota((tq, 2 * BLOCK), 0)
    col = _iota((tq, 2 * BLOCK), 1)
    mask = (col > row) & (col <= row + WINDOW)
    if first_block_has_no_past:
        mask = mask & ((i > 0) | (col >= BLOCK))
    for hk in range(SWA_KV_HEADS):
        k_h = keys[:, hk * HEAD_DIM:(hk + 1) * HEAD_DIM]
        v_h = vals[:, hk * HEAD_DIM:(hk + 1) * HEAD_DIM]
        for gq in range(SWA_GROUP):
            hq = hk * SWA_GROUP + gq
            sl = slice(hq * HEAD_DIM, (hq + 1) * HEAD_DIM)
            s = _dot_nt(qn_ref[:, sl].astype(bf16), k_h) * ATTN_SCALE
            s = jnp.where(mask, s, NEG_BIG)
            sink = sink_ref[hq]
            m = jnp.maximum(jnp.max(s, axis=-1, keepdims=True), sink)
            e = jnp.exp(s - m)
            denom = jnp.sum(e, axis=-1, keepdims=True) + jnp.exp(sink - m)
            o_ref[0, :, sl] = (_dot(e.astype(bf16), v_h) / denom).astype(o_ref.dtype)


def _swa(proj, g_qnorm, sinks, cos, sin, k_prev, k_cur, v_prev, v_cur, tq, prev_index, first_block_has_no_past):
    bsz, t, _ = proj.shape
    g = jnp.tile(g_qnorm.reshape(1, HEAD_DIM), (1, HEADS_PER_SLAB))
    nq = t // tq
    return pl.pallas_call(
        functools.partial(_swa_kernel, first_block_has_no_past=first_block_has_no_past),
        grid=(bsz, nq),
        in_specs=[pl.BlockSpec(memory_space=pltpu.SMEM),
                  pl.BlockSpec((1, tq, MAIN_WIDTH), lambda b, i: (b, i, 0)),
                  pl.BlockSpec((1, LANES), lambda b, i: (0, 0)),
                  pl.BlockSpec((tq, LANES), lambda b, i: (i, 0)),
                  pl.BlockSpec((tq, LANES), lambda b, i: (i, 0)),
                  pl.BlockSpec((1, BLOCK, KV_WIDTH), lambda b, i: (b, prev_index(i), 0)),
                  pl.BlockSpec((1, tq, KV_WIDTH), lambda b, i: (b, i, 0)),
                  pl.BlockSpec((1, BLOCK, KV_WIDTH), lambda b, i: (b, prev_index(i), 0)),
                  pl.BlockSpec((1, tq, KV_WIDTH), lambda b, i: (b, i, 0))],
        out_specs=pl.BlockSpec((1, tq, MAIN_WIDTH), lambda b, i: (b, i, 0)),
        out_shape=jax.ShapeDtypeStruct((bsz, t, MAIN_WIDTH), bf16),
        scratch_shapes=[pltpu.VMEM((tq, MAIN_WIDTH), f32)],
        compiler_params=_cp(("parallel", "parallel")),
        name="swa_attn",
    )(sinks, proj, g, cos, sin, k_prev, k_cur, v_prev, v_cur)


def _swa_step_kernel(sink_ref, q_ref, g_ref, cos_ref, sin_ref, kt_ref, vt_ref, kn_ref, vn_ref,
                     o_ref, kto_ref, vto_ref, qn_ref, *, n_new):
    gmat = _group_ones()
    bb, tq, _ = q_ref.shape
    row = _iota((SWA_GROUP * tq, 2 * BLOCK), 0) & (tq - 1)
    col = _iota((SWA_GROUP * tq, 2 * BLOCK), 1)
    mask = (col > row) & (col <= row + WINDOW)
    grp = _iota((SWA_GROUP * tq, 1), 0) // tq
    lane = _iota((KV_WIDTH, BLOCK), 1)
    pad = jnp.zeros((BLOCK - tq, KV_WIDTH), f32)
    for b in range(bb):
        for j in range(N_SLABS):
            sl = slice(j * LANES, (j + 1) * LANES)
            x = q_ref[b, :, sl]
            xn = x * lax.rsqrt(_head_sum(x * x, gmat) * (1.0 / HEAD_DIM) + NORM_EPS) * g_ref[...]
            qn_ref[:, sl] = xn * cos_ref[...] + _rot_half(xn) * sin_ref[...]
        kt, vt = kt_ref[b], vt_ref[b]
        kn = jnp.concatenate([kn_ref[b], pad], axis=0)
        vn = jnp.concatenate([vn_ref[b], pad], axis=0)
        keep = lane < BLOCK - n_new
        kto_ref[b] = jnp.where(keep, pltpu.roll(kt, BLOCK - n_new, 1), pltpu.roll(kn.T, BLOCK - n_new, 1))
        vto_ref[b] = jnp.where(keep, pltpu.roll(vt, BLOCK - n_new, 1), pltpu.roll(vn.T, BLOCK - n_new, 1))
        ktb, vtb, knb, vnb = kt.astype(bf16), vt.astype(bf16), kn.astype(bf16), vn.astype(bf16)
        for hk in range(SWA_KV_HEADS):
            hs = slice(hk * HEAD_DIM, (hk + 1) * HEAD_DIM)
            q3 = jnp.concatenate([qn_ref[:, (hk * SWA_GROUP + gq) * HEAD_DIM:(hk * SWA_GROUP + gq + 1) * HEAD_DIM]
                                  for gq in range(SWA_GROUP)], axis=0).astype(bf16)
            s = jnp.concatenate([_dot(q3, ktb[hs, :]), _dot_nt(q3, knb[:, hs])], axis=1) * ATTN_SCALE
            s = jnp.where(mask, s, NEG_BIG)
            sink = jnp.where(grp == 0, sink_ref[hk * SWA_GROUP],
                             jnp.where(grp == 1, sink_ref[hk * SWA_GROUP + 1], sink_ref[hk * SWA_GROUP + 2]))
            m = jnp.maximum(jnp.max(s, axis=-1, keepdims=True), sink)
            e = jnp.exp(s - m)
            denom = jnp.sum(e, axis=-1, keepdims=True) + jnp.exp(sink - m)
            eb = e.astype(bf16)
            o = (_dot_nt(eb[:, :BLOCK], vtb[hs, :]) + _dot(eb[:, BLOCK:], vnb[:, hs])) / denom
            for gq in range(SWA_GROUP):
                hq = hk * SWA_GROUP + gq
                o_ref[b, :, hq * HEAD_DIM:(hq + 1) * HEAD_DIM] = o[gq * tq:(gq + 1) * tq]


def _swa_step(proj, g_qnorm, sinks, cos, sin, cache_kt, cache_vt, k_new, v_new, n_new, bb):
    bsz, tq, _ = proj.shape
    assert SWA_GROUP == 3 and tq & (tq - 1) == 0
    g = jnp.tile(g_qnorm.reshape(1, HEAD_DIM), (1, HEADS_PER_SLAB))
    cache_spec = pl.BlockSpec((bb, KV_WIDTH, BLOCK), lambda b: (b, 0, 0))
    new_spec = pl.BlockSpec((bb, tq, KV_WIDTH), lambda b: (b, 0, 0))
    tab_spec = pl.BlockSpec((tq, LANES), lambda b: (0, 0))
    return pl.pallas_call(
        functools.partial(_swa_step_kernel, n_new=n_new),
        grid=(bsz // bb,),
        in_specs=[pl.BlockSpec(memory_space=pltpu.SMEM),
                  pl.BlockSpec((bb, tq, MAIN_WIDTH), lambda b: (b, 0, 0)),
                  pl.BlockSpec((1, LANES), lambda b: (0, 0)), tab_spec, tab_spec,
                  cache_spec, cache_spec, new_spec, new_spec],
        out_specs=[pl.BlockSpec((bb, tq, MAIN_WIDTH), lambda b: (b, 0, 0)), cache_spec, cache_spec],
        out_shape=[jax.ShapeDtypeStruct((bsz, tq, MAIN_WIDTH), f32),
                   jax.ShapeDtypeStruct(cache_kt.shape, f32), jax.ShapeDtypeStruct(cache_vt.shape, f32)],
        scratch_shapes=[pltpu.VMEM((tq, MAIN_WIDTH), f32)],
        compiler_params=_cp(("parallel",)),
        name="swa_step",
    )(sinks, proj, g, cos, sin, cache_kt, cache_vt, k_new, v_new)


def _out_mlp_kernel(h_ref, main_ref, mem_ref, woa_ref, wob_ref, g_ref, wup_ref, wdn_ref, o_ref, *, tf):
    h1 = (h_ref[...] + _dot(main_ref[...].astype(bf16), woa_ref[...])
          + _dot(mem_ref[...].astype(bf16), wob_ref[...]))
    hn = (h1 * lax.rsqrt(jnp.mean(h1 * h1, axis=-1, keepdims=True) + NORM_EPS) * g_ref[...]).astype(bf16)
    acc = None
    for j in range(D_FF // tf):
        u = _dot(hn, wup_ref[:, j * tf:(j + 1) * tf])
        u = jnp.square(jnp.maximum(u, 0.0)).astype(bf16)
        d = _dot(u, wdn_ref[j * tf:(j + 1) * tf, :])
        acc = d if acc is None else acc + d
    o_ref[...] = h1 + acc


def _out_mlp(h, main, mem_o, w_out_bf16, g_mlp, w_up_bf16, w_down_bf16, tm, tf=1024):
    n = h.shape[0]
    tm = min(tm, n)
    rows = lambda w: pl.BlockSpec((tm, w), lambda i: (i, 0))
    resident = lambda a: pl.BlockSpec(a.shape, lambda i: (0, 0))
    woa, wob = w_out_bf16[:MAIN_WIDTH], w_out_bf16[MAIN_WIDTH:]
    g = g_mlp.reshape(1, D_MODEL)
    return pl.pallas_call(
        functools.partial(_out_mlp_kernel, tf=tf),
        grid=(n // tm,),
        in_specs=[rows(D_MODEL), rows(MAIN_WIDTH), rows(MEM_WIDTH),
                  resident(woa), resident(wob), resident(g), resident(w_up_bf16), resident(w_down_bf16)],
        out_specs=rows(D_MODEL),
        out_shape=jax.ShapeDtypeStruct((n, D_MODEL), f32),
        compiler_params=_cp(("parallel",)),
        name="out_mlp",
    )(h, main, mem_o, woa, wob, g, w_up_bf16, w_down_bf16)


def _rope_tables(pos, reps):
    half = HEAD_DIM // 2
    freqs = jnp.power(ROPE_THETA, -jnp.arange(half, dtype=f32) / half)
    ang = pos.astype(f32)[:, None] * freqs[None, :]
    cos, sin = jnp.cos(ang), jnp.sin(ang)
    cos_h = jnp.concatenate([cos, cos], axis=-1)
    sin_h = jnp.concatenate([-sin, sin], axis=-1)
    return jnp.tile(cos_h, (1, reps)), jnp.tile(sin_h, (1, reps))


def _trunk(x, pos, mem_kt, mem_vt, shift0, wkv0, swa_cache, t_real, W):
    bsz, tp, _ = x.shape
    n = bsz * tp
    prompt = swa_cache is None
    mem_bb = 1 if prompt else 8
    act_dtype = bf16 if prompt else f32
    tm = 512
    flat = lambda a: a.reshape(n, a.shape[-1])
    unflat = lambda a: a.reshape(bsz, tp, a.shape[-1])
    cos4, sin4 = _rope_tables(pos, KV_WIDTH // HEAD_DIM)
    cos2, sin2 = cos4[:, :LANES], sin4[:, :LANES]

    proj = unflat(_norm_matmul(flat(x), W["norm_mix"][0], W["w_in_a"], tm))
    new_shift = proj[:, t_real - 1, :RWKV_COLS]
    prep_w = (W["shift_mu"], W["w_w2"], W["w0"], W["w_a2"], W["a0"], W["w_g2"], W["k_k"], W["k_a"], W["r_k"])
    if prompt:
        seqs = _rwkv_prep(proj, shift0, *prep_w, tt=256, seq_dtype=bf16)
        main, new_wkv = _wkv_chunked(*seqs, W["lnx_w"], W["lnx_b"], tt=512)
    else:
        steps = jnp.transpose(proj[:, :t_real, :RWKV_COLS], (1, 0, 2)).reshape(1, t_real * bsz, RWKV_COLS)
        seqs = _rwkv_prep(steps, shift0, *prep_w, tt=None, step_major=True)
        seqs = [s.reshape(t_real, bsz, MAIN_WIDTH) for s in seqs]
        main, new_wkv = _wkv_steps(wkv0, seqs, W["lnx_w"], W["lnx_b"], tp)
    mem_o = _mem_attn(proj, RWKV_COLS // MEM_WIDTH, W["mem_qnorm"][0], mem_kt, mem_vt, 0, 256, mem_bb, act_dtype)
    h = _out_mlp(flat(x), flat(main), flat(mem_o), W["w_out"][0], W["norm_mlp"][0], W["w_up"][0],
                 W["w_down"][0], tm)

    kv = _norm_matmul(h, W["kv_norm"], W["w_kv"], tm)
    if prompt:
        k_sh, v_sh = _kv_post(kv, W["swa_knorm"], cos4, sin4, tm, True)
    else:
        k_sh, v_sh = _kv_post(kv, W["swa_knorm"], jnp.tile(cos4, (tm // tp, 1)), jnp.tile(sin4, (tm // tp, 1)),
                              tm, True)
    k_sh, v_sh = unflat(k_sh), unflat(v_sh)

    proj = unflat(_norm_matmul(h, W["norm_mix"][1], W["w_in_b"], tm))
    if prompt:
        main = _swa(proj, W["swa_qnorm"], W["sinks"], cos2, sin2, k_sh, k_sh, v_sh, v_sh, BLOCK,
                    lambda i: jnp.maximum(i - 1, 0), True)
        win = min(WINDOW, tp)
        heads = lambda a: a.reshape(bsz, a.shape[1], SWA_KV_HEADS, HEAD_DIM)
        k_state, v_state = heads(k_sh[:, tp - win:]), heads(v_sh[:, tp - win:])
    else:
        ckt, cvt = swa_cache
        main, kt_new, vt_new = _swa_step(proj, W["swa_qnorm"], W["sinks"], cos2, sin2, ckt, cvt, k_sh, v_sh,
                                         t_real, bb=4)
        untransposed = lambda a: jnp.transpose(a.reshape(bsz, SWA_KV_HEADS, HEAD_DIM, -1), (0, 3, 1, 2))
        k_state, v_state = untransposed(kt_new), untransposed(vt_new)
    mem_o = _mem_attn(proj, MAIN_WIDTH // MEM_WIDTH, W["mem_qnorm"][1], mem_kt, mem_vt, 1, 256, mem_bb, act_dtype)
    y = _out_mlp(h, flat(main), flat(mem_o), W["w_out"][1], W["norm_mlp"][1], W["w_up"][1], W["w_down"][1], tm)
    return unflat(y)[:, :t_real], new_shift[None], new_wkv[None], k_state, v_state


def kernel(x_prompt, x_sample, state_rwkv_shift, state_rwkv_wkv, cache_swa_k, cache_swa_v, cache_mem_k,
           cache_mem_v, mem_prompt, norm_mix, norm_mlp, w_out, w_up, w_down, mem_norm, w_mem_kv, mem_qnorm,
           mem_knorm, w_in_a, shift_mu, w_w2, w0, w_a2, a0, w_g2, k_k, k_a, r_k, lnx_w, lnx_b, w_in_b,
           swa_qnorm, sinks, kv_norm, w_kv, swa_knorm):
    W = dict(norm_mix=norm_mix, norm_mlp=norm_mlp, w_out=w_out.astype(bf16), w_up=w_up.astype(bf16),
             w_down=w_down.astype(bf16), mem_qnorm=mem_qnorm, w_in_a=w_in_a[0].astype(bf16),
             shift_mu=shift_mu[0], w_w2=w_w2[0], w0=w0[0], w_a2=w_a2[0], a0=a0[0], w_g2=w_g2[0],
             k_k=k_k[0], k_a=k_a[0], r_k=r_k[0].reshape(-1), lnx_w=lnx_w[0], lnx_b=lnx_b[0],
             w_in_b=w_in_b[0].astype(bf16), swa_qnorm=swa_qnorm[0], sinks=sinks[0], kv_norm=kv_norm,
             w_kv=w_kv.astype(bf16), swa_knorm=swa_knorm)
    bp, tp, _ = x_prompt.shape
    bs, ts, _ = x_sample.shape
    depth = norm_mix.shape[0]

    mem_flat = mem_prompt.reshape(bp * N_MEM, D_MODEL)
    ones = jnp.ones((N_MEM, KV_WIDTH), f32)
    p_mem_k, p_mem_v = [], []
    for l in range(depth):
        kv = _norm_matmul(mem_flat, mem_norm[l], w_mem_kv[l].astype(bf16), 512)
        mk, mv = _kv_post(kv, mem_knorm[l], ones, ones, N_MEM, False)
        p_mem_k.append(mk.reshape(bp, N_MEM, MEM_WIDTH))
        p_mem_v.append(mv.reshape(bp, N_MEM, MEM_WIDTH))
    transposed = lambda a: jnp.swapaxes(jnp.stack(a), 2, 3)
    y_p, p_shift, p_wkv, p_k, p_v = _trunk(
        x_prompt, jnp.arange(tp), transposed(p_mem_k), transposed(p_mem_v), jnp.zeros((bp, RWKV_COLS), f32),
        None, None, tp, W)
    mem_heads = lambda a: jnp.stack(a).reshape(depth, bp, N_MEM, MEM_HEADS, HEAD_DIM)
    cache_t = lambda a: jnp.transpose(a, (0, 1, 3, 4, 2)).reshape(depth, bs, MEM_WIDTH, N_MEM)

    tpad = -(-ts // SUBLANES) * SUBLANES
    x_s = jnp.pad(x_sample, ((0, 0), (0, tpad - ts), (0, 0)))
    win = cache_swa_k.shape[1]
    assert win == WINDOW
    swa_t = lambda a: jnp.transpose(a, (0, 2, 3, 1)).reshape(bs, KV_WIDTH, win)
    y_s, s_shift, s_wkv, s_k, s_v = _trunk(
        x_s, PAST_LEN + jnp.arange(tpad), cache_t(cache_mem_k), cache_t(cache_mem_v),
        state_rwkv_shift[0], state_rwkv_wkv[0],
        (swa_t(cache_swa_k), swa_t(cache_swa_v)), ts, W)

    return (y_p, y_s, p_shift, p_wkv, p_k, p_v, mem_heads(p_mem_k), mem_heads(p_mem_v),
            s_shift, s_wkv, s_k, s_v)
```

```python
import functools

import numpy as np
import jax
import jax.numpy as jnp
from jax import lax
from jax.experimental import pallas as pl
from jax.experimental.pallas import tpu as pltpu

f32 = jnp.float32
bf16 = jnp.bfloat16

D_MODEL = 1024
HEAD_DIM = 64
MEM_HEADS = 4
MEM_WIDTH = MEM_HEADS * HEAD_DIM
MAIN_WIDTH = D_MODEL - MEM_WIDTH
RWKV_HEADS = MAIN_WIDTH // HEAD_DIM
DECAY_LORA = 64
AAA_LORA = 64
GATE_LORA = 128
RWKV_COLS = 3 * MAIN_WIDTH + DECAY_LORA + AAA_LORA + GATE_LORA
A_IN_COLS = RWKV_COLS + MEM_WIDTH
SWA_Q_HEADS = MAIN_WIDTH // HEAD_DIM
SWA_KV_HEADS = 4
SWA_GROUP = SWA_Q_HEADS // SWA_KV_HEADS
KV_WIDTH = SWA_KV_HEADS * HEAD_DIM
WINDOW = 128
BLOCK = 128
N_MEM = 256
D_FF = 4 * D_MODEL
PAST_LEN = 16384
ROPE_THETA = 10000.0
NORM_EPS = 1e-6
LNX_EPS = 6.4e-4
L2_EPS = 1e-12
ATTN_SCALE = HEAD_DIM ** -0.5

LANES = 128
SUBLANES = 8
HEADS_PER_SLAB = LANES // HEAD_DIM
N_SLABS = MAIN_WIDTH // LANES
CHUNK = 64
VMEM_LIMIT = 56 * 1024 * 1024
NEG_BIG = -1e30


def _cp(sem, vmem=VMEM_LIMIT):
    return pltpu.CompilerParams(dimension_semantics=sem, vmem_limit_bytes=vmem)


def _dot(a, b, precision=None):
    return jnp.dot(a, b, precision=precision, preferred_element_type=f32)


def _dot_nt(a, b, precision=None):
    return lax.dot_general(a, b, (((1,), (1,)), ((), ())), precision=precision,
                           preferred_element_type=f32)


def _dot_tn(a, b, precision=None):
    return lax.dot_general(a, b, (((0,), (0,)), ((), ())), precision=precision,
                           preferred_element_type=f32)


def _iota(shape, dim):
    return lax.broadcasted_iota(jnp.int32, shape, dim)


def _group_ones():
    r = _iota((LANES, LANES), 0)
    c = _iota((LANES, LANES), 1)
    return jnp.where((r < HEAD_DIM) == (c < HEAD_DIM), 1.0, 0.0).astype(f32)


def _head_sum(x, gmat):
    w = x.shape[-1]
    gb = gmat.astype(bf16)
    hi = x.astype(bf16)
    lo = (x - hi.astype(f32)).astype(bf16)
    parts = [_dot(hi[:, j * LANES:(j + 1) * LANES], gb) + _dot(lo[:, j * LANES:(j + 1) * LANES], gb)
             for j in range(w // LANES)]
    return parts[0] if len(parts) == 1 else jnp.concatenate(parts, axis=-1)


def _cumsum_rows(x):
    rows = x.shape[0]
    row = _iota(x.shape, 0)
    step = 1
    while step < rows:
        x = x + jnp.where(row >= step, pltpu.roll(x, step, 0), 0.0)
        step *= 2
    return x


def _sigmoid(x):
    return 1.0 / (1.0 + jnp.exp(-x))


def _softplus(x):
    return jnp.maximum(x, 0.0) + jnp.log1p(jnp.exp(-jnp.abs(x)))


def _rot_half(x):
    lane = _iota(x.shape, 1)
    first = (lane & (HEAD_DIM - 1)) < (HEAD_DIM // 2)
    return jnp.where(first, pltpu.roll(x, LANES - HEAD_DIM // 2, 1), pltpu.roll(x, HEAD_DIM // 2, 1))


def _norm_matmul_kernel(x_ref, g_ref, w_ref, o_ref):
    x = x_ref[...]
    xn = x * lax.rsqrt(jnp.mean(x * x, axis=-1, keepdims=True) + NORM_EPS) * g_ref[...]
    o_ref[...] = _dot(xn.astype(bf16), w_ref[...])


def _norm_matmul(x, g, w_bf16, tm):
    n, d = x.shape
    m = w_bf16.shape[1]
    tm = min(tm, n)
    return pl.pallas_call(
        _norm_matmul_kernel,
        grid=(n // tm,),
        in_specs=[pl.BlockSpec((tm, d), lambda i: (i, 0)),
                  pl.BlockSpec((1, d), lambda i: (0, 0)),
                  pl.BlockSpec((d, m), lambda i: (0, 0))],
        out_specs=pl.BlockSpec((tm, m), lambda i: (i, 0)),
        out_shape=jax.ShapeDtypeStruct((n, m), f32),
        compiler_params=_cp(("parallel",)),
        name="norm_matmul",
    )(x, g.reshape(1, d), w_bf16)


def _kv_post_kernel(kv_ref, g_ref, cos_ref, sin_ref, k_ref, v_ref, *, use_rope):
    gmat = _group_ones()
    x = kv_ref[:, :KV_WIDTH]
    ms = _head_sum(x * x, gmat) * (1.0 / HEAD_DIM)
    xn = x * lax.rsqrt(ms + NORM_EPS) * g_ref[...]
    if use_rope:
        for j in range(KV_WIDTH // LANES):
            sl = slice(j * LANES, (j + 1) * LANES)
            xs = xn[:, sl]
            k_ref[:, sl] = xs * cos_ref[:, sl] + _rot_half(xs) * sin_ref[:, sl]
    else:
        k_ref[...] = xn
    v_ref[...] = kv_ref[:, KV_WIDTH:]


def _kv_post(kv, g_head, cos, sin, tm, use_rope):
    n = kv.shape[0]
    tm = min(tm, n)
    nt = cos.shape[0] // tm
    g = jnp.tile(g_head.reshape(1, HEAD_DIM), (1, KV_WIDTH // HEAD_DIM))
    return pl.pallas_call(
        functools.partial(_kv_post_kernel, use_rope=use_rope),
        grid=(n // tm,),
        in_specs=[pl.BlockSpec((tm, 2 * KV_WIDTH), lambda i: (i, 0)),
                  pl.BlockSpec((1, KV_WIDTH), lambda i: (0, 0)),
                  pl.BlockSpec((tm, KV_WIDTH), lambda i: (i % nt, 0)),
                  pl.BlockSpec((tm, KV_WIDTH), lambda i: (i % nt, 0))],
        out_specs=[pl.BlockSpec((tm, KV_WIDTH), lambda i: (i, 0)),
                   pl.BlockSpec((tm, KV_WIDTH), lambda i: (i, 0))],
        out_shape=[jax.ShapeDtypeStruct((n, KV_WIDTH), f32)] * 2,
        compiler_params=_cp(("parallel",)),
        name="kv_post",
    )(kv, g, cos, sin)


def _rwkv_prep_kernel(p_ref, pprev_ref, shift_ref, mu_ref, ww2_ref, w0_ref, wa2_ref, a0_ref, wg2_ref,
                      kk_ref, ka_ref, rk_ref,
                      r_o, lw_o, k_o, v_o, a_o, b_o, bonus_o, g_o, *, step_major):
    i = pl.program_id(1)
    gmat = _group_ones()
    pf = p_ref[0]
    if not step_major:
        first_prev = jnp.where(i == 0, shift_ref[0], pprev_ref[0][SUBLANES - 1:SUBLANES, :])
        prev = jnp.where(_iota(pf.shape, 0) == 0, first_prev, pltpu.roll(pf, 1, 0))
    else:
        prev = jnp.concatenate([shift_ref[0], pf[:pf.shape[0] - shift_ref.shape[1]]], axis=0)
    ps = pf + (prev - pf) * mu_ref[...]
    i1, i2, i3 = MAIN_WIDTH, 2 * MAIN_WIDTH, 3 * MAIN_WIDTH
    i4, i5 = i3 + DECAY_LORA, i3 + DECAY_LORA + AAA_LORA
    r, k, v = ps[:, :i1], ps[:, i1:i2], ps[:, i2:i3]
    wd, ad, gd = ps[:, i3:i4], ps[:, i4:i5], ps[:, i5:]
    lora = lambda x, w_ref: _dot(x.astype(bf16), w_ref[...].astype(bf16))
    w_log = -_softplus(-(w0_ref[...] + lora(jnp.tanh(wd), ww2_ref))) - 0.5
    lw = -jnp.exp(w_log)
    a_sig = _sigmoid(a0_ref[...] + lora(ad, wa2_ref))
    g = lora(_sigmoid(gd), wg2_ref)
    kk = k * kk_ref[...]
    kk = kk / jnp.maximum(jnp.sqrt(_head_sum(kk * kk, gmat)), L2_EPS)
    k2 = k * (1.0 + (a_sig - 1.0) * ka_ref[...])
    r_o[0] = r.astype(r_o.dtype)
    lw_o[0] = lw
    k_o[0] = k2.astype(k_o.dtype)
    v_o[0] = v.astype(v_o.dtype)
    a_o[0] = (-kk).astype(a_o.dtype)
    b_o[0] = (kk * a_sig).astype(b_o.dtype)
    bonus_o[0] = (_head_sum(r * k2 * rk_ref[...], gmat) * v).astype(bonus_o.dtype)
    g_o[0] = g.astype(g_o.dtype)


def _rwkv_prep(proj, shift_prev, mu, w_w2, w0, w_a2, a0, w_g2, k_k, k_a, r_k, tt, step_major=False,
               seq_dtype=f32):
    bsz, t, _ = proj.shape
    tt = t if step_major else min(tt, t)
    shift_rows = shift_prev.reshape(bsz, -1, RWKV_COLS)
    row = lambda x: x.reshape(1, -1)
    full = lambda a: pl.BlockSpec(a.shape, lambda b, i: (0,) * a.ndim)
    consts = [row(mu), w_w2, row(w0), w_a2, row(a0), w_g2, row(k_k), row(k_a), row(r_k)]
    out_spec = pl.BlockSpec((1, tt, MAIN_WIDTH), lambda b, i: (b, i, 0))
    return pl.pallas_call(
        functools.partial(_rwkv_prep_kernel, step_major=step_major),
        grid=(bsz, t // tt),
        in_specs=[pl.BlockSpec((1, tt, RWKV_COLS), lambda b, i: (b, i, 0)),
                  pl.BlockSpec((1, SUBLANES, RWKV_COLS),
                               lambda b, i: (b, jnp.maximum(i * (tt // SUBLANES) - 1, 0), 0)),
                  pl.BlockSpec((1,) + shift_rows.shape[1:], lambda b, i: (b, 0, 0))]
                 + [full(c) for c in consts],
        out_specs=[out_spec] * 8,
        out_shape=[jax.ShapeDtypeStruct((bsz, t, MAIN_WIDTH), f32 if j == 1 else seq_dtype) for j in range(8)],
        compiler_params=_cp(("parallel", "arbitrary")),
        name="rwkv_prep",
    )(proj, proj, shift_rows, *consts)


def _wkv_chunk_kernel(r_ref, lw_ref, k_ref, v_ref, a_ref, b_ref, bonus_ref, g_ref, lnw_ref, lnb_ref,
                      o_ref, s_ref, h_ref, qt_ref, yv_ref, pp_ref, hv_ref, tc_ref):
    i = pl.program_id(2)
    c = CHUNK
    nc = r_ref.shape[1] // c
    n = range(nc)

    @pl.when(i == 0)
    def _():
        h_ref[...] = jnp.zeros_like(h_ref)
        qt_ref[...] = jnp.zeros_like(qt_ref)
        yv_ref[...] = jnp.zeros_like(yv_ref)
        pp_ref[...] = jnp.zeros_like(pp_ref)
        hv_ref[...] = jnp.zeros_like(hv_ref)
        tc_ref[...] = jnp.zeros_like(tc_ref)

    gmat = _group_ones()
    bd = gmat > 0.5
    row = _iota((LANES, LANES), 0)
    col = _iota((LANES, LANES), 1)
    t_row = row & (c - 1)
    s_col = col & (c - 1)
    m_strict = s_col < t_row
    m_incl = s_col <= t_row
    top = row < c
    lane_lo = _iota((1, LANES), 1) < HEAD_DIM
    eye = row == col
    zeros = jnp.zeros((c, LANES), f32)
    mm = lambda x: x.astype(bf16)
    gb = gmat.astype(bf16)
    sls = [slice(j * c, (j + 1) * c) for j in n]

    state = {"h": h_ref[...], "ys": []}

    def state_step(j):
        hb = mm(state["h"])
        state["ys"].append(_dot(qt_ref[j], hb) + yv_ref[j])
        state["h"] = state["h"] * tc_ref[j] + _dot(pp_ref[j], hb) + hv_ref[j]

    lws = [lw_ref[0, sl, :] for sl in sls]
    cums = [_cumsum_rows(lw) for lw in lws]
    state_step(0)
    vs = [v_ref[0, sl, :].astype(f32) for sl in sls]
    tots = [cum[c - 1:c, :] for cum in cums]
    ats = [a_ref[0, sl, :].astype(f32) * jnp.exp(cum - lw) for sl, cum, lw in zip(sls, cums, lws)]
    rts = [r_ref[0, sl, :].astype(f32) * jnp.exp(cum) for sl, cum in zip(sls, cums)]
    bks, bkts = [], []
    for sl, cum, tot in zip(sls, cums, tots):
        b, k = b_ref[0, sl, :].astype(f32), k_ref[0, sl, :].astype(f32)
        e_neg, e_rem = jnp.exp(-cum), jnp.exp(tot - cum)
        bks.append(mm(jnp.concatenate([b * e_neg, k * e_neg], axis=0)))
        bkts.append(mm(jnp.concatenate([b * e_rem, k * e_rem], axis=0)))
    lhss = [mm(jnp.concatenate([jnp.where(lane_lo, at, 0.0), jnp.where(lane_lo, 0.0, at),
                                jnp.where(lane_lo, rt, 0.0), jnp.where(lane_lo, 0.0, rt)], axis=0))
            for at, rt in zip(ats, rts)]
    gms = [_dot_nt(lhss[j], bks[j]) for j in n]
    state_step(1)
    aaks = [jnp.where(m_strict, gm[:LANES], 0.0) for gm in gms]
    arks = [mm(jnp.where(m_incl, gm[LANES:], 0.0)) for gm in gms]
    aak_sws = [pltpu.roll(aak, HEAD_DIM, 1) for aak in aaks]
    ms = [jnp.where(bd, jnp.where(top, aaks[j], aak_sws[j]), 0.0) for j in n]
    aks = [mm(jnp.where(bd, jnp.where(top, aak_sws[j], aaks[j]), 0.0)) for j in n]
    vvs = [mm(jnp.concatenate([pltpu.roll(v, HEAD_DIM, 1)] * 2, axis=0)) for v in vs]
    akvs = [_dot(aks[j], vvs[j]) for j in n]
    state_step(2)
    xs = [jnp.where(bd, jnp.concatenate([ats[j], ats[j]], axis=0), akvs[j]) for j in n]
    means = None
    for lvl in range(6):
        if lvl < 5:
            zs = [_dot(mm(ms[j]), mm(jnp.concatenate([xs[j], ms[j]], axis=1))) for j in n]
            xs = [xs[j] + zs[j][:, :LANES] for j in n]
            ms = [zs[j][:, LANES:] for j in n]
            state_step(3 + lvl)
        else:
            zs = [_dot(mm(ms[j]), mm(xs[j])) for j in n]
            xs = [xs[j] + zs[j] for j in n]
            means = [_dot(mm(y), gb) * (1.0 / HEAD_DIM) for y in state["ys"]]
    assert len(state["ys"]) == nc == 8
    h_ref[...] = state["h"]
    rhss = []
    for j in n:
        x = xs[j]
        ta = jnp.where(lane_lo, x[:c], x[c:])
        uv = pltpu.roll(jnp.where(lane_lo, x[c:], x[:c]), HEAD_DIM, 1)
        rhss.append(mm(jnp.concatenate([jnp.concatenate([ta, uv], axis=1),
                                        jnp.concatenate([zeros, vs[j]], axis=1)], axis=0)))
    z2s = [_dot(arks[j], rhss[j]) for j in n]
    pzs = [_dot_tn(bkts[j], rhss[j]) for j in n]
    ycs = [state["ys"][j] - means[j] for j in n]
    vrs = [_dot(mm(yc * yc), gb) * (1.0 / HEAD_DIM) for yc in ycs]
    for j in n:
        yn = ycs[j] * lax.rsqrt(vrs[j] + LNX_EPS) * lnw_ref[...] + lnb_ref[...]
        o_ref[0, sls[j], :] = ((yn + bonus_ref[0, sls[j], :].astype(f32))
                               * g_ref[0, sls[j], :].astype(f32)).astype(o_ref.dtype)
    for j in n:
        qt_ref[j] = mm(rts[j] + jnp.where(lane_lo, z2s[j][:c, :LANES], z2s[j][c:, :LANES]))
        yv_ref[j] = jnp.where(lane_lo, z2s[j][:c, LANES:], z2s[j][c:, LANES:])
        pp_ref[j] = mm(jnp.where(bd, pzs[j][:, :LANES], 0.0))
        hv_ref[j] = jnp.where(bd, pzs[j][:, LANES:], 0.0)
        tc_ref[j] = jnp.sum(jnp.where(eye, jnp.exp(tots[j]), 0.0), axis=1, keepdims=True)

    @pl.when(i == pl.num_programs(2) - 1)
    def _():
        s_ref[0, 0] = h_ref[...].T


def _wkv_chunked(r, lw, k, v, a, b, bonus, g, lnx_w, lnx_b):
    bsz, t, _ = r.shape
    nc = 8
    tt = nc * CHUNK
    nt = t // tt
    assert t % tt == 0
    cur = pl.BlockSpec((1, tt, LANES), lambda bb, hp, i: (bb, jnp.minimum(i, nt - 1), hp))
    pend = pl.BlockSpec((1, tt, LANES), lambda bb, hp, i: (bb, jnp.maximum(i - 1, 0), hp))
    vec = pl.BlockSpec((1, LANES), lambda bb, hp, i: (0, hp))
    out, st = pl.pallas_call(
        _wkv_chunk_kernel,
        grid=(bsz, N_SLABS, nt + 1),
        in_specs=[cur] * 6 + [pend, pend, vec, vec],
        out_specs=[pend, pl.BlockSpec((1, 1, LANES, LANES), lambda bb, hp, i: (bb, hp, 0, 0))],
        out_shape=[jax.ShapeDtypeStruct((bsz, t, MAIN_WIDTH), bf16),
                   jax.ShapeDtypeStruct((bsz, N_SLABS, LANES, LANES), f32)],
        scratch_shapes=[pltpu.VMEM((LANES, LANES), f32),
                        pltpu.VMEM((nc, CHUNK, LANES), bf16), pltpu.VMEM((nc, CHUNK, LANES), f32),
                        pltpu.VMEM((nc, LANES, LANES), bf16), pltpu.VMEM((nc, LANES, LANES), f32),
                        pltpu.VMEM((nc, LANES, 1), f32)],
        compiler_params=_cp(("parallel", "parallel", "arbitrary")),
        name="wkv_chunked",
    )(r, lw, k, v, a, b, bonus, g, lnx_w.reshape(1, -1), lnx_b.reshape(1, -1))
    st = jnp.stack([st[:, :, :HEAD_DIM, :HEAD_DIM], st[:, :, HEAD_DIM:, HEAD_DIM:]], axis=2)
    return out, st.reshape(bsz, RWKV_HEADS, HEAD_DIM, HEAD_DIM)


def _wkv_steps_kernel(s_ref, seq_ref, lnw_ref, lnb_ref, o_ref, so_ref, w_ref, y_ref, *, n_steps):
    i_r, i_lw, i_k, i_v, i_a, i_b, i_bonus, i_g = range(8)
    w_ref[...] = jnp.exp(seq_ref[i_lw])
    sub = _iota((SUBLANES, LANES), 0)

    def group(vg, carry):
        v0 = pl.multiple_of(vg * SUBLANES, SUBLANES)
        v_rows = [seq_ref[i_v, t, pl.ds(v0, SUBLANES), :] for t in range(n_steps)]
        ys = [jnp.zeros((SUBLANES, LANES), f32) for _ in range(n_steps)]
        for j in range(SUBLANES):
            sv = s_ref[0, v0 + j]
            for t in range(n_steps):
                sa = jnp.sum(sv * seq_ref[i_a, t], axis=0, keepdims=True)
                sv = sv * w_ref[t] + sa * seq_ref[i_b, t] + v_rows[t][j:j + 1, :] * seq_ref[i_k, t]
                y = jnp.sum(sv * seq_ref[i_r, t], axis=0, keepdims=True)
                ys[t] = jnp.where(sub == j, y, ys[t])
            so_ref[0, v0 + j] = sv
        for t in range(n_steps):
            y_ref[t, pl.ds(v0, SUBLANES), :] = ys[t]
        return carry

    lax.fori_loop(0, HEAD_DIM // SUBLANES, group, 0)
    y = y_ref[...]
    mean = jnp.mean(y, axis=1, keepdims=True)
    yc = y - mean
    var = jnp.mean(yc * yc, axis=1, keepdims=True)
    yn = yc * lax.rsqrt(var + LNX_EPS) * lnw_ref[...] + lnb_ref[...]
    o_ref[...] = (yn + seq_ref[i_bonus]) * seq_ref[i_g]


def _wkv_steps(state, seqs, lnx_w, lnx_b, tp):
    n_steps, bsz, _ = seqs[0].shape
    assert bsz % LANES == 0
    state_t = jnp.transpose(state, (1, 2, 3, 0))
    seq_t = jnp.transpose(jnp.stack(seqs), (0, 1, 3, 2))
    lanes = lambda x: jnp.broadcast_to(x.reshape(MAIN_WIDTH, 1), (MAIN_WIDTH, LANES))
    sspec = pl.BlockSpec((1, HEAD_DIM, HEAD_DIM, LANES), lambda h, bi: (h, 0, 0, bi))
    vec = pl.BlockSpec((HEAD_DIM, LANES), lambda h, bi: (h, 0))
    ospec = pl.BlockSpec((n_steps, HEAD_DIM, LANES), lambda h, bi: (0, h, bi))
    out, new_state = pl.pallas_call(
        functools.partial(_wkv_steps_kernel, n_steps=n_steps),
        grid=(RWKV_HEADS, bsz // LANES),
        in_specs=[sspec, pl.BlockSpec((8, n_steps, HEAD_DIM, LANES), lambda h, bi: (0, 0, h, bi)), vec, vec],
        out_specs=[ospec, sspec],
        out_shape=[jax.ShapeDtypeStruct((n_steps, MAIN_WIDTH, bsz), f32),
                   jax.ShapeDtypeStruct(state_t.shape, f32)],
        scratch_shapes=[pltpu.VMEM((n_steps, HEAD_DIM, LANES), f32), pltpu.VMEM((n_steps, HEAD_DIM, LANES), f32)],
        compiler_params=_cp(("parallel", "parallel")),
        name="wkv_steps",
    )(state_t, seq_t, lanes(lnx_w), lanes(lnx_b))
    out = jnp.pad(jnp.transpose(out, (2, 0, 1)), ((0, 0), (0, tp - n_steps), (0, 0)))
    return out, jnp.transpose(new_state, (3, 0, 1, 2))


def _mem_attn_kernel(q_ref, g_ref, kt_ref, vt_ref, o_ref):
    gmat = _group_ones()
    bb, tq, _ = q_ref.shape
    head = _iota((1, MEM_WIDTH), 1) // HEAD_DIM
    for b in range(bb):
        q = q_ref[b]
        qn = q * lax.rsqrt(_head_sum(q * q, gmat) * (1.0 / HEAD_DIM) + NORM_EPS) * g_ref[...]
        qs = jnp.concatenate([jnp.where(head == h, qn, 0.0) for h in range(MEM_HEADS)], axis=0)
        s = _dot(qs.astype(bf16), kt_ref[0, b].astype(bf16)) * ATTN_SCALE
        e = jnp.exp(s - jnp.max(s, axis=-1, keepdims=True))
        o4 = _dot_nt(e.astype(bf16), vt_ref[0, b].astype(bf16)) / jnp.sum(e, axis=-1, keepdims=True)
        o = jnp.zeros((tq, MEM_WIDTH), f32)
        for h in range(MEM_HEADS):
            o = o + jnp.where(head == h, o4[h * tq:(h + 1) * tq], 0.0)
        o_ref[b] = o.astype(o_ref.dtype)


def _mem_attn(proj, col_block, g_qnorm, mem_kt, mem_vt, layer, tq, bb, out_dtype):
    bsz, t, _ = proj.shape
    tq = min(tq, t)
    g = jnp.tile(g_qnorm.reshape(1, HEAD_DIM), (1, MEM_HEADS))
    kv_spec = pl.BlockSpec((1, bb, MEM_WIDTH, N_MEM), lambda b, i: (layer, b, 0, 0))
    return pl.pallas_call(
        _mem_attn_kernel,
        grid=(bsz // bb, t // tq),
        in_specs=[pl.BlockSpec((bb, tq, MEM_WIDTH), lambda b, i: (b, i, col_block)),
                  pl.BlockSpec((1, MEM_WIDTH), lambda b, i: (0, 0)),
                  kv_spec, kv_spec],
        out_specs=pl.BlockSpec((bb, tq, MEM_WIDTH), lambda b, i: (b, i, 0)),
        out_shape=jax.ShapeDtypeStruct((bsz, t, MEM_WIDTH), out_dtype),
        compiler_params=_cp(("parallel", "parallel")),
        name="mem_attn",
    )(proj, g, mem_kt, mem_vt)


def _swa_kernel(sink_ref, q_ref, g_ref, cos_ref, sin_ref, kp_ref, kc_ref, vp_ref, vc_ref, o_ref, qn_ref,
                *, first_block_has_no_past):
    i = pl.program_id(1)
    gmat = _group_ones()
    tq = q_ref.shape[1]
    for j in range(N_SLABS):
        sl = slice(j * LANES, (j + 1) * LANES)
        x = q_ref[0, :, sl]
        xn = x * lax.rsqrt(_head_sum(x * x, gmat) * (1.0 / HEAD_DIM) + NORM_EPS) * g_ref[...]
        qn_ref[:, sl] = xn * cos_ref[...] + _rot_half(xn) * sin_ref[...]
    kc, vc = kc_ref[0], vc_ref[0]
    if tq < BLOCK:
        pad = jnp.zeros((BLOCK - tq, KV_WIDTH), f32)
        kc = jnp.concatenate([kc, pad], axis=0)
        vc = jnp.concatenate([vc, pad], axis=0)
    keys = jnp.concatenate([kp_ref[0], kc], axis=0).astype(bf16)
    vals = jnp.concatenate([vp_ref[0], vc], axis=0).astype(bf16)
    rows = SWA_GROUP * tq
    row = _iota((rows, 2 * BLOCK), 0) & (tq - 1)
    col = _iota((rows, 2 * BLOCK), 1)
    mask = (col > row) & (col <= row + WINDOW)
    if first_block_has_no_past:
        mask = mask & ((i > 0) | (col >= BLOCK))
    grp = _iota((rows, 1), 0) // tq
    kv = range(SWA_KV_HEADS)
    hs = [slice(hk * HEAD_DIM, (hk + 1) * HEAD_DIM) for hk in kv]
    q3s = [jnp.concatenate([qn_ref[:, (hk * SWA_GROUP + gq) * HEAD_DIM:(hk * SWA_GROUP + gq + 1) * HEAD_DIM]
                            for gq in range(SWA_GROUP)], axis=0).astype(bf16) for hk in kv]
    ss = [jnp.where(mask, _dot_nt(q3s[hk], keys[:, hs[hk]]) * ATTN_SCALE, NEG_BIG) for hk in kv]
    sinks = [jnp.where(grp == 0, sink_ref[hk * SWA_GROUP],
                       jnp.where(grp == 1, sink_ref[hk * SWA_GROUP + 1], sink_ref[hk * SWA_GROUP + 2])) for hk in kv]
    ms = [jnp.maximum(jnp.max(ss[hk], axis=-1, keepdims=True), sinks[hk]) for hk in kv]
    es = [jnp.exp(ss[hk] - ms[hk]) for hk in kv]
    denoms = [jnp.sum(es[hk], axis=-1, keepdims=True) + jnp.exp(sinks[hk] - ms[hk]) for hk in kv]
    outs = [_dot(es[hk].astype(bf16), vals[:, hs[hk]]) / denoms[hk] for hk in kv]
    for hk in kv:
        for gq in range(SWA_GROUP):
            hq = hk * SWA_GROUP + gq
            o_ref[0, :, hq * HEAD_DIM:(hq + 1) * HEAD_DIM] = outs[hk][gq * tq:(gq + 1) * tq].astype(o_ref.dtype)


def _swa(proj, g_qnorm, sinks, cos, sin, k_prev, k_cur, v_prev, v_cur, tq, prev_index, first_block_has_no_past):
    bsz, t, _ = proj.shape
    g = jnp.tile(g_qnorm.reshape(1, HEAD_DIM), (1, HEADS_PER_SLAB))
    nq = t // tq
    return pl.pallas_call(
        functools.partial(_swa_kernel, first_block_has_no_past=first_block_has_no_past),
        grid=(bsz, nq),
        in_specs=[pl.BlockSpec(memory_space=pltpu.SMEM),
                  pl.BlockSpec((1, tq, MAIN_WIDTH), lambda b, i: (b, i, 0)),
                  pl.BlockSpec((1, LANES), lambda b, i: (0, 0)),
                  pl.BlockSpec((tq, LANES), lambda b, i: (i, 0)),
                  pl.BlockSpec((tq, LANES), lambda b, i: (i, 0)),
                  pl.BlockSpec((1, BLOCK, KV_WIDTH), lambda b, i: (b, prev_index(i), 0)),
                  pl.BlockSpec((1, tq, KV_WIDTH), lambda b, i: (b, i, 0)),
                  pl.BlockSpec((1, BLOCK, KV_WIDTH), lambda b, i: (b, prev_index(i), 0)),
                  pl.BlockSpec((1, tq, KV_WIDTH), lambda b, i: (b, i, 0))],
        out_specs=pl.BlockSpec((1, tq, MAIN_WIDTH), lambda b, i: (b, i, 0)),
        out_shape=jax.ShapeDtypeStruct((bsz, t, MAIN_WIDTH), bf16),
        scratch_shapes=[pltpu.VMEM((tq, MAIN_WIDTH), f32)],
        compiler_params=_cp(("parallel", "parallel")),
        name="swa_attn",
    )(sinks, proj, g, cos, sin, k_prev, k_cur, v_prev, v_cur)


def _swa_step_kernel(sink_ref, q_ref, g_ref, cos_ref, sin_ref, kt_ref, vt_ref, kn_ref, vn_ref,
                     o_ref, kto_ref, vto_ref, qn_ref, *, n_new):
    gmat = _group_ones()
    bb, tq, _ = q_ref.shape
    row = _iota((SWA_GROUP * tq, 2 * BLOCK), 0) & (tq - 1)
    col = _iota((SWA_GROUP * tq, 2 * BLOCK), 1)
    mask = (col > row) & (col <= row + WINDOW)
    grp = _iota((SWA_GROUP * tq, 1), 0) // tq
    lane = _iota((KV_WIDTH, BLOCK), 1)
    pad = jnp.zeros((BLOCK - tq, KV_WIDTH), f32)
    for b in range(bb):
        for j in range(N_SLABS):
            sl = slice(j * LANES, (j + 1) * LANES)
            x = q_ref[b, :, sl]
            xn = x * lax.rsqrt(_head_sum(x * x, gmat) * (1.0 / HEAD_DIM) + NORM_EPS) * g_ref[...]
            qn_ref[:, sl] = xn * cos_ref[...] + _rot_half(xn) * sin_ref[...]
        kt, vt = kt_ref[b], vt_ref[b]
        kn = jnp.concatenate([kn_ref[b], pad], axis=0)
        vn = jnp.concatenate([vn_ref[b], pad], axis=0)
        keep = lane < BLOCK - n_new
        kto_ref[b] = jnp.where(keep, pltpu.roll(kt, BLOCK - n_new, 1), pltpu.roll(kn.T, BLOCK - n_new, 1))
        vto_ref[b] = jnp.where(keep, pltpu.roll(vt, BLOCK - n_new, 1), pltpu.roll(vn.T, BLOCK - n_new, 1))
        ktb, vtb, knb, vnb = kt.astype(bf16), vt.astype(bf16), kn.astype(bf16), vn.astype(bf16)
        for hk in range(SWA_KV_HEADS):
            hs = slice(hk * HEAD_DIM, (hk + 1) * HEAD_DIM)
            q3 = jnp.concatenate([qn_ref[:, (hk * SWA_GROUP + gq) * HEAD_DIM:(hk * SWA_GROUP + gq + 1) * HEAD_DIM]
                                  for gq in range(SWA_GROUP)], axis=0).astype(bf16)
            s = jnp.concatenate([_dot(q3, ktb[hs, :]), _dot_nt(q3, knb[:, hs])], axis=1) * ATTN_SCALE
            s = jnp.where(mask, s, NEG_BIG)
            sink = jnp.where(grp == 0, sink_ref[hk * SWA_GROUP],
                             jnp.where(grp == 1, sink_ref[hk * SWA_GROUP + 1], sink_ref[hk * SWA_GROUP + 2]))
            m = jnp.maximum(jnp.max(s, axis=-1, keepdims=True), sink)
            e = jnp.exp(s - m)
            denom = jnp.sum(e, axis=-1, keepdims=True) + jnp.exp(sink - m)
            eb = e.astype(bf16)
            o = (_dot_nt(eb[:, :BLOCK], vtb[hs, :]) + _dot(eb[:, BLOCK:], vnb[:, hs])) / denom
            for gq in range(SWA_GROUP):
                hq = hk * SWA_GROUP + gq
                o_ref[b, :, hq * HEAD_DIM:(hq + 1) * HEAD_DIM] = o[gq * tq:(gq + 1) * tq]


def _swa_step(proj, g_qnorm, sinks, cos, sin, cache_kt, cache_vt, k_new, v_new, n_new, bb):
    bsz, tq, _ = proj.shape
    assert SWA_GROUP == 3 and tq & (tq - 1) == 0
    g = jnp.tile(g_qnorm.reshape(1, HEAD_DIM), (1, HEADS_PER_SLAB))
    cache_spec = pl.BlockSpec((bb, KV_WIDTH, BLOCK), lambda b: (b, 0, 0))
    new_spec = pl.BlockSpec((bb, tq, KV_WIDTH), lambda b: (b, 0, 0))
    tab_spec = pl.BlockSpec((tq, LANES), lambda b: (0, 0))
    return pl.pallas_call(
        functools.partial(_swa_step_kernel, n_new=n_new),
        grid=(bsz // bb,),
        in_specs=[pl.BlockSpec(memory_space=pltpu.SMEM),
                  pl.BlockSpec((bb, tq, MAIN_WIDTH), lambda b: (b, 0, 0)),
                  pl.BlockSpec((1, LANES), lambda b: (0, 0)), tab_spec, tab_spec,
                  cache_spec, cache_spec, new_spec, new_spec],
        out_specs=[pl.BlockSpec((bb, tq, MAIN_WIDTH), lambda b: (b, 0, 0)), cache_spec, cache_spec],
        out_shape=[jax.ShapeDtypeStruct((bsz, tq, MAIN_WIDTH), f32),
                   jax.ShapeDtypeStruct(cache_kt.shape, f32), jax.ShapeDtypeStruct(cache_vt.shape, f32)],
        scratch_shapes=[pltpu.VMEM((tq, MAIN_WIDTH), f32)],
        compiler_params=_cp(("parallel",)),
        name="swa_step",
    )(sinks, proj, g, cos, sin, cache_kt, cache_vt, k_new, v_new)


def _out_mlp_kernel(h_ref, main_ref, mem_ref, woa_ref, wob_ref, g_ref, wup_ref, wdn_ref, o_ref, *, tf):
    h1 = (h_ref[...] + _dot(main_ref[...].astype(bf16), woa_ref[...])
          + _dot(mem_ref[...].astype(bf16), wob_ref[...]))
    hn = (h1 * lax.rsqrt(jnp.mean(h1 * h1, axis=-1, keepdims=True) + NORM_EPS) * g_ref[...]).astype(bf16)
    acc = None
    for j in range(D_FF // tf):
        u = _dot(hn, wup_ref[:, j * tf:(j + 1) * tf])
        u = jnp.square(jnp.maximum(u, 0.0)).astype(bf16)
        d = _dot(u, wdn_ref[j * tf:(j + 1) * tf, :])
        acc = d if acc is None else acc + d
    o_ref[...] = h1 + acc


def _out_mlp(h, main, mem_o, w_out_bf16, g_mlp, w_up_bf16, w_down_bf16, tm, tf=1024):
    n = h.shape[0]
    tm = min(tm, n)
    rows = lambda w: pl.BlockSpec((tm, w), lambda i: (i, 0))
    resident = lambda a: pl.BlockSpec(a.shape, lambda i: (0, 0))
    woa, wob = w_out_bf16[:MAIN_WIDTH], w_out_bf16[MAIN_WIDTH:]
    g = g_mlp.reshape(1, D_MODEL)
    return pl.pallas_call(
        functools.partial(_out_mlp_kernel, tf=tf),
        grid=(n // tm,),
        in_specs=[rows(D_MODEL), rows(MAIN_WIDTH), rows(MEM_WIDTH),
                  resident(woa), resident(wob), resident(g), resident(w_up_bf16), resident(w_down_bf16)],
        out_specs=rows(D_MODEL),
        out_shape=jax.ShapeDtypeStruct((n, D_MODEL), f32),
        compiler_params=_cp(("parallel",)),
        name="out_mlp",
    )(h, main, mem_o, woa, wob, g, w_up_bf16, w_down_bf16)


def _rope_tables(pos, reps):
    half = HEAD_DIM // 2
    freqs = jnp.power(ROPE_THETA, -jnp.arange(half, dtype=f32) / half)
    ang = pos.astype(f32)[:, None] * freqs[None, :]
    cos, sin = jnp.cos(ang), jnp.sin(ang)
    cos_h = jnp.concatenate([cos, cos], axis=-1)
    sin_h = jnp.concatenate([-sin, sin], axis=-1)
    return jnp.tile(cos_h, (1, reps)), jnp.tile(sin_h, (1, reps))


def _trunk(x, pos, mem_kt, mem_vt, shift0, wkv0, swa_cache, t_real, W):
    bsz, tp, _ = x.shape
    n = bsz * tp
    prompt = swa_cache is None
    mem_bb = 1 if prompt else 8
    act_dtype = bf16 if prompt else f32
    tm = 512
    flat = lambda a: a.reshape(n, a.shape[-1])
    unflat = lambda a: a.reshape(bsz, tp, a.shape[-1])
    cos4, sin4 = _rope_tables(pos, KV_WIDTH // HEAD_DIM)
    cos2, sin2 = cos4[:, :LANES], sin4[:, :LANES]

    proj = unflat(_norm_matmul(flat(x), W["norm_mix"][0], W["w_in_a"], tm))
    new_shift = proj[:, t_real - 1, :RWKV_COLS]
    prep_w = (W["shift_mu"], W["w_w2"], W["w0"], W["w_a2"], W["a0"], W["w_g2"], W["k_k"], W["k_a"], W["r_k"])
    if prompt:
        seqs = _rwkv_prep(proj, shift0, *prep_w, tt=256, seq_dtype=bf16)
        main, new_wkv = _wkv_chunked(*seqs, W["lnx_w"], W["lnx_b"])
    else:
        steps = jnp.transpose(proj[:, :t_real, :RWKV_COLS], (1, 0, 2)).reshape(1, t_real * bsz, RWKV_COLS)
        seqs = _rwkv_prep(steps, shift0, *prep_w, tt=None, step_major=True)
        seqs = [s.reshape(t_real, bsz, MAIN_WIDTH) for s in seqs]
        main, new_wkv = _wkv_steps(wkv0, seqs, W["lnx_w"], W["lnx_b"], tp)
    mem_o = _mem_attn(proj, RWKV_COLS // MEM_WIDTH, W["mem_qnorm"][0], mem_kt, mem_vt, 0, 256, mem_bb, act_dtype)
    h = _out_mlp(flat(x), flat(main), flat(mem_o), W["w_out"][0], W["norm_mlp"][0], W["w_up"][0],
                 W["w_down"][0], tm)

    kv = _norm_matmul(h, W["kv_norm"], W["w_kv"], tm)
    if prompt:
        k_sh, v_sh = _kv_post(kv, W["swa_knorm"], cos4, sin4, tm, True)
    else:
        k_sh, v_sh = _kv_post(kv, W["swa_knorm"], jnp.tile(cos4, (tm // tp, 1)), jnp.tile(sin4, (tm // tp, 1)),
                              tm, True)
    k_sh, v_sh = unflat(k_sh), unflat(v_sh)

    proj = unflat(_norm_matmul(h, W["norm_mix"][1], W["w_in_b"], tm))
    if prompt:
        main = _swa(proj, W["swa_qnorm"], W["sinks"], cos2, sin2, k_sh, k_sh, v_sh, v_sh, BLOCK,
                    lambda i: jnp.maximum(i - 1, 0), True)
        win = min(WINDOW, tp)
        heads = lambda a: a.reshape(bsz, a.shape[1], SWA_KV_HEADS, HEAD_DIM)
        k_state, v_state = heads(k_sh[:, tp - win:]), heads(v_sh[:, tp - win:])
    else:
        ckt, cvt = swa_cache
        main, kt_new, vt_new = _swa_step(proj, W["swa_qnorm"], W["sinks"], cos2, sin2, ckt, cvt, k_sh, v_sh,
                                         t_real, bb=4)
        untransposed = lambda a: jnp.transpose(a.reshape(bsz, SWA_KV_HEADS, HEAD_DIM, -1), (0, 3, 1, 2))
        k_state, v_state = untransposed(kt_new), untransposed(vt_new)
    mem_o = _mem_attn(proj, MAIN_WIDTH // MEM_WIDTH, W["mem_qnorm"][1], mem_kt, mem_vt, 1, 256, mem_bb, act_dtype)
    y = _out_mlp(h, flat(main), flat(mem_o), W["w_out"][1], W["norm_mlp"][1], W["w_up"][1], W["w_down"][1], tm)
    return unflat(y)[:, :t_real], new_shift[None], new_wkv[None], k_state, v_state


def kernel(x_prompt, x_sample, state_rwkv_shift, state_rwkv_wkv, cache_swa_k, cache_swa_v, cache_mem_k,
           cache_mem_v, mem_prompt, norm_mix, norm_mlp, w_out, w_up, w_down, mem_norm, w_mem_kv, mem_qnorm,
           mem_knorm, w_in_a, shift_mu, w_w2, w0, w_a2, a0, w_g2, k_k, k_a, r_k, lnx_w, lnx_b, w_in_b,
           swa_qnorm, sinks, kv_norm, w_kv, swa_knorm):
    W = dict(norm_mix=norm_mix, norm_mlp=norm_mlp, w_out=w_out.astype(bf16), w_up=w_up.astype(bf16),
             w_down=w_down.astype(bf16), mem_qnorm=mem_qnorm, w_in_a=w_in_a[0].astype(bf16),
             shift_mu=shift_mu[0], w_w2=w_w2[0], w0=w0[0], w_a2=w_a2[0], a0=a0[0], w_g2=w_g2[0],
             k_k=k_k[0], k_a=k_a[0], r_k=r_k[0].reshape(-1), lnx_w=lnx_w[0], lnx_b=lnx_b[0],
             w_in_b=w_in_b[0].astype(bf16), swa_qnorm=swa_qnorm[0], sinks=sinks[0], kv_norm=kv_norm,
             w_kv=w_kv.astype(bf16), swa_knorm=swa_knorm)
    bp, tp, _ = x_prompt.shape
    bs, ts, _ = x_sample.shape
    depth = norm_mix.shape[0]

    mem_flat = mem_prompt.reshape(bp * N_MEM, D_MODEL)
    ones = jnp.ones((N_MEM, KV_WIDTH), f32)
    p_mem_k, p_mem_v = [], []
    for l in range(depth):
        kv = _norm_matmul(mem_flat, mem_norm[l], w_mem_kv[l].astype(bf16), 512)
        mk, mv = _kv_post(kv, mem_knorm[l], ones, ones, N_MEM, False)
        p_mem_k.append(mk.reshape(bp, N_MEM, MEM_WIDTH))
        p_mem_v.append(mv.reshape(bp, N_MEM, MEM_WIDTH))
    transposed = lambda a: jnp.swapaxes(jnp.stack(a), 2, 3)
    y_p, p_shift, p_wkv, p_k, p_v = _trunk(
        x_prompt, jnp.arange(tp), transposed(p_mem_k), transposed(p_mem_v), jnp.zeros((bp, RWKV_COLS), f32),
        None, None, tp, W)
    mem_heads = lambda a: jnp.stack(a).reshape(depth, bp, N_MEM, MEM_HEADS, HEAD_DIM)
    cache_t = lambda a: jnp.transpose(a, (0, 1, 3, 4, 2)).reshape(depth, bs, MEM_WIDTH, N_MEM)

    tpad = -(-ts // SUBLANES) * SUBLANES
    x_s = jnp.pad(x_sample, ((0, 0), (0, tpad - ts), (0, 0)))
    win = cache_swa_k.shape[1]
    assert win == WINDOW
    swa_t = lambda a: jnp.transpose(a, (0, 2, 3, 1)).reshape(bs, KV_WIDTH, win)
    y_s, s_shift, s_wkv, s_k, s_v = _trunk(
        x_s, PAST_LEN + jnp.arange(tpad), cache_t(cache_mem_k), cache_t(cache_mem_v),
        state_rwkv_shift[0], state_rwkv_wkv[0],
        (swa_t(cache_swa_k), swa_t(cache_swa_v)), ts, W)

    return (y_p, y_s, p_shift, p_wkv, p_k, p_v, mem_heads(p_mem_k), mem_heads(p_mem_v),
            s_shift, s_wkv, s_k, s_v)
```

```python
import functools

import jax
import jax.numpy as jnp
from jax import lax
from jax.experimental import pallas as pl
from jax.experimental.pallas import tpu as pltpu

f32 = jnp.float32
bf16 = jnp.bfloat16

D_MODEL = 1024
HEAD_DIM = 64
MEM_HEADS = 4
MEM_WIDTH = MEM_HEADS * HEAD_DIM
MAIN_WIDTH = D_MODEL - MEM_WIDTH
RWKV_HEADS = MAIN_WIDTH // HEAD_DIM
DECAY_LORA = 64
AAA_LORA = 64
GATE_LORA = 128
RWKV_COLS = 3 * MAIN_WIDTH + DECAY_LORA + AAA_LORA + GATE_LORA
A_IN_COLS = RWKV_COLS + MEM_WIDTH
SWA_Q_HEADS = MAIN_WIDTH // HEAD_DIM
SWA_KV_HEADS = 4
SWA_GROUP = SWA_Q_HEADS // SWA_KV_HEADS
KV_WIDTH = SWA_KV_HEADS * HEAD_DIM
WINDOW = 128
BLOCK = 128
N_MEM = 256
D_FF = 4 * D_MODEL
PAST_LEN = 16384
ROPE_THETA = 10000.0
NORM_EPS = 1e-6
LNX_EPS = 6.4e-4
L2_EPS = 1e-12
ATTN_SCALE = HEAD_DIM ** -0.5

LANES = 128
SUBLANES = 8
HEADS_PER_SLAB = LANES // HEAD_DIM
N_SLABS = MAIN_WIDTH // LANES
CHUNK = 64
VMEM_LIMIT = 56 * 1024 * 1024
NEG_BIG = -1e30
ROW_TILE = 512
PREP_TILE = 256
MEM_Q_TILE = 256
STEP_MEM_SEQS = 8
STEP_SWA_SEQS = 4


def _cp(sem, vmem=VMEM_LIMIT):
    return pltpu.CompilerParams(dimension_semantics=sem, vmem_limit_bytes=vmem)


def _dot(a, b):
    return jnp.dot(a, b, preferred_element_type=f32)


def _dot_nt(a, b):
    return lax.dot_general(a, b, (((1,), (1,)), ((), ())), preferred_element_type=f32)


def _dot_tn(a, b):
    return lax.dot_general(a, b, (((0,), (0,)), ((), ())), preferred_element_type=f32)


def _iota(shape, dim):
    return lax.broadcasted_iota(jnp.int32, shape, dim)


def _group_ones():
    r = _iota((LANES, LANES), 0)
    c = _iota((LANES, LANES), 1)
    return jnp.where((r < HEAD_DIM) == (c < HEAD_DIM), 1.0, 0.0).astype(f32)


def _head_sum(x, gmat):
    w = x.shape[-1]
    gb = gmat.astype(bf16)
    hi = x.astype(bf16)
    lo = (x - hi.astype(f32)).astype(bf16)
    parts = [_dot(hi[:, j * LANES:(j + 1) * LANES], gb) + _dot(lo[:, j * LANES:(j + 1) * LANES], gb)
             for j in range(w // LANES)]
    return parts[0] if len(parts) == 1 else jnp.concatenate(parts, axis=-1)


def _cumsum_rows(x):
    rows = x.shape[0]
    row = _iota(x.shape, 0)
    step = 1
    while step < rows:
        x = x + jnp.where(row >= step, pltpu.roll(x, step, 0), 0.0)
        step *= 2
    return x


def _sigmoid(x):
    return 1.0 / (1.0 + jnp.exp(-x))


def _softplus(x):
    return jnp.maximum(x, 0.0) + jnp.log1p(jnp.exp(-jnp.abs(x)))


def _rot_half(x):
    lane = _iota(x.shape, 1)
    first = (lane & (HEAD_DIM - 1)) < (HEAD_DIM // 2)
    return jnp.where(first, pltpu.roll(x, LANES - HEAD_DIM // 2, 1), pltpu.roll(x, HEAD_DIM // 2, 1))


def _norm_matmul_kernel(x_ref, g_ref, w_ref, o_ref):
    x = x_ref[...]
    xn = x * lax.rsqrt(jnp.mean(x * x, axis=-1, keepdims=True) + NORM_EPS) * g_ref[...]
    o_ref[...] = _dot(xn.astype(bf16), w_ref[...])


def _norm_matmul(x, g, w_bf16, tm):
    n, d = x.shape
    m = w_bf16.shape[1]
    tm = min(tm, n)
    return pl.pallas_call(
        _norm_matmul_kernel,
        grid=(n // tm,),
        in_specs=[pl.BlockSpec((tm, d), lambda i: (i, 0)),
                  pl.BlockSpec((1, d), lambda i: (0, 0)),
                  pl.BlockSpec((d, m), lambda i: (0, 0))],
        out_specs=pl.BlockSpec((tm, m), lambda i: (i, 0)),
        out_shape=jax.ShapeDtypeStruct((n, m), f32),
        compiler_params=_cp(("parallel",)),
        name="norm_matmul",
    )(x, g.reshape(1, d), w_bf16)


def _head_norm_rope(x, g, cos, sin, gmat):
    xn = x * lax.rsqrt(_head_sum(x * x, gmat) * (1.0 / HEAD_DIM) + NORM_EPS) * g
    slabs = [xn[:, j * LANES:(j + 1) * LANES] for j in range(x.shape[-1] // LANES)]
    return [xs * cos + _rot_half(xs) * sin for xs in slabs]


def _kv_post_kernel(kv_ref, g_ref, k_ref, v_ref):
    gmat = _group_ones()
    x = kv_ref[:, :KV_WIDTH]
    k_ref[...] = x * lax.rsqrt(_head_sum(x * x, gmat) * (1.0 / HEAD_DIM) + NORM_EPS) * g_ref[...]
    v_ref[...] = kv_ref[:, KV_WIDTH:]


def _kv_post(kv, g_head, tm):
    n = kv.shape[0]
    tm = min(tm, n)
    g = jnp.tile(g_head.reshape(1, HEAD_DIM), (1, KV_WIDTH // HEAD_DIM))
    return pl.pallas_call(
        _kv_post_kernel,
        grid=(n // tm,),
        in_specs=[pl.BlockSpec((tm, 2 * KV_WIDTH), lambda i: (i, 0)),
                  pl.BlockSpec((1, KV_WIDTH), lambda i: (0, 0))],
        out_specs=[pl.BlockSpec((tm, KV_WIDTH), lambda i: (i, 0)),
                   pl.BlockSpec((tm, KV_WIDTH), lambda i: (i, 0))],
        out_shape=[jax.ShapeDtypeStruct((n, KV_WIDTH), f32)] * 2,
        compiler_params=_cp(("parallel",)),
        name="kv_post",
    )(kv, g)


def _time_mix_inputs(pf, prev, consts, outs):
    mu_ref, ww2_ref, w0_ref, wa2_ref, a0_ref, wg2_ref, kk_ref, ka_ref, rk_ref = consts
    r_o, lw_o, k_o, v_o, a_o, b_o, bonus_o, g_o = outs
    gmat = _group_ones()
    ps = pf + (prev - pf) * mu_ref[...]
    i1, i2, i3 = MAIN_WIDTH, 2 * MAIN_WIDTH, 3 * MAIN_WIDTH
    i4, i5 = i3 + DECAY_LORA, i3 + DECAY_LORA + AAA_LORA
    r, k, v = ps[:, :i1], ps[:, i1:i2], ps[:, i2:i3]
    wd, ad, gd = ps[:, i3:i4], ps[:, i4:i5], ps[:, i5:]
    lora = lambda x, w_ref: _dot(x.astype(bf16), w_ref[...].astype(bf16))
    w_log = -_softplus(-(w0_ref[...] + lora(jnp.tanh(wd), ww2_ref))) - 0.5
    lw = -jnp.exp(w_log)
    a_sig = _sigmoid(a0_ref[...] + lora(ad, wa2_ref))
    g = lora(_sigmoid(gd), wg2_ref)
    kk = k * kk_ref[...]
    kk = kk / jnp.maximum(jnp.sqrt(_head_sum(kk * kk, gmat)), L2_EPS)
    k2 = k * (1.0 + (a_sig - 1.0) * ka_ref[...])
    r_o[0] = r.astype(r_o.dtype)
    lw_o[0] = lw
    k_o[0] = k2.astype(k_o.dtype)
    v_o[0] = v.astype(v_o.dtype)
    a_o[0] = (-kk).astype(a_o.dtype)
    b_o[0] = (kk * a_sig).astype(b_o.dtype)
    bonus_o[0] = (_head_sum(r * k2 * rk_ref[...], gmat) * v).astype(bonus_o.dtype)
    g_o[0] = g.astype(g_o.dtype)


def _rwkv_prep_kernel(p_ref, shift_ref, *refs):
    pf = p_ref[0]
    prev = jnp.concatenate([shift_ref[0], pf[:pf.shape[0] - shift_ref.shape[1]]], axis=0)
    _time_mix_inputs(pf, prev, refs[:9], refs[9:])


def _proj_prep_kernel(x_ref, gx_ref, w_ref, shift_ref, *refs):
    consts, outs, (q_o, last_o, carry_ref) = refs[:9], refs[9:17], refs[17:]
    i = pl.program_id(1)
    x = x_ref[0]
    xn = x * lax.rsqrt(jnp.mean(x * x, axis=-1, keepdims=True) + NORM_EPS) * gx_ref[...]
    proj = _dot(xn.astype(bf16), w_ref[...])
    q_o[0] = proj[:, RWKV_COLS:]
    pf = proj[:, :RWKV_COLS]
    first_prev = jnp.where(i == 0, shift_ref[0], carry_ref[...])
    prev = jnp.where(_iota(pf.shape, 0) == 0, first_prev, pltpu.roll(pf, 1, 0))
    last = pf[pf.shape[0] - 1:, :]
    carry_ref[...] = last
    last_o[0] = last
    _time_mix_inputs(pf, prev, consts, outs)


def _proj_prep(x, g_norm, w_in_bf16, shift_prev, prep_w, tt, seq_dtype):
    bsz, t, _ = x.shape
    row = lambda a: a.reshape(1, -1)
    full = lambda a: pl.BlockSpec(a.shape, lambda b, i: (0,) * a.ndim)
    mu, w_w2, w0, w_a2, a0, w_g2, k_k, k_a, r_k = prep_w
    consts = [row(mu), w_w2, row(w0), w_a2, row(a0), w_g2, row(k_k), row(k_a), row(r_k)]
    tile = lambda w: pl.BlockSpec((1, tt, w), lambda b, i: (b, i, 0))
    per_seq = pl.BlockSpec((1, 1, RWKV_COLS), lambda b, i: (b, 0, 0))
    res = pl.pallas_call(
        _proj_prep_kernel,
        grid=(bsz, t // tt),
        in_specs=[tile(D_MODEL), full(row(g_norm)), full(w_in_bf16), per_seq] + [full(c) for c in consts],
        out_specs=[tile(MAIN_WIDTH)] * 8 + [tile(MEM_WIDTH), per_seq],
        out_shape=[jax.ShapeDtypeStruct((bsz, t, MAIN_WIDTH), f32 if j == 1 else seq_dtype) for j in range(8)]
                  + [jax.ShapeDtypeStruct((bsz, t, MEM_WIDTH), f32), jax.ShapeDtypeStruct((bsz, 1, RWKV_COLS), f32)],
        scratch_shapes=[pltpu.VMEM((1, RWKV_COLS), f32)],
        compiler_params=_cp(("parallel", "arbitrary")),
        name="proj_prep",
    )(x, row(g_norm), w_in_bf16, shift_prev.reshape(bsz, 1, RWKV_COLS), *consts)
    return res[:8], res[8], res[9].reshape(bsz, RWKV_COLS)


def _rwkv_prep(steps, shift_prev, prep_w):
    _, t, _ = steps.shape
    row = lambda a: a.reshape(1, -1)
    full = lambda a: pl.BlockSpec(a.shape, lambda i: (0,) * a.ndim)
    mu, w_w2, w0, w_a2, a0, w_g2, k_k, k_a, r_k = prep_w
    consts = [row(mu), w_w2, row(w0), w_a2, row(a0), w_g2, row(k_k), row(k_a), row(r_k)]
    shift_rows = shift_prev[None]
    return pl.pallas_call(
        _rwkv_prep_kernel,
        grid=(1,),
        in_specs=[full(steps), full(shift_rows)] + [full(c) for c in consts],
        out_specs=[pl.BlockSpec((1, t, MAIN_WIDTH), lambda i: (0, 0, 0))] * 8,
        out_shape=[jax.ShapeDtypeStruct((1, t, MAIN_WIDTH), f32)] * 8,
        compiler_params=_cp(("arbitrary",)),
        name="rwkv_prep",
    )(steps, shift_rows, *consts)


def _wkv_chunk_kernel(r_ref, lw_ref, k_ref, v_ref, a_ref, b_ref, bonus_ref, g_ref, lnw_ref, lnb_ref,
                      o_ref, s_ref, h_ref, qt_ref, yv_ref, pp_ref, hv_ref, tc_ref):
    i = pl.program_id(2)
    c = CHUNK
    nc = r_ref.shape[1] // c
    n = range(nc)

    @pl.when(i == 0)
    def _():
        h_ref[...] = jnp.zeros_like(h_ref)
        qt_ref[...] = jnp.zeros_like(qt_ref)
        yv_ref[...] = jnp.zeros_like(yv_ref)
        pp_ref[...] = jnp.zeros_like(pp_ref)
        hv_ref[...] = jnp.zeros_like(hv_ref)
        tc_ref[...] = jnp.zeros_like(tc_ref)

    gmat = _group_ones()
    bd = gmat > 0.5
    row = _iota((LANES, LANES), 0)
    col = _iota((LANES, LANES), 1)
    t_row = row & (c - 1)
    s_col = col & (c - 1)
    m_strict = s_col < t_row
    m_incl = s_col <= t_row
    top = row < c
    lane_lo = _iota((1, LANES), 1) < HEAD_DIM
    eye = row == col
    zeros = jnp.zeros((c, LANES), f32)
    mm = lambda x: x.astype(bf16)
    gb = gmat.astype(bf16)
    sls = [slice(j * c, (j + 1) * c) for j in n]

    state = {"h": h_ref[...], "ys": []}

    def state_step(j):
        hb = mm(state["h"])
        state["ys"].append(_dot(qt_ref[j], hb) + yv_ref[j])
        state["h"] = state["h"] * tc_ref[j] + _dot(pp_ref[j], hb) + hv_ref[j]

    lws = [lw_ref[0, sl, :] for sl in sls]
    cums = [_cumsum_rows(lw) for lw in lws]
    state_step(0)
    vs = [v_ref[0, sl, :].astype(f32) for sl in sls]
    tots = [cum[c - 1:c, :] for cum in cums]
    ats = [a_ref[0, sl, :].astype(f32) * jnp.exp(cum - lw) for sl, cum, lw in zip(sls, cums, lws)]
    rts = [r_ref[0, sl, :].astype(f32) * jnp.exp(cum) for sl, cum in zip(sls, cums)]
    bks, bkts = [], []
    for sl, cum, tot in zip(sls, cums, tots):
        b, k = b_ref[0, sl, :].astype(f32), k_ref[0, sl, :].astype(f32)
        e_neg, e_rem = jnp.exp(-cum), jnp.exp(tot - cum)
        bks.append(mm(jnp.concatenate([b * e_neg, k * e_neg], axis=0)))
        bkts.append(mm(jnp.concatenate([b * e_rem, k * e_rem], axis=0)))
    lhss = [mm(jnp.concatenate([jnp.where(lane_lo, at, 0.0), jnp.where(lane_lo, 0.0, at),
                                jnp.where(lane_lo, rt, 0.0), jnp.where(lane_lo, 0.0, rt)], axis=0))
            for at, rt in zip(ats, rts)]
    gms = [_dot_nt(lhss[j], bks[j]) for j in n]
    state_step(1)
    aaks = [jnp.where(m_strict, gm[:LANES], 0.0) for gm in gms]
    arks = [mm(jnp.where(m_incl, gm[LANES:], 0.0)) for gm in gms]
    aak_sws = [pltpu.roll(aak, HEAD_DIM, 1) for aak in aaks]
    ms = [jnp.where(bd, jnp.where(top, aaks[j], aak_sws[j]), 0.0) for j in n]
    aks = [mm(jnp.where(bd, jnp.where(top, aak_sws[j], aaks[j]), 0.0)) for j in n]
    vvs = [mm(jnp.concatenate([pltpu.roll(v, HEAD_DIM, 1)] * 2, axis=0)) for v in vs]
    akvs = [_dot(aks[j], vvs[j]) for j in n]
    state_step(2)
    xs = [jnp.where(bd, jnp.concatenate([ats[j], ats[j]], axis=0), akvs[j]) for j in n]
    means = None
    for lvl in range(6):
        if lvl < 5:
            zs = [_dot(mm(ms[j]), mm(jnp.concatenate([xs[j], ms[j]], axis=1))) for j in n]
            xs = [xs[j] + zs[j][:, :LANES] for j in n]
            ms = [zs[j][:, LANES:] for j in n]
            state_step(3 + lvl)
        else:
            zs = [_dot(mm(ms[j]), mm(xs[j])) for j in n]
            xs = [xs[j] + zs[j] for j in n]
            means = [_dot(mm(y), gb) * (1.0 / HEAD_DIM) for y in state["ys"]]
    assert len(state["ys"]) == nc == 8
    h_ref[...] = state["h"]
    rhss = []
    for j in n:
        x = xs[j]
        ta = jnp.where(lane_lo, x[:c], x[c:])
        uv = pltpu.roll(jnp.where(lane_lo, x[c:], x[:c]), HEAD_DIM, 1)
        rhss.append(mm(jnp.concatenate([jnp.concatenate([ta, uv], axis=1),
                                        jnp.concatenate([zeros, vs[j]], axis=1)], axis=0)))
    z2s = [_dot(arks[j], rhss[j]) for j in n]
    pzs = [_dot_tn(bkts[j], rhss[j]) for j in n]
    ycs = [state["ys"][j] - means[j] for j in n]
    vrs = [_dot(mm(yc * yc), gb) * (1.0 / HEAD_DIM) for yc in ycs]
    for j in n:
        yn = ycs[j] * lax.rsqrt(vrs[j] + LNX_EPS) * lnw_ref[...] + lnb_ref[...]
        o_ref[0, sls[j], :] = ((yn + bonus_ref[0, sls[j], :].astype(f32))
                               * g_ref[0, sls[j], :].astype(f32)).astype(o_ref.dtype)
    for j in n:
        qt_ref[j] = mm(rts[j] + jnp.where(lane_lo, z2s[j][:c, :LANES], z2s[j][c:, :LANES]))
        yv_ref[j] = jnp.where(lane_lo, z2s[j][:c, LANES:], z2s[j][c:, LANES:])
        pp_ref[j] = mm(jnp.where(bd, pzs[j][:, :LANES], 0.0))
        hv_ref[j] = jnp.where(bd, pzs[j][:, LANES:], 0.0)
        tc_ref[j] = jnp.sum(jnp.where(eye, jnp.exp(tots[j]), 0.0), axis=1, keepdims=True)

    @pl.when(i == pl.num_programs(2) - 1)
    def _():
        s_ref[0, 0] = h_ref[...].T


def _wkv_chunked(r, lw, k, v, a, b, bonus, g, lnx_w, lnx_b):
    bsz, t, _ = r.shape
    nc = 8
    tt = nc * CHUNK
    nt = t // tt
    assert t % tt == 0
    cur = pl.BlockSpec((1, tt, LANES), lambda bb, hp, i: (bb, jnp.minimum(i, nt - 1), hp))
    pend = pl.BlockSpec((1, tt, LANES), lambda bb, hp, i: (bb, jnp.maximum(i - 1, 0), hp))
    vec = pl.BlockSpec((1, LANES), lambda bb, hp, i: (0, hp))
    out, st = pl.pallas_call(
        _wkv_chunk_kernel,
        grid=(bsz, N_SLABS, nt + 1),
        in_specs=[cur] * 6 + [pend, pend, vec, vec],
        out_specs=[pend, pl.BlockSpec((1, 1, LANES, LANES), lambda bb, hp, i: (bb, hp, 0, 0))],
        out_shape=[jax.ShapeDtypeStruct((bsz, t, MAIN_WIDTH), bf16),
                   jax.ShapeDtypeStruct((bsz, N_SLABS, LANES, LANES), f32)],
        scratch_shapes=[pltpu.VMEM((LANES, LANES), f32),
                        pltpu.VMEM((nc, CHUNK, LANES), bf16), pltpu.VMEM((nc, CHUNK, LANES), f32),
                        pltpu.VMEM((nc, LANES, LANES), bf16), pltpu.VMEM((nc, LANES, LANES), f32),
                        pltpu.VMEM((nc, LANES, 1), f32)],
        compiler_params=_cp(("parallel", "parallel", "arbitrary")),
        name="wkv_chunked",
    )(r, lw, k, v, a, b, bonus, g, lnx_w.reshape(1, -1), lnx_b.reshape(1, -1))
    st = jnp.stack([st[:, :, :HEAD_DIM, :HEAD_DIM], st[:, :, HEAD_DIM:, HEAD_DIM:]], axis=2)
    return out, st.reshape(bsz, RWKV_HEADS, HEAD_DIM, HEAD_DIM)


def _wkv_steps_kernel(s_ref, seq_ref, lnw_ref, lnb_ref, o_ref, so_ref, w_ref, y_ref, *, n_steps):
    i_r, i_lw, i_k, i_v, i_a, i_b, i_bonus, i_g = range(8)
    w_ref[...] = jnp.exp(seq_ref[i_lw])
    sub = _iota((SUBLANES, LANES), 0)

    def group(vg, carry):
        v0 = pl.multiple_of(vg * SUBLANES, SUBLANES)
        v_rows = [seq_ref[i_v, t, pl.ds(v0, SUBLANES), :] for t in range(n_steps)]
        ys = [jnp.zeros((SUBLANES, LANES), f32) for _ in range(n_steps)]
        for j in range(SUBLANES):
            sv = s_ref[0, v0 + j]
            for t in range(n_steps):
                sa = jnp.sum(sv * seq_ref[i_a, t], axis=0, keepdims=True)
                sv = sv * w_ref[t] + sa * seq_ref[i_b, t] + v_rows[t][j:j + 1, :] * seq_ref[i_k, t]
                y = jnp.sum(sv * seq_ref[i_r, t], axis=0, keepdims=True)
                ys[t] = jnp.where(sub == j, y, ys[t])
            so_ref[0, v0 + j] = sv
        for t in range(n_steps):
            y_ref[t, pl.ds(v0, SUBLANES), :] = ys[t]
        return carry

    lax.fori_loop(0, HEAD_DIM // SUBLANES, group, 0)
    y = y_ref[...]
    mean = jnp.mean(y, axis=1, keepdims=True)
    yc = y - mean
    var = jnp.mean(yc * yc, axis=1, keepdims=True)
    yn = yc * lax.rsqrt(var + LNX_EPS) * lnw_ref[...] + lnb_ref[...]
    o_ref[...] = (yn + seq_ref[i_bonus]) * seq_ref[i_g]


def _wkv_steps(state, seqs, lnx_w, lnx_b, tp):
    n_steps, bsz, _ = seqs[0].shape
    assert bsz % LANES == 0
    state_t = jnp.transpose(state, (1, 2, 3, 0))
    seq_t = jnp.transpose(jnp.stack(seqs), (0, 1, 3, 2))
    lanes = lambda x: jnp.broadcast_to(x.reshape(MAIN_WIDTH, 1), (MAIN_WIDTH, LANES))
    sspec = pl.BlockSpec((1, HEAD_DIM, HEAD_DIM, LANES), lambda h, bi: (h, 0, 0, bi))
    vec = pl.BlockSpec((HEAD_DIM, LANES), lambda h, bi: (h, 0))
    ospec = pl.BlockSpec((n_steps, HEAD_DIM, LANES), lambda h, bi: (0, h, bi))
    out, new_state = pl.pallas_call(
        functools.partial(_wkv_steps_kernel, n_steps=n_steps),
        grid=(RWKV_HEADS, bsz // LANES),
        in_specs=[sspec, pl.BlockSpec((8, n_steps, HEAD_DIM, LANES), lambda h, bi: (0, 0, h, bi)), vec, vec],
        out_specs=[ospec, sspec],
        out_shape=[jax.ShapeDtypeStruct((n_steps, MAIN_WIDTH, bsz), f32),
                   jax.ShapeDtypeStruct(state_t.shape, f32)],
        scratch_shapes=[pltpu.VMEM((n_steps, HEAD_DIM, LANES), f32), pltpu.VMEM((n_steps, HEAD_DIM, LANES), f32)],
        compiler_params=_cp(("parallel", "parallel")),
        name="wkv_steps",
    )(state_t, seq_t, lanes(lnx_w), lanes(lnx_b))
    out = jnp.pad(jnp.transpose(out, (2, 0, 1)), ((0, 0), (0, tp - n_steps), (0, 0)))
    return out, jnp.transpose(new_state, (3, 0, 1, 2))


def _mem_attn_kernel(q_ref, g_ref, kt_ref, vt_ref, o_ref):
    gmat = _group_ones()
    bb, tq, _ = q_ref.shape
    head = _iota((1, MEM_WIDTH), 1) // HEAD_DIM
    for b in range(bb):
        q = q_ref[b]
        qn = q * lax.rsqrt(_head_sum(q * q, gmat) * (1.0 / HEAD_DIM) + NORM_EPS) * g_ref[...]
        qs = jnp.concatenate([jnp.where(head == h, qn, 0.0) for h in range(MEM_HEADS)], axis=0)
        s = _dot(qs.astype(bf16), kt_ref[0, b].astype(bf16)) * ATTN_SCALE
        e = jnp.exp(s - jnp.max(s, axis=-1, keepdims=True))
        o4 = _dot_nt(e.astype(bf16), vt_ref[0, b].astype(bf16)) / jnp.sum(e, axis=-1, keepdims=True)
        o = jnp.zeros((tq, MEM_WIDTH), f32)
        for h in range(MEM_HEADS):
            o = o + jnp.where(head == h, o4[h * tq:(h + 1) * tq], 0.0)
        o_ref[b] = o.astype(o_ref.dtype)


def _mem_attn(proj, col_block, g_qnorm, mem_kt, mem_vt, layer, tq, bb, out_dtype):
    bsz, t, _ = proj.shape
    tq = min(tq, t)
    g = jnp.tile(g_qnorm.reshape(1, HEAD_DIM), (1, MEM_HEADS))
    kv_spec = pl.BlockSpec((1, bb, MEM_WIDTH, N_MEM), lambda b, i: (layer, b, 0, 0))
    return pl.pallas_call(
        _mem_attn_kernel,
        grid=(bsz // bb, t // tq),
        in_specs=[pl.BlockSpec((bb, tq, MEM_WIDTH), lambda b, i: (b, i, col_block)),
                  pl.BlockSpec((1, MEM_WIDTH), lambda b, i: (0, 0)),
                  kv_spec, kv_spec],
        out_specs=pl.BlockSpec((bb, tq, MEM_WIDTH), lambda b, i: (b, i, 0)),
        out_shape=jax.ShapeDtypeStruct((bsz, t, MEM_WIDTH), out_dtype),
        compiler_params=_cp(("parallel", "parallel")),
        name="mem_attn",
    )(proj, g, mem_kt, mem_vt)


def _swa_kernel(sink_ref, q_ref, g_ref, cos_ref, sin_ref, kp_ref, kc_ref, vp_ref, vc_ref, o_ref, qn_ref):
    i = pl.program_id(1)
    gmat = _group_ones()
    tq = q_ref.shape[1]
    for j in range(N_SLABS):
        sl = slice(j * LANES, (j + 1) * LANES)
        x = q_ref[0, :, sl]
        xn = x * lax.rsqrt(_head_sum(x * x, gmat) * (1.0 / HEAD_DIM) + NORM_EPS) * g_ref[...]
        qn_ref[:, sl] = xn * cos_ref[...] + _rot_half(xn) * sin_ref[...]
    keys = jnp.concatenate([kp_ref[0], kc_ref[0]], axis=0).astype(bf16)
    vals = jnp.concatenate([vp_ref[0], vc_ref[0]], axis=0).astype(bf16)
    rows = SWA_GROUP * tq
    row = _iota((rows, 2 * BLOCK), 0) & (tq - 1)
    col = _iota((rows, 2 * BLOCK), 1)
    mask = (col > row) & (col <= row + WINDOW) & ((i > 0) | (col >= BLOCK))
    grp = _iota((rows, 1), 0) // tq
    kv = range(SWA_KV_HEADS)
    hs = [slice(hk * HEAD_DIM, (hk + 1) * HEAD_DIM) for hk in kv]
    q3s = [jnp.concatenate([qn_ref[:, (hk * SWA_GROUP + gq) * HEAD_DIM:(hk * SWA_GROUP + gq + 1) * HEAD_DIM]
                            for gq in range(SWA_GROUP)], axis=0).astype(bf16) for hk in kv]
    ss = [jnp.where(mask, _dot_nt(q3s[hk], keys[:, hs[hk]]) * ATTN_SCALE, NEG_BIG) for hk in kv]
    sinks = [jnp.where(grp == 0, sink_ref[hk * SWA_GROUP],
                       jnp.where(grp == 1, sink_ref[hk * SWA_GROUP + 1], sink_ref[hk * SWA_GROUP + 2])) for hk in kv]
    ms = [jnp.maximum(jnp.max(ss[hk], axis=-1, keepdims=True), sinks[hk]) for hk in kv]
    es = [jnp.exp(ss[hk] - ms[hk]) for hk in kv]
    denoms = [jnp.sum(es[hk], axis=-1, keepdims=True) + jnp.exp(sinks[hk] - ms[hk]) for hk in kv]
    outs = [_dot(es[hk].astype(bf16), vals[:, hs[hk]]) / denoms[hk] for hk in kv]
    for hk in kv:
        for gq in range(SWA_GROUP):
            hq = hk * SWA_GROUP + gq
            o_ref[0, :, hq * HEAD_DIM:(hq + 1) * HEAD_DIM] = outs[hk][gq * tq:(gq + 1) * tq].astype(o_ref.dtype)


def _swa(proj, g_qnorm, sinks, cos, sin, k, v):
    bsz, t, _ = proj.shape
    assert SWA_GROUP == 3
    tq = BLOCK
    g = jnp.tile(g_qnorm.reshape(1, HEAD_DIM), (1, HEADS_PER_SLAB))
    prev = pl.BlockSpec((1, BLOCK, KV_WIDTH), lambda b, i: (b, jnp.maximum(i - 1, 0), 0))
    cur = pl.BlockSpec((1, BLOCK, KV_WIDTH), lambda b, i: (b, i, 0))
    return pl.pallas_call(
        _swa_kernel,
        grid=(bsz, t // tq),
        in_specs=[pl.BlockSpec(memory_space=pltpu.SMEM),
                  pl.BlockSpec((1, tq, MAIN_WIDTH), lambda b, i: (b, i, 0)),
                  pl.BlockSpec((1, LANES), lambda b, i: (0, 0)),
                  pl.BlockSpec((tq, LANES), lambda b, i: (i, 0)),
                  pl.BlockSpec((tq, LANES), lambda b, i: (i, 0)),
                  prev, cur, prev, cur],
        out_specs=pl.BlockSpec((1, tq, MAIN_WIDTH), lambda b, i: (b, i, 0)),
        out_shape=jax.ShapeDtypeStruct((bsz, t, MAIN_WIDTH), bf16),
        scratch_shapes=[pltpu.VMEM((tq, MAIN_WIDTH), f32)],
        compiler_params=_cp(("parallel", "parallel")),
        name="swa_attn",
    )(sinks, proj, g, cos, sin, k, k, v, v)


def _swa_step_kernel(sink_ref, q_ref, g_ref, cos_ref, sin_ref, kt_ref, vt_ref, kn_ref, vn_ref,
                     o_ref, kto_ref, vto_ref, qn_ref, *, n_new):
    gmat = _group_ones()
    bb, tq, _ = q_ref.shape
    row = _iota((SWA_GROUP * tq, 2 * BLOCK), 0) & (tq - 1)
    col = _iota((SWA_GROUP * tq, 2 * BLOCK), 1)
    mask = (col > row) & (col <= row + WINDOW)
    grp = _iota((SWA_GROUP * tq, 1), 0) // tq
    lane = _iota((KV_WIDTH, BLOCK), 1)
    pad = jnp.zeros((BLOCK - tq, KV_WIDTH), f32)
    for b in range(bb):
        for j in range(N_SLABS):
            sl = slice(j * LANES, (j + 1) * LANES)
            x = q_ref[b, :, sl]
            xn = x * lax.rsqrt(_head_sum(x * x, gmat) * (1.0 / HEAD_DIM) + NORM_EPS) * g_ref[...]
            qn_ref[:, sl] = xn * cos_ref[...] + _rot_half(xn) * sin_ref[...]
        kt, vt = kt_ref[b], vt_ref[b]
        kn = jnp.concatenate([kn_ref[b], pad], axis=0)
        vn = jnp.concatenate([vn_ref[b], pad], axis=0)
        keep = lane < BLOCK - n_new
        kto_ref[b] = jnp.where(keep, pltpu.roll(kt, BLOCK - n_new, 1), pltpu.roll(kn.T, BLOCK - n_new, 1))
        vto_ref[b] = jnp.where(keep, pltpu.roll(vt, BLOCK - n_new, 1), pltpu.roll(vn.T, BLOCK - n_new, 1))
        ktb, vtb, knb, vnb = kt.astype(bf16), vt.astype(bf16), kn.astype(bf16), vn.astype(bf16)
        for hk in range(SWA_KV_HEADS):
            hs = slice(hk * HEAD_DIM, (hk + 1) * HEAD_DIM)
            q3 = jnp.concatenate([qn_ref[:, (hk * SWA_GROUP + gq) * HEAD_DIM:(hk * SWA_GROUP + gq + 1) * HEAD_DIM]
                                  for gq in range(SWA_GROUP)], axis=0).astype(bf16)
            s = jnp.concatenate([_dot(q3, ktb[hs, :]), _dot_nt(q3, knb[:, hs])], axis=1) * ATTN_SCALE
            s = jnp.where(mask, s, NEG_BIG)
            sink = jnp.where(grp == 0, sink_ref[hk * SWA_GROUP],
                             jnp.where(grp == 1, sink_ref[hk * SWA_GROUP + 1], sink_ref[hk * SWA_GROUP + 2]))
            m = jnp.maximum(jnp.max(s, axis=-1, keepdims=True), sink)
            e = jnp.exp(s - m)
            denom = jnp.sum(e, axis=-1, keepdims=True) + jnp.exp(sink - m)
            eb = e.astype(bf16)
            o = (_dot_nt(eb[:, :BLOCK], vtb[hs, :]) + _dot(eb[:, BLOCK:], vnb[:, hs])) / denom
            for gq in range(SWA_GROUP):
                hq = hk * SWA_GROUP + gq
                o_ref[b, :, hq * HEAD_DIM:(hq + 1) * HEAD_DIM] = o[gq * tq:(gq + 1) * tq]


def _swa_step(proj, g_qnorm, sinks, cos, sin, cache_kt, cache_vt, k_new, v_new, n_new, bb):
    bsz, tq, _ = proj.shape
    assert SWA_GROUP == 3 and tq & (tq - 1) == 0
    g = jnp.tile(g_qnorm.reshape(1, HEAD_DIM), (1, HEADS_PER_SLAB))
    cache_spec = pl.BlockSpec((bb, KV_WIDTH, BLOCK), lambda b: (b, 0, 0))
    new_spec = pl.BlockSpec((bb, tq, KV_WIDTH), lambda b: (b, 0, 0))
    tab_spec = pl.BlockSpec((tq, LANES), lambda b: (0, 0))
    return pl.pallas_call(
        functools.partial(_swa_step_kernel, n_new=n_new),
        grid=(bsz // bb,),
        in_specs=[pl.BlockSpec(memory_space=pltpu.SMEM),
                  pl.BlockSpec((bb, tq, MAIN_WIDTH), lambda b: (b, 0, 0)),
                  pl.BlockSpec((1, LANES), lambda b: (0, 0)), tab_spec, tab_spec,
                  cache_spec, cache_spec, new_spec, new_spec],
        out_specs=[pl.BlockSpec((bb, tq, MAIN_WIDTH), lambda b: (b, 0, 0)), cache_spec, cache_spec],
        out_shape=[jax.ShapeDtypeStruct((bsz, tq, MAIN_WIDTH), f32),
                   jax.ShapeDtypeStruct(cache_kt.shape, f32), jax.ShapeDtypeStruct(cache_vt.shape, f32)],
        scratch_shapes=[pltpu.VMEM((tq, MAIN_WIDTH), f32)],
        compiler_params=_cp(("parallel",)),
        name="swa_step",
    )(sinks, proj, g, cos, sin, cache_kt, cache_vt, k_new, v_new)


def _out_mlp_kernel(*refs, tf, with_next):
    if with_next:
        (h_ref, main_ref, mem_ref, woa_ref, wob_ref, g_ref, wup_ref, wdn_ref,
         gkv_ref, wkv_ref, gk_ref, cos_ref, sin_ref, gin_ref, win_ref, o_ref, k_ref, v_ref, p_ref) = refs
    else:
        h_ref, main_ref, mem_ref, woa_ref, wob_ref, g_ref, wup_ref, wdn_ref, o_ref = refs
    h1 = (h_ref[...] + _dot(main_ref[...].astype(bf16), woa_ref[...])
          + _dot(mem_ref[...].astype(bf16), wob_ref[...]))
    hn = (h1 * lax.rsqrt(jnp.mean(h1 * h1, axis=-1, keepdims=True) + NORM_EPS) * g_ref[...]).astype(bf16)
    acc = None
    for j in range(D_FF // tf):
        u = _dot(hn, wup_ref[:, j * tf:(j + 1) * tf])
        u = jnp.square(jnp.maximum(u, 0.0)).astype(bf16)
        d = _dot(u, wdn_ref[j * tf:(j + 1) * tf, :])
        acc = d if acc is None else acc + d
    h2 = h1 + acc
    o_ref[...] = h2
    if with_next:
        xhat = h2 * lax.rsqrt(jnp.mean(h2 * h2, axis=-1, keepdims=True) + NORM_EPS)
        kv = _dot((xhat * gkv_ref[...]).astype(bf16), wkv_ref[...])
        slabs = _head_norm_rope(kv[:, :KV_WIDTH], gk_ref[...], cos_ref[...], sin_ref[...], _group_ones())
        for j, slab in enumerate(slabs):
            k_ref[:, j * LANES:(j + 1) * LANES] = slab
        v_ref[...] = kv[:, KV_WIDTH:]
        p_ref[...] = _dot((xhat * gin_ref[...]).astype(bf16), win_ref[...])


def _out_mlp(h, main, mem_o, w_out_bf16, g_mlp, w_up_bf16, w_down_bf16, tm, tf=1024, nxt=None):
    n = h.shape[0]
    tm = min(tm, n)
    rows = lambda w: pl.BlockSpec((tm, w), lambda i: (i, 0))
    resident = lambda a: pl.BlockSpec(a.shape, lambda i: (0, 0))
    woa, wob = w_out_bf16[:MAIN_WIDTH], w_out_bf16[MAIN_WIDTH:]
    row = lambda x: x.reshape(1, -1)
    args = [h, main, mem_o, woa, wob, row(g_mlp), w_up_bf16, w_down_bf16]
    in_specs = [rows(D_MODEL), rows(MAIN_WIDTH), rows(MEM_WIDTH)] + [resident(a) for a in args[3:]]
    out_specs, out_shape = [rows(D_MODEL)], [jax.ShapeDtypeStruct((n, D_MODEL), f32)]
    if nxt is not None:
        kv_norm, w_kv, k_norm, cos, sin, in_norm, w_in = nxt
        nt = cos.shape[0] // tm
        table = pl.BlockSpec((tm, LANES), lambda i: (i % nt, 0))
        extra = [row(kv_norm), w_kv, jnp.tile(row(k_norm), (1, KV_WIDTH // HEAD_DIM)), cos, sin, row(in_norm), w_in]
        args += extra
        in_specs += [resident(extra[0]), resident(w_kv), resident(extra[2]), table, table,
                     resident(extra[5]), resident(w_in)]
        out_specs += [rows(KV_WIDTH), rows(KV_WIDTH), rows(D_MODEL)]
        out_shape += [jax.ShapeDtypeStruct((n, KV_WIDTH), f32)] * 2 + [jax.ShapeDtypeStruct((n, D_MODEL), f32)]
    res = pl.pallas_call(
        functools.partial(_out_mlp_kernel, tf=tf, with_next=nxt is not None),
        grid=(n // tm,),
        in_specs=in_specs,
        out_specs=out_specs,
        out_shape=out_shape,
        compiler_params=_cp(("parallel",)),
        name="out_mlp",
    )(*args)
    return res if nxt is not None else res[0]


def _rope_tables(pos, reps):
    half = HEAD_DIM // 2
    freqs = jnp.power(ROPE_THETA, -jnp.arange(half, dtype=f32) / half)
    ang = pos.astype(f32)[:, None] * freqs[None, :]
    cos, sin = jnp.cos(ang), jnp.sin(ang)
    cos_h = jnp.concatenate([cos, cos], axis=-1)
    sin_h = jnp.concatenate([-sin, sin], axis=-1)
    return jnp.tile(cos_h, (1, reps)), jnp.tile(sin_h, (1, reps))


def _trunk(x, pos, mem_kt, mem_vt, shift0, wkv0, swa_cache, t_real, W):
    bsz, tp, _ = x.shape
    n = bsz * tp
    prompt = swa_cache is None
    mem_bb = 1 if prompt else STEP_MEM_SEQS
    act_dtype = bf16 if prompt else f32
    tm = ROW_TILE
    flat = lambda a: a.reshape(n, a.shape[-1])
    unflat = lambda a: a.reshape(bsz, tp, a.shape[-1])
    cos2, sin2 = _rope_tables(pos, HEADS_PER_SLAB)

    prep_w = (W["shift_mu"], W["w_w2"], W["w0"], W["w_a2"], W["a0"], W["w_g2"], W["k_k"], W["k_a"], W["r_k"])
    if prompt:
        seqs, q_mem, new_shift = _proj_prep(x, W["norm_mix"][0], W["w_in_a"], shift0, prep_w, PREP_TILE, bf16)
        q_col = 0
        main, new_wkv = _wkv_chunked(*seqs, W["lnx_w"], W["lnx_b"])
    else:
        q_mem = unflat(_norm_matmul(flat(x), W["norm_mix"][0], W["w_in_a"], tm))
        q_col = RWKV_COLS // MEM_WIDTH
        new_shift = q_mem[:, t_real - 1, :RWKV_COLS]
        steps = jnp.transpose(q_mem[:, :t_real, :RWKV_COLS], (1, 0, 2)).reshape(1, t_real * bsz, RWKV_COLS)
        seqs = [s.reshape(t_real, bsz, MAIN_WIDTH) for s in _rwkv_prep(steps, shift0, prep_w)]
        main, new_wkv = _wkv_steps(wkv0, seqs, W["lnx_w"], W["lnx_b"], tp)
    mem_o = _mem_attn(q_mem, q_col, W["mem_qnorm"][0], mem_kt, mem_vt, 0, MEM_Q_TILE, mem_bb, act_dtype)
    tables = (cos2, sin2) if prompt else (jnp.tile(cos2, (tm // tp, 1)), jnp.tile(sin2, (tm // tp, 1)))
    h, k_sh, v_sh, proj = _out_mlp(flat(x), flat(main), flat(mem_o), W["w_out"][0], W["norm_mlp"][0], W["w_up"][0],
                                   W["w_down"][0], tm,
                                   nxt=(W["kv_norm"], W["w_kv"], W["swa_knorm"], *tables, W["norm_mix"][1], W["w_in_b"]))
    k_sh, v_sh, proj = unflat(k_sh), unflat(v_sh), unflat(proj)

    if prompt:
        main = _swa(proj, W["swa_qnorm"], W["sinks"], cos2, sin2, k_sh, v_sh)
        win = min(WINDOW, tp)
        heads = lambda a: a.reshape(bsz, a.shape[1], SWA_KV_HEADS, HEAD_DIM)
        k_state, v_state = heads(k_sh[:, tp - win:]), heads(v_sh[:, tp - win:])
    else:
        ckt, cvt = swa_cache
        main, kt_new, vt_new = _swa_step(proj, W["swa_qnorm"], W["sinks"], cos2, sin2, ckt, cvt, k_sh, v_sh,
                                         t_real, bb=STEP_SWA_SEQS)
        untransposed = lambda a: jnp.transpose(a.reshape(bsz, SWA_KV_HEADS, HEAD_DIM, -1), (0, 3, 1, 2))
        k_state, v_state = untransposed(kt_new), untransposed(vt_new)
    mem_o = _mem_attn(proj, MAIN_WIDTH // MEM_WIDTH, W["mem_qnorm"][1], mem_kt, mem_vt, 1, MEM_Q_TILE, mem_bb, act_dtype)
    y = _out_mlp(h, flat(main), flat(mem_o), W["w_out"][1], W["norm_mlp"][1], W["w_up"][1], W["w_down"][1], tm)
    return unflat(y)[:, :t_real], new_shift[None], new_wkv[None], k_state, v_state


def kernel(x_prompt, x_sample, state_rwkv_shift, state_rwkv_wkv, cache_swa_k, cache_swa_v, cache_mem_k,
           cache_mem_v, mem_prompt, norm_mix, norm_mlp, w_out, w_up, w_down, mem_norm, w_mem_kv, mem_qnorm,
           mem_knorm, w_in_a, shift_mu, w_w2, w0, w_a2, a0, w_g2, k_k, k_a, r_k, lnx_w, lnx_b, w_in_b,
           swa_qnorm, sinks, kv_norm, w_kv, swa_knorm):
    W = dict(norm_mix=norm_mix, norm_mlp=norm_mlp, w_out=w_out.astype(bf16), w_up=w_up.astype(bf16),
             w_down=w_down.astype(bf16), mem_qnorm=mem_qnorm, w_in_a=w_in_a[0].astype(bf16),
             shift_mu=shift_mu[0], w_w2=w_w2[0], w0=w0[0], w_a2=w_a2[0], a0=a0[0], w_g2=w_g2[0],
             k_k=k_k[0], k_a=k_a[0], r_k=r_k[0].reshape(-1), lnx_w=lnx_w[0], lnx_b=lnx_b[0],
             w_in_b=w_in_b[0].astype(bf16), swa_qnorm=swa_qnorm[0], sinks=sinks[0], kv_norm=kv_norm,
             w_kv=w_kv.astype(bf16), swa_knorm=swa_knorm)
    bp, tp, _ = x_prompt.shape
    bs, ts, _ = x_sample.shape
    depth = norm_mix.shape[0]

    mem_flat = mem_prompt.reshape(bp * N_MEM, D_MODEL)
    p_mem_k, p_mem_v = [], []
    for l in range(depth):
        kv = _norm_matmul(mem_flat, mem_norm[l], w_mem_kv[l].astype(bf16), ROW_TILE)
        mk, mv = _kv_post(kv, mem_knorm[l], N_MEM)
        p_mem_k.append(mk.reshape(bp, N_MEM, MEM_WIDTH))
        p_mem_v.append(mv.reshape(bp, N_MEM, MEM_WIDTH))
    transposed = lambda a: jnp.swapaxes(jnp.stack(a), 2, 3)
    y_p, p_shift, p_wkv, p_k, p_v = _trunk(
        x_prompt, jnp.arange(tp), transposed(p_mem_k), transposed(p_mem_v), jnp.zeros((bp, RWKV_COLS), f32),
        None, None, tp, W)
    mem_heads = lambda a: jnp.stack(a).reshape(depth, bp, N_MEM, MEM_HEADS, HEAD_DIM)
    cache_t = lambda a: jnp.transpose(a, (0, 1, 3, 4, 2)).reshape(depth, bs, MEM_WIDTH, N_MEM)

    tpad = -(-ts // SUBLANES) * SUBLANES
    x_s = jnp.pad(x_sample, ((0, 0), (0, tpad - ts), (0, 0)))
    win = cache_swa_k.shape[1]
    assert win == WINDOW
    swa_t = lambda a: jnp.transpose(a, (0, 2, 3, 1)).reshape(bs, KV_WIDTH, win)
    y_s, s_shift, s_wkv, s_k, s_v = _trunk(
        x_s, PAST_LEN + jnp.arange(tpad), cache_t(cache_mem_k), cache_t(cache_mem_v),
        state_rwkv_shift[0], state_rwkv_wkv[0],
        (swa_t(cache_swa_k), swa_t(cache_swa_v)), ts, W)

    return (y_p, y_s, p_shift, p_wkv, p_k, p_v, mem_heads(p_mem_k), mem_heads(p_mem_v),
            s_shift, s_wkv, s_k, s_v)
```

```python
import functools
import math

import jax
import jax.numpy as jnp
from jax import lax
from jax.experimental import pallas as pl
from jax.experimental.pallas import tpu as pltpu

f32 = jnp.float32
bf16 = jnp.bfloat16

D_MODEL = 1024
HEAD_DIM = 64
MEM_HEADS = 4
MEM_WIDTH = MEM_HEADS * HEAD_DIM
MAIN_WIDTH = D_MODEL - MEM_WIDTH
RWKV_HEADS = MAIN_WIDTH // HEAD_DIM
DECAY_LORA = 64
AAA_LORA = 64
GATE_LORA = 128
RWKV_COLS = 3 * MAIN_WIDTH + DECAY_LORA + AAA_LORA + GATE_LORA
A_IN_COLS = RWKV_COLS + MEM_WIDTH
SWA_Q_HEADS = MAIN_WIDTH // HEAD_DIM
SWA_KV_HEADS = 4
SWA_GROUP = SWA_Q_HEADS // SWA_KV_HEADS
KV_WIDTH = SWA_KV_HEADS * HEAD_DIM
WINDOW = 128
BLOCK = 128
N_MEM = 256
D_FF = 4 * D_MODEL
PAST_LEN = 16384
ROPE_THETA = 10000.0
NORM_EPS = 1e-6
LNX_EPS = 6.4e-4
L2_EPS = 1e-12
ATTN_SCALE = HEAD_DIM ** -0.5

LANES = 128
SUBLANES = 8
HEADS_PER_SLAB = LANES // HEAD_DIM
N_SLABS = MAIN_WIDTH // LANES
CHUNK = 64
VMEM_LIMIT = 56 * 1024 * 1024
NEG_BIG = -1e30
ROW_TILE = 512
PREP_TILE = 256
MEM_Q_TILE = 256
STEP_MEM_SEQS = 8
STEP_SWA_SEQS = 4


def _cp(sem, vmem=VMEM_LIMIT):
    return pltpu.CompilerParams(dimension_semantics=sem, vmem_limit_bytes=vmem)


def _dot(a, b):
    return jnp.dot(a, b, preferred_element_type=f32)


def _dot_nt(a, b):
    return lax.dot_general(a, b, (((1,), (1,)), ((), ())), preferred_element_type=f32)


def _dot_tn(a, b):
    return lax.dot_general(a, b, (((0,), (0,)), ((), ())), preferred_element_type=f32)


def _iota(shape, dim):
    return lax.broadcasted_iota(jnp.int32, shape, dim)


def _group_ones():
    r = _iota((LANES, LANES), 0)
    c = _iota((LANES, LANES), 1)
    return jnp.where((r < HEAD_DIM) == (c < HEAD_DIM), 1.0, 0.0).astype(f32)


def _head_sum(x, gmat):
    w = x.shape[-1]
    gb = gmat.astype(bf16)
    hi = x.astype(bf16)
    lo = (x - hi.astype(f32)).astype(bf16)
    parts = [_dot(hi[:, j * LANES:(j + 1) * LANES], gb) + _dot(lo[:, j * LANES:(j + 1) * LANES], gb)
             for j in range(w // LANES)]
    return parts[0] if len(parts) == 1 else jnp.concatenate(parts, axis=-1)


def _cumsum_rows(x):
    rows = x.shape[0]
    row = _iota(x.shape, 0)
    step = 1
    while step < rows:
        x = x + jnp.where(row >= step, pltpu.roll(x, step, 0), 0.0)
        step *= 2
    return x


def _sigmoid(x):
    return 1.0 / (1.0 + jnp.exp(-x))


def _rot_half(x):
    lane = _iota(x.shape, 1)
    first = (lane & (HEAD_DIM - 1)) < (HEAD_DIM // 2)
    return jnp.where(first, pltpu.roll(x, LANES - HEAD_DIM // 2, 1), pltpu.roll(x, HEAD_DIM // 2, 1))


def _norm_matmul_kernel(x_ref, g_ref, w_ref, o_ref):
    x = x_ref[...]
    xn = x * lax.rsqrt(jnp.mean(x * x, axis=-1, keepdims=True) + NORM_EPS) * g_ref[...]
    o_ref[...] = _dot(xn.astype(bf16), w_ref[...])


def _norm_matmul(x, g, w_bf16, tm):
    n, d = x.shape
    m = w_bf16.shape[1]
    tm = min(tm, n)
    return pl.pallas_call(
        _norm_matmul_kernel,
        grid=(n // tm,),
        in_specs=[pl.BlockSpec((tm, d), lambda i: (i, 0)),
                  pl.BlockSpec((1, d), lambda i: (0, 0)),
                  pl.BlockSpec((d, m), lambda i: (0, 0))],
        out_specs=pl.BlockSpec((tm, m), lambda i: (i, 0)),
        out_shape=jax.ShapeDtypeStruct((n, m), f32),
        compiler_params=_cp(("parallel",)),
        name="norm_matmul",
    )(x, g.reshape(1, d), w_bf16)


def _head_norm_rope(x, g, cos, sin, gmat):
    xn = x * lax.rsqrt(_head_sum(x * x, gmat) * (1.0 / HEAD_DIM) + NORM_EPS) * g
    slabs = [xn[:, j * LANES:(j + 1) * LANES] for j in range(x.shape[-1] // LANES)]
    return [xs * cos + _rot_half(xs) * sin for xs in slabs]


def _kv_post_kernel(kv_ref, g_ref, k_ref, v_ref):
    gmat = _group_ones()
    x = kv_ref[:, :KV_WIDTH]
    k_ref[...] = x * lax.rsqrt(_head_sum(x * x, gmat) * (1.0 / HEAD_DIM) + NORM_EPS) * g_ref[...]
    v_ref[...] = kv_ref[:, KV_WIDTH:]


def _kv_post(kv, g_head, tm):
    n = kv.shape[0]
    tm = min(tm, n)
    g = jnp.tile(g_head.reshape(1, HEAD_DIM), (1, KV_WIDTH // HEAD_DIM))
    return pl.pallas_call(
        _kv_post_kernel,
        grid=(n // tm,),
        in_specs=[pl.BlockSpec((tm, 2 * KV_WIDTH), lambda i: (i, 0)),
                  pl.BlockSpec((1, KV_WIDTH), lambda i: (0, 0))],
        out_specs=[pl.BlockSpec((tm, KV_WIDTH), lambda i: (i, 0)),
                   pl.BlockSpec((tm, KV_WIDTH), lambda i: (i, 0))],
        out_shape=[jax.ShapeDtypeStruct((n, KV_WIDTH), f32)] * 2,
        compiler_params=_cp(("parallel",)),
        name="kv_post",
    )(kv, g)


def _time_mix_inputs(pf, prev, consts, outs):
    mu_ref, ww2_ref, w0_ref, wa2_ref, a0_ref, wg2_ref, kk_ref, ka_ref, rk_ref = consts
    r_o, lw_o, k_o, v_o, a_o, b_o, bonus_o, g_o = outs
    gmat = _group_ones()
    ps = pf + (prev - pf) * mu_ref[...]
    i1, i2, i3 = MAIN_WIDTH, 2 * MAIN_WIDTH, 3 * MAIN_WIDTH
    i4, i5 = i3 + DECAY_LORA, i3 + DECAY_LORA + AAA_LORA
    r, k, v = ps[:, :i1], ps[:, i1:i2], ps[:, i2:i3]
    wd, ad, gd = ps[:, i3:i4], ps[:, i4:i5], ps[:, i5:]
    lora = lambda x, w_ref: _dot(x.astype(bf16), w_ref[...].astype(bf16))
    lw = -math.exp(-0.5) * _sigmoid(w0_ref[...] + lora(jnp.tanh(wd), ww2_ref))
    a_sig = _sigmoid(a0_ref[...] + lora(ad, wa2_ref))
    g = lora(_sigmoid(gd), wg2_ref)
    kk = k * kk_ref[...]
    kk = kk / jnp.maximum(jnp.sqrt(_head_sum(kk * kk, gmat)), L2_EPS)
    k2 = k * (1.0 + (a_sig - 1.0) * ka_ref[...])
    r_o[0] = r.astype(r_o.dtype)
    lw_o[0] = lw
    k_o[0] = k2.astype(k_o.dtype)
    v_o[0] = v.astype(v_o.dtype)
    a_o[0] = (-kk).astype(a_o.dtype)
    b_o[0] = (kk * a_sig).astype(b_o.dtype)
    bonus_o[0] = (_head_sum(r * k2 * rk_ref[...], gmat) * v).astype(bonus_o.dtype)
    g_o[0] = g.astype(g_o.dtype)


def _rwkv_prep_kernel(p_ref, shift_ref, *refs):
    pf = p_ref[0]
    prev = jnp.concatenate([shift_ref[0], pf[:pf.shape[0] - shift_ref.shape[1]]], axis=0)
    _time_mix_inputs(pf, prev, refs[:9], refs[9:])


def _proj_prep_kernel(x_ref, gx_ref, w_ref, shift_ref, *refs):
    consts, outs, (q_o, last_o, carry_ref) = refs[:9], refs[9:17], refs[17:]
    i = pl.program_id(1)
    x = x_ref[0]
    xn = x * lax.rsqrt(jnp.mean(x * x, axis=-1, keepdims=True) + NORM_EPS) * gx_ref[...]
    proj = _dot(xn.astype(bf16), w_ref[...])
    q_o[0] = proj[:, RWKV_COLS:]
    pf = proj[:, :RWKV_COLS]
    first_prev = jnp.where(i == 0, shift_ref[0], carry_ref[...])
    prev = jnp.where(_iota(pf.shape, 0) == 0, first_prev, pltpu.roll(pf, 1, 0))
    last = pf[pf.shape[0] - 1:, :]
    carry_ref[...] = last
    last_o[0] = last
    _time_mix_inputs(pf, prev, consts, outs)


def _proj_prep(x, g_norm, w_in_bf16, shift_prev, prep_w, tt, seq_dtype):
    bsz, t, _ = x.shape
    row = lambda a: a.reshape(1, -1)
    full = lambda a: pl.BlockSpec(a.shape, lambda b, i: (0,) * a.ndim)
    mu, w_w2, w0, w_a2, a0, w_g2, k_k, k_a, r_k = prep_w
    consts = [row(mu), w_w2, row(w0), w_a2, row(a0), w_g2, row(k_k), row(k_a), row(r_k)]
    tile = lambda w: pl.BlockSpec((1, tt, w), lambda b, i: (b, i, 0))
    per_seq = pl.BlockSpec((1, 1, RWKV_COLS), lambda b, i: (b, 0, 0))
    res = pl.pallas_call(
        _proj_prep_kernel,
        grid=(bsz, t // tt),
        in_specs=[tile(D_MODEL), full(row(g_norm)), full(w_in_bf16), per_seq] + [full(c) for c in consts],
        out_specs=[tile(MAIN_WIDTH)] * 8 + [tile(MEM_WIDTH), per_seq],
        out_shape=[jax.ShapeDtypeStruct((bsz, t, MAIN_WIDTH), f32 if j == 1 else seq_dtype) for j in range(8)]
                  + [jax.ShapeDtypeStruct((bsz, t, MEM_WIDTH), f32), jax.ShapeDtypeStruct((bsz, 1, RWKV_COLS), f32)],
        scratch_shapes=[pltpu.VMEM((1, RWKV_COLS), f32)],
        compiler_params=_cp(("parallel", "arbitrary")),
        name="proj_prep",
    )(x, row(g_norm), w_in_bf16, shift_prev.reshape(bsz, 1, RWKV_COLS), *consts)
    return res[:8], res[8], res[9].reshape(bsz, RWKV_COLS)


def _rwkv_prep(steps, shift_prev, prep_w):
    _, t, _ = steps.shape
    row = lambda a: a.reshape(1, -1)
    full = lambda a: pl.BlockSpec(a.shape, lambda i: (0,) * a.ndim)
    mu, w_w2, w0, w_a2, a0, w_g2, k_k, k_a, r_k = prep_w
    consts = [row(mu), w_w2, row(w0), w_a2, row(a0), w_g2, row(k_k), row(k_a), row(r_k)]
    shift_rows = shift_prev[None]
    return pl.pallas_call(
        _rwkv_prep_kernel,
        grid=(1,),
        in_specs=[full(steps), full(shift_rows)] + [full(c) for c in consts],
        out_specs=[pl.BlockSpec((1, t, MAIN_WIDTH), lambda i: (0, 0, 0))] * 8,
        out_shape=[jax.ShapeDtypeStruct((1, t, MAIN_WIDTH), f32)] * 8,
        compiler_params=_cp(("arbitrary",)),
        name="rwkv_prep",
    )(steps, shift_rows, *consts)


def _wkv_chunk_kernel(r_ref, lw_ref, k_ref, v_ref, a_ref, b_ref, bonus_ref, g_ref, lnw_ref, lnb_ref,
                      o_ref, s_ref, h_ref, qt_ref, yv_ref, pp_ref, hv_ref, tc_ref):
    i = pl.program_id(2)
    c = CHUNK
    nc = r_ref.shape[1] // c
    n = range(nc)

    @pl.when(i == 0)
    def _():
        h_ref[...] = jnp.zeros_like(h_ref)
        qt_ref[...] = jnp.zeros_like(qt_ref)
        yv_ref[...] = jnp.zeros_like(yv_ref)
        pp_ref[...] = jnp.zeros_like(pp_ref)
        hv_ref[...] = jnp.zeros_like(hv_ref)
        tc_ref[...] = jnp.zeros_like(tc_ref)

    gmat = _group_ones()
    bd = gmat > 0.5
    row = _iota((LANES, LANES), 0)
    col = _iota((LANES, LANES), 1)
    t_row = row & (c - 1)
    s_col = col & (c - 1)
    m_strict = s_col < t_row
    m_incl = s_col <= t_row
    top = row < c
    lane_lo = _iota((1, LANES), 1) < HEAD_DIM
    eye = row == col
    zeros = jnp.zeros((c, LANES), f32)
    mm = lambda x: x.astype(bf16)
    gb = gmat.astype(bf16)
    sls = [slice(j * c, (j + 1) * c) for j in n]

    state = {"h": h_ref[...], "ys": []}

    def state_step(j):
        hb = mm(state["h"])
        state["ys"].append(_dot(qt_ref[j], hb) + yv_ref[j])
        state["h"] = state["h"] * tc_ref[j] + _dot(pp_ref[j], hb) + hv_ref[j]

    lws = [lw_ref[0, sl, :] for sl in sls]
    cums = [_cumsum_rows(lw) for lw in lws]
    state_step(0)
    vs = [v_ref[0, sl, :].astype(f32) for sl in sls]
    tots = [cum[c - 1:c, :] for cum in cums]
    ats = [a_ref[0, sl, :].astype(f32) * jnp.exp(cum - lw) for sl, cum, lw in zip(sls, cums, lws)]
    rts = [r_ref[0, sl, :].astype(f32) * jnp.exp(cum) for sl, cum in zip(sls, cums)]
    bks, bkts = [], []
    for sl, cum, tot in zip(sls, cums, tots):
        b, k = b_ref[0, sl, :].astype(f32), k_ref[0, sl, :].astype(f32)
        e_neg, e_rem = jnp.exp(-cum), jnp.exp(tot - cum)
        bks.append(mm(jnp.concatenate([b * e_neg, k * e_neg], axis=0)))
        bkts.append(mm(jnp.concatenate([b * e_rem, k * e_rem], axis=0)))
    lhss = [mm(jnp.concatenate([jnp.where(lane_lo, at, 0.0), jnp.where(lane_lo, 0.0, at),
                                jnp.where(lane_lo, rt, 0.0), jnp.where(lane_lo, 0.0, rt)], axis=0))
            for at, rt in zip(ats, rts)]
    gms = [_dot_nt(lhss[j], bks[j]) for j in n]
    state_step(1)
    aaks = [jnp.where(m_strict, gm[:LANES], 0.0) for gm in gms]
    arks = [mm(jnp.where(m_incl, gm[LANES:], 0.0)) for gm in gms]
    aak_sws = [pltpu.roll(aak, HEAD_DIM, 1) for aak in aaks]
    ms = [jnp.where(bd, jnp.where(top, aaks[j], aak_sws[j]), 0.0) for j in n]
    aks = [mm(jnp.where(bd, jnp.where(top, aak_sws[j], aaks[j]), 0.0)) for j in n]
    vvs = [mm(jnp.concatenate([pltpu.roll(v, HEAD_DIM, 1)] * 2, axis=0)) for v in vs]
    akvs = [_dot(aks[j], vvs[j]) for j in n]
    state_step(2)
    xs = [jnp.where(bd, jnp.concatenate([ats[j], ats[j]], axis=0), akvs[j]) for j in n]
    means = None
    for lvl in range(6):
        if lvl < 5:
            zs = [_dot(mm(ms[j]), mm(jnp.concatenate([xs[j], ms[j]], axis=1))) for j in n]
            xs = [xs[j] + zs[j][:, :LANES] for j in n]
            ms = [zs[j][:, LANES:] for j in n]
            state_step(3 + lvl)
        else:
            zs = [_dot(mm(ms[j]), mm(xs[j])) for j in n]
            xs = [xs[j] + zs[j] for j in n]
            means = [_dot(mm(y), gb) * (1.0 / HEAD_DIM) for y in state["ys"]]
    assert len(state["ys"]) == nc == 8
    h_ref[...] = state["h"]
    rhss = []
    for j in n:
        x = xs[j]
        ta = jnp.where(lane_lo, x[:c], x[c:])
        uv = pltpu.roll(jnp.where(lane_lo, x[c:], x[:c]), HEAD_DIM, 1)
        rhss.append(mm(jnp.concatenate([jnp.concatenate([ta, uv], axis=1),
                                        jnp.concatenate([zeros, vs[j]], axis=1)], axis=0)))
    z2s = [_dot(arks[j], rhss[j]) for j in n]
    pzs = [_dot_tn(bkts[j], rhss[j]) for j in n]
    ycs = [state["ys"][j] - means[j] for j in n]
    vrs = [_dot(mm(yc * yc), gb) * (1.0 / HEAD_DIM) for yc in ycs]
    for j in n:
        yn = ycs[j] * lax.rsqrt(vrs[j] + LNX_EPS) * lnw_ref[...] + lnb_ref[...]
        o_ref[0, sls[j], :] = ((yn + bonus_ref[0, sls[j], :].astype(f32))
                               * g_ref[0, sls[j], :].astype(f32)).astype(o_ref.dtype)
    for j in n:
        qt_ref[j] = mm(rts[j] + jnp.where(lane_lo, z2s[j][:c, :LANES], z2s[j][c:, :LANES]))
        yv_ref[j] = jnp.where(lane_lo, z2s[j][:c, LANES:], z2s[j][c:, LANES:])
        pp_ref[j] = mm(jnp.where(bd, pzs[j][:, :LANES], 0.0))
        hv_ref[j] = jnp.where(bd, pzs[j][:, LANES:], 0.0)
        tc_ref[j] = jnp.sum(jnp.where(eye, jnp.exp(tots[j]), 0.0), axis=1, keepdims=True)

    @pl.when(i == pl.num_programs(2) - 1)
    def _():
        s_ref[0, 0] = h_ref[...].T


def _wkv_chunked(r, lw, k, v, a, b, bonus, g, lnx_w, lnx_b):
    bsz, t, _ = r.shape
    nc = 8
    tt = nc * CHUNK
    nt = t // tt
    assert t % tt == 0
    cur = pl.BlockSpec((1, tt, LANES), lambda bb, hp, i: (bb, jnp.minimum(i, nt - 1), hp))
    pend = pl.BlockSpec((1, tt, LANES), lambda bb, hp, i: (bb, jnp.maximum(i - 1, 0), hp))
    vec = pl.BlockSpec((1, LANES), lambda bb, hp, i: (0, hp))
    out, st = pl.pallas_call(
        _wkv_chunk_kernel,
        grid=(bsz, N_SLABS, nt + 1),
        in_specs=[cur] * 6 + [pend, pend, vec, vec],
        out_specs=[pend, pl.BlockSpec((1, 1, LANES, LANES), lambda bb, hp, i: (bb, hp, 0, 0))],
        out_shape=[jax.ShapeDtypeStruct((bsz, t, MAIN_WIDTH), bf16),
                   jax.ShapeDtypeStruct((bsz, N_SLABS, LANES, LANES), f32)],
        scratch_shapes=[pltpu.VMEM((LANES, LANES), f32),
                        pltpu.VMEM((nc, CHUNK, LANES), bf16), pltpu.VMEM((nc, CHUNK, LANES), f32),
                        pltpu.VMEM((nc, LANES, LANES), bf16), pltpu.VMEM((nc, LANES, LANES), f32),
                        pltpu.VMEM((nc, LANES, 1), f32)],
        compiler_params=_cp(("parallel", "parallel", "arbitrary")),
        name="wkv_chunked",
    )(r, lw, k, v, a, b, bonus, g, lnx_w.reshape(1, -1), lnx_b.reshape(1, -1))
    st = jnp.stack([st[:, :, :HEAD_DIM, :HEAD_DIM], st[:, :, HEAD_DIM:, HEAD_DIM:]], axis=2)
    return out, st.reshape(bsz, RWKV_HEADS, HEAD_DIM, HEAD_DIM)


def _wkv_steps_kernel(s_ref, seq_ref, lnw_ref, lnb_ref, o_ref, so_ref, w_ref, y_ref, *, n_steps):
    i_r, i_lw, i_k, i_v, i_a, i_b, i_bonus, i_g = range(8)
    w_ref[...] = jnp.exp(seq_ref[i_lw])
    sub = _iota((SUBLANES, LANES), 0)

    def group(vg, carry):
        v0 = pl.multiple_of(vg * SUBLANES, SUBLANES)
        v_rows = [seq_ref[i_v, t, pl.ds(v0, SUBLANES), :] for t in range(n_steps)]
        ys = [jnp.zeros((SUBLANES, LANES), f32) for _ in range(n_steps)]
        for j in range(SUBLANES):
            sv = s_ref[0, v0 + j]
            for t in range(n_steps):
                sa = jnp.sum(sv * seq_ref[i_a, t], axis=0, keepdims=True)
                sv = sv * w_ref[t] + sa * seq_ref[i_b, t] + v_rows[t][j:j + 1, :] * seq_ref[i_k, t]
                y = jnp.sum(sv * seq_ref[i_r, t], axis=0, keepdims=True)
                ys[t] = jnp.where(sub == j, y, ys[t])
            so_ref[0, v0 + j] = sv
        for t in range(n_steps):
            y_ref[t, pl.ds(v0, SUBLANES), :] = ys[t]
        return carry

    lax.fori_loop(0, HEAD_DIM // SUBLANES, group, 0)
    y = y_ref[...]
    mean = jnp.mean(y, axis=1, keepdims=True)
    yc = y - mean
    var = jnp.mean(yc * yc, axis=1, keepdims=True)
    yn = yc * lax.rsqrt(var + LNX_EPS) * lnw_ref[...] + lnb_ref[...]
    o_ref[...] = (yn + seq_ref[i_bonus]) * seq_ref[i_g]


def _wkv_steps(state, seqs, lnx_w, lnx_b, tp):
    n_steps, bsz, _ = seqs[0].shape
    assert bsz % LANES == 0
    state_t = jnp.transpose(state, (1, 2, 3, 0))
    seq_t = jnp.transpose(jnp.stack(seqs), (0, 1, 3, 2))
    lanes = lambda x: jnp.broadcast_to(x.reshape(MAIN_WIDTH, 1), (MAIN_WIDTH, LANES))
    sspec = pl.BlockSpec((1, HEAD_DIM, HEAD_DIM, LANES), lambda h, bi: (h, 0, 0, bi))
    vec = pl.BlockSpec((HEAD_DIM, LANES), lambda h, bi: (h, 0))
    ospec = pl.BlockSpec((n_steps, HEAD_DIM, LANES), lambda h, bi: (0, h, bi))
    out, new_state = pl.pallas_call(
        functools.partial(_wkv_steps_kernel, n_steps=n_steps),
        grid=(RWKV_HEADS, bsz // LANES),
        in_specs=[sspec, pl.BlockSpec((8, n_steps, HEAD_DIM, LANES), lambda h, bi: (0, 0, h, bi)), vec, vec],
        out_specs=[ospec, sspec],
        out_shape=[jax.ShapeDtypeStruct((n_steps, MAIN_WIDTH, bsz), f32),
                   jax.ShapeDtypeStruct(state_t.shape, f32)],
        scratch_shapes=[pltpu.VMEM((n_steps, HEAD_DIM, LANES), f32), pltpu.VMEM((n_steps, HEAD_DIM, LANES), f32)],
        compiler_params=_cp(("parallel", "parallel")),
        name="wkv_steps",
    )(state_t, seq_t, lanes(lnx_w), lanes(lnx_b))
    out = jnp.pad(jnp.transpose(out, (2, 0, 1)), ((0, 0), (0, tp - n_steps), (0, 0)))
    return out, jnp.transpose(new_state, (3, 0, 1, 2))


def _mem_attn_kernel(q_ref, g_ref, kt_ref, vt_ref, o_ref):
    gmat = _group_ones()
    bb, tq, _ = q_ref.shape
    head = _iota((1, MEM_WIDTH), 1) // HEAD_DIM
    seqs = range(bb)
    qs = []
    for b in seqs:
        q = q_ref[b]
        qn = q * lax.rsqrt(_head_sum(q * q, gmat) * (1.0 / HEAD_DIM) + NORM_EPS) * g_ref[...]
        qs.append(jnp.concatenate([jnp.where(head == h, qn, 0.0) for h in range(MEM_HEADS)], axis=0).astype(bf16))
    ss = [_dot(qs[b], kt_ref[0, b].astype(bf16)) * ATTN_SCALE for b in seqs]
    es = [jnp.exp(s - jnp.max(s, axis=-1, keepdims=True)) for s in ss]
    o4s = [_dot_nt(es[b].astype(bf16), vt_ref[0, b].astype(bf16)) / jnp.sum(es[b], axis=-1, keepdims=True)
           for b in seqs]
    for b in seqs:
        o = jnp.zeros((tq, MEM_WIDTH), f32)
        for h in range(MEM_HEADS):
            o = o + jnp.where(head == h, o4s[b][h * tq:(h + 1) * tq], 0.0)
        o_ref[b] = o.astype(o_ref.dtype)


def _mem_attn(proj, col_block, g_qnorm, mem_kt, mem_vt, layer, tq, bb, out_dtype):
    bsz, t, _ = proj.shape
    tq = min(tq, t)
    g = jnp.tile(g_qnorm.reshape(1, HEAD_DIM), (1, MEM_HEADS))
    kv_spec = pl.BlockSpec((1, bb, MEM_WIDTH, N_MEM), lambda b, i: (layer, b, 0, 0))
    return pl.pallas_call(
        _mem_attn_kernel,
        grid=(bsz // bb, t // tq),
        in_specs=[pl.BlockSpec((bb, tq, MEM_WIDTH), lambda b, i: (b, i, col_block)),
                  pl.BlockSpec((1, MEM_WIDTH), lambda b, i: (0, 0)),
                  kv_spec, kv_spec],
        out_specs=pl.BlockSpec((bb, tq, MEM_WIDTH), lambda b, i: (b, i, 0)),
        out_shape=jax.ShapeDtypeStruct((bsz, t, MEM_WIDTH), out_dtype),
        compiler_params=_cp(("parallel", "parallel")),
        name="mem_attn",
    )(proj, g, mem_kt, mem_vt)


def _swa_kernel(sink_ref, q_ref, g_ref, cos_ref, sin_ref, kp_ref, kc_ref, vp_ref, vc_ref, o_ref, qn_ref):
    i = pl.program_id(1)
    gmat = _group_ones()
    tq = q_ref.shape[1]
    for j in range(N_SLABS):
        sl = slice(j * LANES, (j + 1) * LANES)
        x = q_ref[0, :, sl]
        xn = x * lax.rsqrt(_head_sum(x * x, gmat) * (1.0 / HEAD_DIM) + NORM_EPS) * g_ref[...]
        qn_ref[:, sl] = xn * cos_ref[...] + _rot_half(xn) * sin_ref[...]
    keys = jnp.concatenate([kp_ref[0], kc_ref[0]], axis=0).astype(bf16)
    vals = jnp.concatenate([vp_ref[0], vc_ref[0]], axis=0).astype(bf16)
    rows = SWA_GROUP * tq
    row = _iota((rows, 2 * BLOCK), 0) & (tq - 1)
    col = _iota((rows, 2 * BLOCK), 1)
    mask = (col > row) & (col <= row + WINDOW) & ((i > 0) | (col >= BLOCK))
    grp = _iota((rows, 1), 0) // tq
    kv = range(SWA_KV_HEADS)
    hs = [slice(hk * HEAD_DIM, (hk + 1) * HEAD_DIM) for hk in kv]
    q3s = [jnp.concatenate([qn_ref[:, (hk * SWA_GROUP + gq) * HEAD_DIM:(hk * SWA_GROUP + gq + 1) * HEAD_DIM]
                            for gq in range(SWA_GROUP)], axis=0).astype(bf16) for hk in kv]
    ss = [jnp.where(mask, _dot_nt(q3s[hk], keys[:, hs[hk]]) * ATTN_SCALE, NEG_BIG) for hk in kv]
    sinks = [jnp.where(grp == 0, sink_ref[hk * SWA_GROUP],
                       jnp.where(grp == 1, sink_ref[hk * SWA_GROUP + 1], sink_ref[hk * SWA_GROUP + 2])) for hk in kv]
    ms = [jnp.maximum(jnp.max(ss[hk], axis=-1, keepdims=True), sinks[hk]) for hk in kv]
    es = [jnp.exp(ss[hk] - ms[hk]) for hk in kv]
    denoms = [jnp.sum(es[hk], axis=-1, keepdims=True) + jnp.exp(sinks[hk] - ms[hk]) for hk in kv]
    outs = [_dot(es[hk].astype(bf16), vals[:, hs[hk]]) / denoms[hk] for hk in kv]
    for hk in kv:
        for gq in range(SWA_GROUP):
            hq = hk * SWA_GROUP + gq
            o_ref[0, :, hq * HEAD_DIM:(hq + 1) * HEAD_DIM] = outs[hk][gq * tq:(gq + 1) * tq].astype(o_ref.dtype)


def _swa(proj, g_qnorm, sinks, cos, sin, k, v):
    bsz, t, _ = proj.shape
    assert SWA_GROUP == 3
    tq = BLOCK
    g = jnp.tile(g_qnorm.reshape(1, HEAD_DIM), (1, HEADS_PER_SLAB))
    prev = pl.BlockSpec((1, BLOCK, KV_WIDTH), lambda b, i: (b, jnp.maximum(i - 1, 0), 0))
    cur = pl.BlockSpec((1, BLOCK, KV_WIDTH), lambda b, i: (b, i, 0))
    return pl.pallas_call(
        _swa_kernel,
        grid=(bsz, t // tq),
        in_specs=[pl.BlockSpec(memory_space=pltpu.SMEM),
                  pl.BlockSpec((1, tq, MAIN_WIDTH), lambda b, i: (b, i, 0)),
                  pl.BlockSpec((1, LANES), lambda b, i: (0, 0)),
                  pl.BlockSpec((tq, LANES), lambda b, i: (i, 0)),
                  pl.BlockSpec((tq, LANES), lambda b, i: (i, 0)),
                  prev, cur, prev, cur],
        out_specs=pl.BlockSpec((1, tq, MAIN_WIDTH), lambda b, i: (b, i, 0)),
        out_shape=jax.ShapeDtypeStruct((bsz, t, MAIN_WIDTH), bf16),
        scratch_shapes=[pltpu.VMEM((tq, MAIN_WIDTH), f32)],
        compiler_params=_cp(("parallel", "parallel")),
        name="swa_attn",
    )(sinks, proj, g, cos, sin, k, k, v, v)


def _swa_step_kernel(sink_ref, q_ref, g_ref, cos_ref, sin_ref, kt_ref, vt_ref, kn_ref, vn_ref,
                     o_ref, kto_ref, vto_ref, qn_ref, *, n_new):
    gmat = _group_ones()
    bb, tq, _ = q_ref.shape
    row = _iota((SWA_GROUP * tq, 2 * BLOCK), 0) & (tq - 1)
    col = _iota((SWA_GROUP * tq, 2 * BLOCK), 1)
    mask = (col > row) & (col <= row + WINDOW)
    grp = _iota((SWA_GROUP * tq, 1), 0) // tq
    keep = _iota((KV_WIDTH, BLOCK), 1) < BLOCK - n_new
    pad = jnp.zeros((BLOCK - tq, KV_WIDTH), f32)
    ktb, vtb, knb, vnb = [], [], [], []
    for b in range(bb):
        for j in range(N_SLABS):
            sl = slice(j * LANES, (j + 1) * LANES)
            x = q_ref[b, :, sl]
            xn = x * lax.rsqrt(_head_sum(x * x, gmat) * (1.0 / HEAD_DIM) + NORM_EPS) * g_ref[...]
            qn_ref[b, :, sl] = xn * cos_ref[...] + _rot_half(xn) * sin_ref[...]
        kt, vt = kt_ref[b], vt_ref[b]
        kn = jnp.concatenate([kn_ref[b], pad], axis=0)
        vn = jnp.concatenate([vn_ref[b], pad], axis=0)
        kto_ref[b] = jnp.where(keep, pltpu.roll(kt, BLOCK - n_new, 1), pltpu.roll(kn.T, BLOCK - n_new, 1))
        vto_ref[b] = jnp.where(keep, pltpu.roll(vt, BLOCK - n_new, 1), pltpu.roll(vn.T, BLOCK - n_new, 1))
        ktb.append(kt.astype(bf16))
        vtb.append(vt.astype(bf16))
        knb.append(kn.astype(bf16))
        vnb.append(vn.astype(bf16))
    pairs = [(b, hk) for b in range(bb) for hk in range(SWA_KV_HEADS)]
    hs = [slice(hk * HEAD_DIM, (hk + 1) * HEAD_DIM) for hk in range(SWA_KV_HEADS)]
    q3s = [jnp.concatenate([qn_ref[b, :, (hk * SWA_GROUP + gq) * HEAD_DIM:(hk * SWA_GROUP + gq + 1) * HEAD_DIM]
                            for gq in range(SWA_GROUP)], axis=0).astype(bf16) for b, hk in pairs]
    ss = [jnp.where(mask, jnp.concatenate([_dot(q3, ktb[b][hs[hk], :]), _dot_nt(q3, knb[b][:, hs[hk]])], axis=1)
                    * ATTN_SCALE, NEG_BIG) for q3, (b, hk) in zip(q3s, pairs)]
    sinks = [jnp.where(grp == 0, sink_ref[hk * SWA_GROUP],
                       jnp.where(grp == 1, sink_ref[hk * SWA_GROUP + 1], sink_ref[hk * SWA_GROUP + 2]))
             for hk in range(SWA_KV_HEADS)]
    ms = [jnp.maximum(jnp.max(s, axis=-1, keepdims=True), sinks[hk]) for s, (b, hk) in zip(ss, pairs)]
    es = [jnp.exp(s - m) for s, m in zip(ss, ms)]
    denoms = [jnp.sum(e, axis=-1, keepdims=True) + jnp.exp(sinks[hk] - m) for e, m, (b, hk) in zip(es, ms, pairs)]
    ebs = [e.astype(bf16) for e in es]
    outs = [(_dot_nt(eb[:, :BLOCK], vtb[b][hs[hk], :]) + _dot(eb[:, BLOCK:], vnb[b][:, hs[hk]])) / d
            for eb, d, (b, hk) in zip(ebs, denoms, pairs)]
    for o, (b, hk) in zip(outs, pairs):
        for gq in range(SWA_GROUP):
            hq = hk * SWA_GROUP + gq
            o_ref[b, :, hq * HEAD_DIM:(hq + 1) * HEAD_DIM] = o[gq * tq:(gq + 1) * tq]


def _swa_step(proj, g_qnorm, sinks, cos, sin, cache_kt, cache_vt, k_new, v_new, n_new, bb):
    bsz, tq, _ = proj.shape
    assert SWA_GROUP == 3 and tq & (tq - 1) == 0
    g = jnp.tile(g_qnorm.reshape(1, HEAD_DIM), (1, HEADS_PER_SLAB))
    cache_spec = pl.BlockSpec((bb, KV_WIDTH, BLOCK), lambda b: (b, 0, 0))
    new_spec = pl.BlockSpec((bb, tq, KV_WIDTH), lambda b: (b, 0, 0))
    tab_spec = pl.BlockSpec((tq, LANES), lambda b: (0, 0))
    return pl.pallas_call(
        functools.partial(_swa_step_kernel, n_new=n_new),
        grid=(bsz // bb,),
        in_specs=[pl.BlockSpec(memory_space=pltpu.SMEM),
                  pl.BlockSpec((bb, tq, MAIN_WIDTH), lambda b: (b, 0, 0)),
                  pl.BlockSpec((1, LANES), lambda b: (0, 0)), tab_spec, tab_spec,
                  cache_spec, cache_spec, new_spec, new_spec],
        out_specs=[pl.BlockSpec((bb, tq, MAIN_WIDTH), lambda b: (b, 0, 0)), cache_spec, cache_spec],
        out_shape=[jax.ShapeDtypeStruct((bsz, tq, MAIN_WIDTH), f32),
                   jax.ShapeDtypeStruct(cache_kt.shape, f32), jax.ShapeDtypeStruct(cache_vt.shape, f32)],
        scratch_shapes=[pltpu.VMEM((bb, tq, MAIN_WIDTH), f32)],
        compiler_params=_cp(("parallel",)),
        name="swa_step",
    )(sinks, proj, g, cos, sin, cache_kt, cache_vt, k_new, v_new)


def _out_mlp_kernel(*refs, tf, with_next):
    if with_next:
        (h_ref, main_ref, mem_ref, woa_ref, wob_ref, g_ref, wup_ref, wdn_ref,
         gkv_ref, wkv_ref, gk_ref, cos_ref, sin_ref, gin_ref, win_ref, o_ref, k_ref, v_ref, p_ref) = refs
    else:
        h_ref, main_ref, mem_ref, woa_ref, wob_ref, g_ref, wup_ref, wdn_ref, o_ref = refs
    h1 = (h_ref[...] + _dot(main_ref[...].astype(bf16), woa_ref[...])
          + _dot(mem_ref[...].astype(bf16), wob_ref[...]))
    hn = (h1 * lax.rsqrt(jnp.mean(h1 * h1, axis=-1, keepdims=True) + NORM_EPS) * g_ref[...]).astype(bf16)
    acc = None
    for j in range(D_FF // tf):
        u = _dot(hn, wup_ref[:, j * tf:(j + 1) * tf])
        u = jnp.square(jnp.maximum(u, 0.0)).astype(bf16)
        d = _dot(u, wdn_ref[j * tf:(j + 1) * tf, :])
        acc = d if acc is None else acc + d
    h2 = h1 + acc
    o_ref[...] = h2
    if with_next:
        xhat = h2 * lax.rsqrt(jnp.mean(h2 * h2, axis=-1, keepdims=True) + NORM_EPS)
        kv = _dot((xhat * gkv_ref[...]).astype(bf16), wkv_ref[...])
        slabs = _head_norm_rope(kv[:, :KV_WIDTH], gk_ref[...], cos_ref[...], sin_ref[...], _group_ones())
        for j, slab in enumerate(slabs):
            k_ref[:, j * LANES:(j + 1) * LANES] = slab
        v_ref[...] = kv[:, KV_WIDTH:]
        p_ref[...] = _dot((xhat * gin_ref[...]).astype(bf16), win_ref[...])


def _out_mlp(h, main, mem_o, w_out_bf16, g_mlp, w_up_bf16, w_down_bf16, tm, tf=1024, nxt=None):
    n = h.shape[0]
    tm = min(tm, n)
    rows = lambda w: pl.BlockSpec((tm, w), lambda i: (i, 0))
    resident = lambda a: pl.BlockSpec(a.shape, lambda i: (0, 0))
    woa, wob = w_out_bf16[:MAIN_WIDTH], w_out_bf16[MAIN_WIDTH:]
    row = lambda x: x.reshape(1, -1)
    args = [h, main, mem_o, woa, wob, row(g_mlp), w_up_bf16, w_down_bf16]
    in_specs = [rows(D_MODEL), rows(MAIN_WIDTH), rows(MEM_WIDTH)] + [resident(a) for a in args[3:]]
    out_specs, out_shape = [rows(D_MODEL)], [jax.ShapeDtypeStruct((n, D_MODEL), f32)]
    if nxt is not None:
        kv_norm, w_kv, k_norm, cos, sin, in_norm, w_in = nxt
        nt = cos.shape[0] // tm
        table = pl.BlockSpec((tm, LANES), lambda i: (i % nt, 0))
        extra = [row(kv_norm), w_kv, jnp.tile(row(k_norm), (1, KV_WIDTH // HEAD_DIM)), cos, sin, row(in_norm), w_in]
        args += extra
        in_specs += [resident(extra[0]), resident(w_kv), resident(extra[2]), table, table,
                     resident(extra[5]), resident(w_in)]
        out_specs += [rows(KV_WIDTH), rows(KV_WIDTH), rows(D_MODEL)]
        out_shape += [jax.ShapeDtypeStruct((n, KV_WIDTH), f32)] * 2 + [jax.ShapeDtypeStruct((n, D_MODEL), f32)]
    res = pl.pallas_call(
        functools.partial(_out_mlp_kernel, tf=tf, with_next=nxt is not None),
        grid=(n // tm,),
        in_specs=in_specs,
        out_specs=out_specs,
        out_shape=out_shape,
        compiler_params=_cp(("parallel",)),
        name="out_mlp",
    )(*args)
    return res if nxt is not None else res[0]


def _rope_tables(pos, reps):
    half = HEAD_DIM // 2
    freqs = jnp.power(ROPE_THETA, -jnp.arange(half, dtype=f32) / half)
    ang = pos.astype(f32)[:, None] * freqs[None, :]
    cos, sin = jnp.cos(ang), jnp.sin(ang)
    cos_h = jnp.concatenate([cos, cos], axis=-1)
    sin_h = jnp.concatenate([-sin, sin], axis=-1)
    return jnp.tile(cos_h, (1, reps)), jnp.tile(sin_h, (1, reps))


def _trunk(x, pos, mem_kt, mem_vt, shift0, wkv0, swa_cache, t_real, W):
    bsz, tp, _ = x.shape
    n = bsz * tp
    prompt = swa_cache is None
    mem_bb = 1 if prompt else STEP_MEM_SEQS
    act_dtype = bf16 if prompt else f32
    tm = ROW_TILE
    flat = lambda a: a.reshape(n, a.shape[-1])
    unflat = lambda a: a.reshape(bsz, tp, a.shape[-1])
    cos2, sin2 = _rope_tables(pos, HEADS_PER_SLAB)

    prep_w = (W["shift_mu"], W["w_w2"], W["w0"], W["w_a2"], W["a0"], W["w_g2"], W["k_k"], W["k_a"], W["r_k"])
    if prompt:
        seqs, q_mem, new_shift = _proj_prep(x, W["norm_mix"][0], W["w_in_a"], shift0, prep_w, PREP_TILE, bf16)
        q_col = 0
        main, new_wkv = _wkv_chunked(*seqs, W["lnx_w"], W["lnx_b"])
    else:
        q_mem = unflat(_norm_matmul(flat(x), W["norm_mix"][0], W["w_in_a"], tm))
        q_col = RWKV_COLS // MEM_WIDTH
        new_shift = q_mem[:, t_real - 1, :RWKV_COLS]
        steps = jnp.transpose(q_mem[:, :t_real, :RWKV_COLS], (1, 0, 2)).reshape(1, t_real * bsz, RWKV_COLS)
        seqs = [s.reshape(t_real, bsz, MAIN_WIDTH) for s in _rwkv_prep(steps, shift0, prep_w)]
        main, new_wkv = _wkv_steps(wkv0, seqs, W["lnx_w"], W["lnx_b"], tp)
    mem_o = _mem_attn(q_mem, q_col, W["mem_qnorm"][0], mem_kt, mem_vt, 0, MEM_Q_TILE, mem_bb, act_dtype)
    tables = (cos2, sin2) if prompt else (jnp.tile(cos2, (tm // tp, 1)), jnp.tile(sin2, (tm // tp, 1)))
    h, k_sh, v_sh, proj = _out_mlp(flat(x), flat(main), flat(mem_o), W["w_out"][0], W["norm_mlp"][0], W["w_up"][0],
                                   W["w_down"][0], tm,
                                   nxt=(W["kv_norm"], W["w_kv"], W["swa_knorm"], *tables, W["norm_mix"][1], W["w_in_b"]))
    k_sh, v_sh, proj = unflat(k_sh), unflat(v_sh), unflat(proj)

    if prompt:
        main = _swa(proj, W["swa_qnorm"], W["sinks"], cos2, sin2, k_sh, v_sh)
        win = min(WINDOW, tp)
        heads = lambda a: a.reshape(bsz, a.shape[1], SWA_KV_HEADS, HEAD_DIM)
        k_state, v_state = heads(k_sh[:, tp - win:]), heads(v_sh[:, tp - win:])
    else:
        ckt, cvt = swa_cache
        main, kt_new, vt_new = _swa_step(proj, W["swa_qnorm"], W["sinks"], cos2, sin2, ckt, cvt, k_sh, v_sh,
                                         t_real, bb=STEP_SWA_SEQS)
        untransposed = lambda a: jnp.transpose(a.reshape(bsz, SWA_KV_HEADS, HEAD_DIM, -1), (0, 3, 1, 2))
        k_state, v_state = untransposed(kt_new), untransposed(vt_new)
    mem_o = _mem_attn(proj, MAIN_WIDTH // MEM_WIDTH, W["mem_qnorm"][1], mem_kt, mem_vt, 1, MEM_Q_TILE, mem_bb, act_dtype)
    y = _out_mlp(h, flat(main), flat(mem_o), W["w_out"][1], W["norm_mlp"][1], W["w_up"][1], W["w_down"][1], tm)
    return unflat(y)[:, :t_real], new_shift[None], new_wkv[None], k_state, v_state


def kernel(x_prompt, x_sample, state_rwkv_shift, state_rwkv_wkv, cache_swa_k, cache_swa_v, cache_mem_k,
           cache_mem_v, mem_prompt, norm_mix, norm_mlp, w_out, w_up, w_down, mem_norm, w_mem_kv, mem_qnorm,
           mem_knorm, w_in_a, shift_mu, w_w2, w0, w_a2, a0, w_g2, k_k, k_a, r_k, lnx_w, lnx_b, w_in_b,
           swa_qnorm, sinks, kv_norm, w_kv, swa_knorm):
    W = dict(norm_mix=norm_mix, norm_mlp=norm_mlp, w_out=w_out.astype(bf16), w_up=w_up.astype(bf16),
             w_down=w_down.astype(bf16), mem_qnorm=mem_qnorm, w_in_a=w_in_a[0].astype(bf16),
             shift_mu=shift_mu[0], w_w2=w_w2[0], w0=w0[0], w_a2=w_a2[0], a0=a0[0], w_g2=w_g2[0],
             k_k=k_k[0], k_a=k_a[0], r_k=r_k[0].reshape(-1), lnx_w=lnx_w[0], lnx_b=lnx_b[0],
             w_in_b=w_in_b[0].astype(bf16), swa_qnorm=swa_qnorm[0], sinks=sinks[0], kv_norm=kv_norm,
             w_kv=w_kv.astype(bf16), swa_knorm=swa_knorm)
    bp, tp, _ = x_prompt.shape
    bs, ts, _ = x_sample.shape
    depth = norm_mix.shape[0]

    mem_flat = mem_prompt.reshape(bp * N_MEM, D_MODEL)
    p_mem_k, p_mem_v = [], []
    for l in range(depth):
        kv = _norm_matmul(mem_flat, mem_norm[l], w_mem_kv[l].astype(bf16), ROW_TILE)
        mk, mv = _kv_post(kv, mem_knorm[l], N_MEM)
        p_mem_k.append(mk.reshape(bp, N_MEM, MEM_WIDTH))
        p_mem_v.append(mv.reshape(bp, N_MEM, MEM_WIDTH))
    transposed = lambda a: jnp.swapaxes(jnp.stack(a), 2, 3)
    y_p, p_shift, p_wkv, p_k, p_v = _trunk(
        x_prompt, jnp.arange(tp), transposed(p_mem_k), transposed(p_mem_v), jnp.zeros((bp, RWKV_COLS), f32),
        None, None, tp, W)
    mem_heads = lambda a: jnp.stack(a).reshape(depth, bp, N_MEM, MEM_HEADS, HEAD_DIM)
    cache_t = lambda a: jnp.transpose(a, (0, 1, 3, 4, 2)).reshape(depth, bs, MEM_WIDTH, N_MEM)

    tpad = -(-ts // SUBLANES) * SUBLANES
    x_s = jnp.pad(x_sample, ((0, 0), (0, tpad - ts), (0, 0)))
    win = cache_swa_k.shape[1]
    assert win == WINDOW
    swa_t = lambda a: jnp.transpose(a, (0, 2, 3, 1)).reshape(bs, KV_WIDTH, win)
    y_s, s_shift, s_wkv, s_k, s_v = _trunk(
        x_s, PAST_LEN + jnp.arange(tpad), cache_t(cache_mem_k), cache_t(cache_mem_v),
        state_rwkv_shift[0], state_rwkv_wkv[0],
        (swa_t(cache_swa_k), swa_t(cache_swa_v)), ts, W)

    return (y_p, y_s, p_shift, p_wkv, p_k, p_v, mem_heads(p_mem_k), mem_heads(p_mem_v),
            s_shift, s_wkv, s_k, s_v)
```

```python
import functools
import math

import jax
import jax.numpy as jnp
from jax import lax
from jax.experimental import pallas as pl
from jax.experimental.pallas import tpu as pltpu

f32 = jnp.float32
bf16 = jnp.bfloat16

D_MODEL = 1024
HEAD_DIM = 64
MEM_HEADS = 4
MEM_WIDTH = MEM_HEADS * HEAD_DIM
MAIN_WIDTH = D_MODEL - MEM_WIDTH
RWKV_HEADS = MAIN_WIDTH // HEAD_DIM
DECAY_LORA = 64
AAA_LORA = 64
GATE_LORA = 128
RWKV_COLS = 3 * MAIN_WIDTH + DECAY_LORA + AAA_LORA + GATE_LORA
A_IN_COLS = RWKV_COLS + MEM_WIDTH
SWA_Q_HEADS = MAIN_WIDTH // HEAD_DIM
SWA_KV_HEADS = 4
SWA_GROUP = SWA_Q_HEADS // SWA_KV_HEADS
KV_WIDTH = SWA_KV_HEADS * HEAD_DIM
WINDOW = 128
BLOCK = 128
N_MEM = 256
D_FF = 4 * D_MODEL
PAST_LEN = 16384
ROPE_THETA = 10000.0
NORM_EPS = 1e-6
LNX_EPS = 6.4e-4
L2_EPS = 1e-12
ATTN_SCALE = HEAD_DIM ** -0.5

LANES = 128
SUBLANES = 8
HEADS_PER_SLAB = LANES // HEAD_DIM
N_SLABS = MAIN_WIDTH // LANES
CHUNK = 64
WKV_CHUNKS_PER_STEP = 16
VMEM_LIMIT = 56 * 1024 * 1024
NEG_BIG = -1e30
ROW_TILE = 512
PREP_TILE = 256
MEM_Q_TILE = 512
MEM_Q_SUB = 256
STEP_MEM_SEQS = 8
STEP_SWA_SEQS = 4


def _cp(sem, vmem=VMEM_LIMIT):
    return pltpu.CompilerParams(dimension_semantics=sem, vmem_limit_bytes=vmem)


def _dot(a, b):
    return jnp.dot(a, b, preferred_element_type=f32)


def _dot_nt(a, b):
    return lax.dot_general(a, b, (((1,), (1,)), ((), ())), preferred_element_type=f32)


def _dot_tn(a, b):
    return lax.dot_general(a, b, (((0,), (0,)), ((), ())), preferred_element_type=f32)


def _iota(shape, dim):
    return lax.broadcasted_iota(jnp.int32, shape, dim)


def _group_ones():
    r = _iota((LANES, LANES), 0)
    c = _iota((LANES, LANES), 1)
    return jnp.where((r < HEAD_DIM) == (c < HEAD_DIM), 1.0, 0.0).astype(f32)


def _head_sum(x, gmat):
    w = x.shape[-1]
    gb = gmat.astype(bf16)
    hi = x.astype(bf16)
    lo = (x - hi.astype(f32)).astype(bf16)
    parts = [_dot(hi[:, j * LANES:(j + 1) * LANES], gb) + _dot(lo[:, j * LANES:(j + 1) * LANES], gb)
             for j in range(w // LANES)]
    return parts[0] if len(parts) == 1 else jnp.concatenate(parts, axis=-1)


def _cumsum_rows(x):
    rows = x.shape[0]
    row = _iota(x.shape, 0)
    step = 1
    while step < rows:
        x = x + jnp.where(row >= step, pltpu.roll(x, step, 0), 0.0)
        step *= 2
    return x


def _sigmoid(x):
    return 1.0 / (1.0 + jnp.exp(-x))


def _rot_half(x):
    lane = _iota(x.shape, 1)
    first = (lane & (HEAD_DIM - 1)) < (HEAD_DIM // 2)
    return jnp.where(first, pltpu.roll(x, LANES - HEAD_DIM // 2, 1), pltpu.roll(x, HEAD_DIM // 2, 1))


def _norm_matmul_kernel(x_ref, g_ref, w_ref, o_ref):
    x = x_ref[...]
    xn = x * lax.rsqrt(jnp.mean(x * x, axis=-1, keepdims=True) + NORM_EPS) * g_ref[...]
    o_ref[...] = _dot(xn.astype(bf16), w_ref[...])


def _norm_matmul(x, g, w_bf16, tm):
    n, d = x.shape
    m = w_bf16.shape[1]
    tm = min(tm, n)
    return pl.pallas_call(
        _norm_matmul_kernel,
        grid=(n // tm,),
        in_specs=[pl.BlockSpec((tm, d), lambda i: (i, 0)),
                  pl.BlockSpec((1, d), lambda i: (0, 0)),
                  pl.BlockSpec((d, m), lambda i: (0, 0))],
        out_specs=pl.BlockSpec((tm, m), lambda i: (i, 0)),
        out_shape=jax.ShapeDtypeStruct((n, m), f32),
        compiler_params=_cp(("parallel",)),
        name="norm_matmul",
    )(x, g.reshape(1, d), w_bf16)


def _head_norm_rope(x, g, cos, sin, gmat):
    xn = x * lax.rsqrt(_head_sum(x * x, gmat) * (1.0 / HEAD_DIM) + NORM_EPS) * g
    slabs = [xn[:, j * LANES:(j + 1) * LANES] for j in range(x.shape[-1] // LANES)]
    return [xs * cos + _rot_half(xs) * sin for xs in slabs]


def _kv_post_kernel(kv_ref, g_ref, k_ref, v_ref):
    gmat = _group_ones()
    x = kv_ref[:, :KV_WIDTH]
    k_ref[...] = x * lax.rsqrt(_head_sum(x * x, gmat) * (1.0 / HEAD_DIM) + NORM_EPS) * g_ref[...]
    v_ref[...] = kv_ref[:, KV_WIDTH:]


def _kv_post(kv, g_head, tm):
    n = kv.shape[0]
    tm = min(tm, n)
    g = jnp.tile(g_head.reshape(1, HEAD_DIM), (1, KV_WIDTH // HEAD_DIM))
    return pl.pallas_call(
        _kv_post_kernel,
        grid=(n // tm,),
        in_specs=[pl.BlockSpec((tm, 2 * KV_WIDTH), lambda i: (i, 0)),
                  pl.BlockSpec((1, KV_WIDTH), lambda i: (0, 0))],
        out_specs=[pl.BlockSpec((tm, KV_WIDTH), lambda i: (i, 0)),
                   pl.BlockSpec((tm, KV_WIDTH), lambda i: (i, 0))],
        out_shape=[jax.ShapeDtypeStruct((n, KV_WIDTH), f32)] * 2,
        compiler_params=_cp(("parallel",)),
        name="kv_post",
    )(kv, g)


def _time_mix_inputs(pf, prev, consts, outs):
    mu_ref, ww2_ref, w0_ref, wa2_ref, a0_ref, wg2_ref, kk_ref, ka_ref, rk_ref = consts
    r_o, lw_o, k_o, v_o, a_o, b_o, bonus_o, g_o = outs
    gmat = _group_ones()
    ps = pf + (prev - pf) * mu_ref[...]
    i1, i2, i3 = MAIN_WIDTH, 2 * MAIN_WIDTH, 3 * MAIN_WIDTH
    i4, i5 = i3 + DECAY_LORA, i3 + DECAY_LORA + AAA_LORA
    r, k, v = ps[:, :i1], ps[:, i1:i2], ps[:, i2:i3]
    wd, ad, gd = ps[:, i3:i4], ps[:, i4:i5], ps[:, i5:]
    lora = lambda x, w_ref: _dot(x.astype(bf16), w_ref[...].astype(bf16))
    lw = -math.exp(-0.5) * _sigmoid(w0_ref[...] + lora(jnp.tanh(wd), ww2_ref))
    a_sig = _sigmoid(a0_ref[...] + lora(ad, wa2_ref))
    g = lora(_sigmoid(gd), wg2_ref)
    kk = k * kk_ref[...]
    kk = kk / jnp.maximum(jnp.sqrt(_head_sum(kk * kk, gmat)), L2_EPS)
    k2 = k * (1.0 + (a_sig - 1.0) * ka_ref[...])
    r_o[0] = r.astype(r_o.dtype)
    lw_o[0] = lw
    k_o[0] = k2.astype(k_o.dtype)
    v_o[0] = v.astype(v_o.dtype)
    a_o[0] = (-kk).astype(a_o.dtype)
    b_o[0] = (kk * a_sig).astype(b_o.dtype)
    bonus_o[0] = (_head_sum(r * k2 * rk_ref[...], gmat) * v).astype(bonus_o.dtype)
    g_o[0] = g.astype(g_o.dtype)


def _rwkv_prep_kernel(p_ref, shift_ref, *refs):
    pf = p_ref[0]
    prev = jnp.concatenate([shift_ref[0], pf[:pf.shape[0] - shift_ref.shape[1]]], axis=0)
    _time_mix_inputs(pf, prev, refs[:9], refs[9:])


def _proj_prep_kernel(x_ref, gx_ref, w_ref, shift_ref, *refs):
    consts, outs, (q_o, last_o, carry_ref) = refs[:9], refs[9:17], refs[17:]
    i = pl.program_id(1)
    x = x_ref[0]
    xn = x * lax.rsqrt(jnp.mean(x * x, axis=-1, keepdims=True) + NORM_EPS) * gx_ref[...]
    proj = _dot(xn.astype(bf16), w_ref[...])
    q_o[0] = proj[:, RWKV_COLS:]
    pf = proj[:, :RWKV_COLS]
    first_prev = jnp.where(i == 0, shift_ref[0], carry_ref[...])
    prev = jnp.where(_iota(pf.shape, 0) == 0, first_prev, pltpu.roll(pf, 1, 0))
    last = pf[pf.shape[0] - 1:, :]
    carry_ref[...] = last
    last_o[0] = last
    _time_mix_inputs(pf, prev, consts, outs)


def _proj_prep(x, g_norm, w_in_bf16, shift_prev, prep_w, tt, seq_dtype):
    bsz, t, _ = x.shape
    row = lambda a: a.reshape(1, -1)
    full = lambda a: pl.BlockSpec(a.shape, lambda b, i: (0,) * a.ndim)
    mu, w_w2, w0, w_a2, a0, w_g2, k_k, k_a, r_k = prep_w
    consts = [row(mu), w_w2, row(w0), w_a2, row(a0), w_g2, row(k_k), row(k_a), row(r_k)]
    tile = lambda w: pl.BlockSpec((1, tt, w), lambda b, i: (b, i, 0))
    per_seq = pl.BlockSpec((1, 1, RWKV_COLS), lambda b, i: (b, 0, 0))
    res = pl.pallas_call(
        _proj_prep_kernel,
        grid=(bsz, t // tt),
        in_specs=[tile(D_MODEL), full(row(g_norm)), full(w_in_bf16), per_seq] + [full(c) for c in consts],
        out_specs=[tile(MAIN_WIDTH)] * 8 + [tile(MEM_WIDTH), per_seq],
        out_shape=[jax.ShapeDtypeStruct((bsz, t, MAIN_WIDTH), f32 if j == 1 else seq_dtype) for j in range(8)]
                  + [jax.ShapeDtypeStruct((bsz, t, MEM_WIDTH), f32), jax.ShapeDtypeStruct((bsz, 1, RWKV_COLS), f32)],
        scratch_shapes=[pltpu.VMEM((1, RWKV_COLS), f32)],
        compiler_params=_cp(("parallel", "arbitrary")),
        name="proj_prep",
    )(x, row(g_norm), w_in_bf16, shift_prev.reshape(bsz, 1, RWKV_COLS), *consts)
    return res[:8], res[8], res[9].reshape(bsz, RWKV_COLS)


def _rwkv_prep(steps, shift_prev, prep_w):
    _, t, _ = steps.shape
    row = lambda a: a.reshape(1, -1)
    full = lambda a: pl.BlockSpec(a.shape, lambda i: (0,) * a.ndim)
    mu, w_w2, w0, w_a2, a0, w_g2, k_k, k_a, r_k = prep_w
    consts = [row(mu), w_w2, row(w0), w_a2, row(a0), w_g2, row(k_k), row(k_a), row(r_k)]
    shift_rows = shift_prev[None]
    return pl.pallas_call(
        _rwkv_prep_kernel,
        grid=(1,),
        in_specs=[full(steps), full(shift_rows)] + [full(c) for c in consts],
        out_specs=[pl.BlockSpec((1, t, MAIN_WIDTH), lambda i: (0, 0, 0))] * 8,
        out_shape=[jax.ShapeDtypeStruct((1, t, MAIN_WIDTH), f32)] * 8,
        compiler_params=_cp(("arbitrary",)),
        name="rwkv_prep",
    )(steps, shift_rows, *consts)


def _wkv_chunk_kernel(r_ref, lw_ref, k_ref, v_ref, a_ref, b_ref, bonus_ref, g_ref, lnw_ref, lnb_ref,
                      o_ref, s_ref, h_ref, qt_ref, yv_ref, pp_ref, hv_ref, tc_ref):
    i = pl.program_id(2)
    c = CHUNK
    nc = r_ref.shape[1] // c
    n = range(nc)

    @pl.when(i == 0)
    def _():
        h_ref[...] = jnp.zeros_like(h_ref)
        qt_ref[...] = jnp.zeros_like(qt_ref)
        yv_ref[...] = jnp.zeros_like(yv_ref)
        pp_ref[...] = jnp.zeros_like(pp_ref)
        hv_ref[...] = jnp.zeros_like(hv_ref)
        tc_ref[...] = jnp.zeros_like(tc_ref)

    gmat = _group_ones()
    bd = gmat > 0.5
    row = _iota((LANES, LANES), 0)
    col = _iota((LANES, LANES), 1)
    t_row = row & (c - 1)
    s_col = col & (c - 1)
    m_strict = s_col < t_row
    m_incl = s_col <= t_row
    top = row < c
    lane_lo = _iota((1, LANES), 1) < HEAD_DIM
    eye = row == col
    zeros = jnp.zeros((c, LANES), f32)
    mm = lambda x: x.astype(bf16)
    gb = gmat.astype(bf16)
    sls = [slice(j * c, (j + 1) * c) for j in n]

    def pending_state():
        state = {"h": h_ref[...], "ys": []}

        def state_step(j):
            hb = mm(state["h"])
            state["ys"].append(_dot(qt_ref[j], hb) + yv_ref[j])
            state["h"] = state["h"] * tc_ref[j] + _dot(pp_ref[j], hb) + hv_ref[j]

        def finish():
            h_ref[...] = state["h"]
            means = [_dot(mm(y), gb) * (1.0 / HEAD_DIM) for y in state["ys"]]
            ycs = [state["ys"][j] - means[j] for j in n]
            vrs = [_dot(mm(yc * yc), gb) * (1.0 / HEAD_DIM) for yc in ycs]
            outs = []
            for j in n:
                yn = ycs[j] * lax.rsqrt(vrs[j] + LNX_EPS) * lnw_ref[...] + lnb_ref[...]
                outs.append(((yn + bonus_ref[0, sls[j], :].astype(f32))
                             * g_ref[0, sls[j], :].astype(f32)).astype(o_ref.dtype))
            return outs

        return state_step, finish

    def chunk_group(js):
        st = {}

        def decay_and_blocks():
            st["vs"], st["tots"], st["ats"], st["rts"], st["bkts"], gms = [], [], [], [], [], []
            for j in js:
                sl = sls[j]
                lw = lw_ref[0, sl, :]
                cum = _cumsum_rows(lw)
                tot = cum[c - 1:c, :]
                b, k = b_ref[0, sl, :].astype(f32), k_ref[0, sl, :].astype(f32)
                e_neg, e_rem = jnp.exp(-cum), jnp.exp(tot - cum)
                at = a_ref[0, sl, :].astype(f32) * jnp.exp(cum - lw)
                rt = r_ref[0, sl, :].astype(f32) * jnp.exp(cum)
                bk = mm(jnp.concatenate([b * e_neg, k * e_neg], axis=0))
                lhs = mm(jnp.concatenate([jnp.where(lane_lo, at, 0.0), jnp.where(lane_lo, 0.0, at),
                                          jnp.where(lane_lo, rt, 0.0), jnp.where(lane_lo, 0.0, rt)], axis=0))
                gms.append(_dot_nt(lhs, bk))
                st["vs"].append(v_ref[0, sl, :].astype(f32))
                st["tots"].append(tot)
                st["ats"].append(at)
                st["rts"].append(rt)
                st["bkts"].append(mm(jnp.concatenate([b * e_rem, k * e_rem], axis=0)))
            st["gms"] = gms

        def first_products():
            m = range(len(js))
            aaks = [jnp.where(m_strict, gm[:LANES], 0.0) for gm in st["gms"]]
            st["arks"] = [mm(jnp.where(m_incl, gm[LANES:], 0.0)) for gm in st["gms"]]
            aak_sws = [pltpu.roll(aak, HEAD_DIM, 1) for aak in aaks]
            st["ms"] = [jnp.where(bd, jnp.where(top, aaks[q], aak_sws[q]), 0.0) for q in m]
            aks = [mm(jnp.where(bd, jnp.where(top, aak_sws[q], aaks[q]), 0.0)) for q in m]
            vvs = [mm(jnp.concatenate([pltpu.roll(v, HEAD_DIM, 1)] * 2, axis=0)) for v in st["vs"]]
            akvs = [_dot(aks[q], vvs[q]) for q in m]
            st["xs"] = [jnp.where(bd, jnp.concatenate([st["ats"][q]] * 2, axis=0), akvs[q]) for q in m]

        def square(last):
            def stage():
                m = range(len(js))
                ms, xs = st["ms"], st["xs"]
                if not last:
                    zs = [_dot(mm(ms[q]), mm(jnp.concatenate([xs[q], ms[q]], axis=1))) for q in m]
                    st["xs"] = [xs[q] + zs[q][:, :LANES] for q in m]
                    st["ms"] = [zs[q][:, LANES:] for q in m]
                else:
                    zs = [_dot(mm(ms[q]), mm(xs[q])) for q in m]
                    st["xs"] = [xs[q] + zs[q] for q in m]
            return stage

        def last_products():
            m = range(len(js))
            rhss = []
            for q in m:
                x = st["xs"][q]
                ta = jnp.where(lane_lo, x[:c], x[c:])
                uv = pltpu.roll(jnp.where(lane_lo, x[c:], x[:c]), HEAD_DIM, 1)
                rhss.append(mm(jnp.concatenate([jnp.concatenate([ta, uv], axis=1),
                                                jnp.concatenate([zeros, st["vs"][q]], axis=1)], axis=0)))
            st["z2s"] = [_dot(st["arks"][q], rhss[q]) for q in m]
            st["pzs"] = [_dot_tn(st["bkts"][q], rhss[q]) for q in m]

        def make_pending():
            for q, j in enumerate(js):
                z2, pz = st["z2s"][q], st["pzs"][q]
                qt_ref[j] = mm(st["rts"][q] + jnp.where(lane_lo, z2[:c, :LANES], z2[c:, :LANES]))
                yv_ref[j] = jnp.where(lane_lo, z2[:c, LANES:], z2[c:, LANES:])
                pp_ref[j] = mm(jnp.where(bd, pz[:, :LANES], 0.0))
                hv_ref[j] = jnp.where(bd, pz[:, LANES:], 0.0)
                tc_ref[j] = jnp.sum(jnp.where(eye, jnp.exp(st["tots"][q]), 0.0), axis=1, keepdims=True)

        return ([decay_and_blocks, first_products] + [square(False)] * 5 + [square(True), last_products], make_pending)

    @pl.when(i < pl.num_programs(2) - 1)
    def _():
        half = nc // 2
        (stages_a, pend_a), (stages_b, pend_b) = chunk_group(list(range(half))), chunk_group(list(range(half, nc)))
        lag = 4
        state_step, finish = pending_state()
        outs = None
        for slot in range(max(len(stages_a) + lag, nc + 1)):
            if slot < nc:
                state_step(slot)
            if slot == nc:
                outs = finish()
            if slot < len(stages_a):
                stages_a[slot]()
            if 0 <= slot - lag < len(stages_b):
                stages_b[slot - lag]()
        for j in n:
            o_ref[0, sls[j], :] = outs[j]
        pend_a()
        pend_b()

    @pl.when(i == pl.num_programs(2) - 1)
    def _():
        state_step, finish = pending_state()
        for j in n:
            state_step(j)
        outs = finish()
        for j in n:
            o_ref[0, sls[j], :] = outs[j]

    @pl.when(i == pl.num_programs(2) - 1)
    def _():
        s_ref[0, 0] = h_ref[...].T


def _wkv_chunked(r, lw, k, v, a, b, bonus, g, lnx_w, lnx_b):
    bsz, t, _ = r.shape
    nc = WKV_CHUNKS_PER_STEP
    tt = nc * CHUNK
    nt = t // tt
    assert t % tt == 0
    cur = pl.BlockSpec((1, tt, LANES), lambda bb, hp, i: (bb, jnp.minimum(i, nt - 1), hp))
    pend = pl.BlockSpec((1, tt, LANES), lambda bb, hp, i: (bb, jnp.maximum(i - 1, 0), hp))
    vec = pl.BlockSpec((1, LANES), lambda bb, hp, i: (0, hp))
    out, st = pl.pallas_call(
        _wkv_chunk_kernel,
        grid=(bsz, N_SLABS, nt + 1),
        in_specs=[cur] * 6 + [pend, pend, vec, vec],
        out_specs=[pend, pl.BlockSpec((1, 1, LANES, LANES), lambda bb, hp, i: (bb, hp, 0, 0))],
        out_shape=[jax.ShapeDtypeStruct((bsz, t, MAIN_WIDTH), bf16),
                   jax.ShapeDtypeStruct((bsz, N_SLABS, LANES, LANES), f32)],
        scratch_shapes=[pltpu.VMEM((LANES, LANES), f32),
                        pltpu.VMEM((nc, CHUNK, LANES), bf16), pltpu.VMEM((nc, CHUNK, LANES), f32),
                        pltpu.VMEM((nc, LANES, LANES), bf16), pltpu.VMEM((nc, LANES, LANES), f32),
                        pltpu.VMEM((nc, LANES, 1), f32)],
        compiler_params=_cp(("parallel", "parallel", "arbitrary")),
        name="wkv_chunked",
    )(r, lw, k, v, a, b, bonus, g, lnx_w.reshape(1, -1), lnx_b.reshape(1, -1))
    st = jnp.stack([st[:, :, :HEAD_DIM, :HEAD_DIM], st[:, :, HEAD_DIM:, HEAD_DIM:]], axis=2)
    return out, st.reshape(bsz, RWKV_HEADS, HEAD_DIM, HEAD_DIM)


def _wkv_steps_kernel(s_ref, seq_ref, lnw_ref, lnb_ref, o_ref, so_ref, w_ref, y_ref, *, n_steps):
    i_r, i_lw, i_k, i_v, i_a, i_b, i_bonus, i_g = range(8)
    w_ref[...] = jnp.exp(seq_ref[i_lw])
    sub = _iota((SUBLANES, LANES), 0)

    def group(vg, carry):
        v0 = pl.multiple_of(vg * SUBLANES, SUBLANES)
        v_rows = [seq_ref[i_v, t, pl.ds(v0, SUBLANES), :] for t in range(n_steps)]
        ys = [jnp.zeros((SUBLANES, LANES), f32) for _ in range(n_steps)]
        for j in range(SUBLANES):
            sv = s_ref[0, v0 + j]
            for t in range(n_steps):
                sa = jnp.sum(sv * seq_ref[i_a, t], axis=0, keepdims=True)
                sv = sv * w_ref[t] + sa * seq_ref[i_b, t] + v_rows[t][j:j + 1, :] * seq_ref[i_k, t]
                y = jnp.sum(sv * seq_ref[i_r, t], axis=0, keepdims=True)
                ys[t] = jnp.where(sub == j, y, ys[t])
            so_ref[0, v0 + j] = sv
        for t in range(n_steps):
            y_ref[t, pl.ds(v0, SUBLANES), :] = ys[t]
        return carry

    lax.fori_loop(0, HEAD_DIM // SUBLANES, group, 0)
    y = y_ref[...]
    mean = jnp.mean(y, axis=1, keepdims=True)
    yc = y - mean
    var = jnp.mean(yc * yc, axis=1, keepdims=True)
    yn = yc * lax.rsqrt(var + LNX_EPS) * lnw_ref[...] + lnb_ref[...]
    o_ref[...] = (yn + seq_ref[i_bonus]) * seq_ref[i_g]


def _wkv_steps(state, seqs, lnx_w, lnx_b, tp):
    n_steps, bsz, _ = seqs[0].shape
    assert bsz % LANES == 0
    state_t = jnp.transpose(state, (1, 2, 3, 0))
    seq_t = jnp.transpose(jnp.stack(seqs), (0, 1, 3, 2))
    lanes = lambda x: jnp.broadcast_to(x.reshape(MAIN_WIDTH, 1), (MAIN_WIDTH, LANES))
    sspec = pl.BlockSpec((1, HEAD_DIM, HEAD_DIM, LANES), lambda h, bi: (h, 0, 0, bi))
    vec = pl.BlockSpec((HEAD_DIM, LANES), lambda h, bi: (h, 0))
    ospec = pl.BlockSpec((n_steps, HEAD_DIM, LANES), lambda h, bi: (0, h, bi))
    out, new_state = pl.pallas_call(
        functools.partial(_wkv_steps_kernel, n_steps=n_steps),
        grid=(RWKV_HEADS, bsz // LANES),
        in_specs=[sspec, pl.BlockSpec((8, n_steps, HEAD_DIM, LANES), lambda h, bi: (0, 0, h, bi)), vec, vec],
        out_specs=[ospec, sspec],
        out_shape=[jax.ShapeDtypeStruct((n_steps, MAIN_WIDTH, bsz), f32),
                   jax.ShapeDtypeStruct(state_t.shape, f32)],
        scratch_shapes=[pltpu.VMEM((n_steps, HEAD_DIM, LANES), f32), pltpu.VMEM((n_steps, HEAD_DIM, LANES), f32)],
        compiler_params=_cp(("parallel", "parallel")),
        name="wkv_steps",
    )(state_t, seq_t, lanes(lnx_w), lanes(lnx_b))
    out = jnp.pad(jnp.transpose(out, (2, 0, 1)), ((0, 0), (0, tp - n_steps), (0, 0)))
    return out, jnp.transpose(new_state, (3, 0, 1, 2))


def _mem_attn_kernel(q_ref, g_ref, kt_ref, vt_ref, o_ref, *, sub):
    gmat = _group_ones()
    bb, tq, _ = q_ref.shape
    head = _iota((1, MEM_WIDTH), 1) // HEAD_DIM
    tiles = [(b, pl.ds(r * sub, sub)) for b in range(bb) for r in range(tq // sub)]
    qs = []
    for b, rows in tiles:
        q = q_ref[b, rows, :]
        qn = q * lax.rsqrt(_head_sum(q * q, gmat) * (1.0 / HEAD_DIM) + NORM_EPS) * g_ref[...]
        qs.append(jnp.concatenate([jnp.where(head == h, qn, 0.0) for h in range(MEM_HEADS)], axis=0).astype(bf16))
    ss = [_dot(q4, kt_ref[0, b].astype(bf16)) * ATTN_SCALE for q4, (b, _) in zip(qs, tiles)]
    es = [jnp.exp(s - jnp.max(s, axis=-1, keepdims=True)) for s in ss]
    o4s = [_dot_nt(e.astype(bf16), vt_ref[0, b].astype(bf16)) / jnp.sum(e, axis=-1, keepdims=True)
           for e, (b, _) in zip(es, tiles)]
    for o4, (b, rows) in zip(o4s, tiles):
        o = jnp.zeros((sub, MEM_WIDTH), f32)
        for h in range(MEM_HEADS):
            o = o + jnp.where(head == h, o4[h * sub:(h + 1) * sub], 0.0)
        o_ref[b, rows, :] = o.astype(o_ref.dtype)


def _mem_attn(proj, col_block, g_qnorm, mem_kt, mem_vt, layer, tq, bb, out_dtype):
    bsz, t, _ = proj.shape
    tq = min(tq, t)
    g = jnp.tile(g_qnorm.reshape(1, HEAD_DIM), (1, MEM_HEADS))
    kv_spec = pl.BlockSpec((1, bb, MEM_WIDTH, N_MEM), lambda b, i: (layer, b, 0, 0))
    return pl.pallas_call(
        functools.partial(_mem_attn_kernel, sub=min(tq, MEM_Q_SUB)),
        grid=(bsz // bb, t // tq),
        in_specs=[pl.BlockSpec((bb, tq, MEM_WIDTH), lambda b, i: (b, i, col_block)),
                  pl.BlockSpec((1, MEM_WIDTH), lambda b, i: (0, 0)),
                  kv_spec, kv_spec],
        out_specs=pl.BlockSpec((bb, tq, MEM_WIDTH), lambda b, i: (b, i, 0)),
        out_shape=jax.ShapeDtypeStruct((bsz, t, MEM_WIDTH), out_dtype),
        compiler_params=_cp(("parallel", "parallel")),
        name="mem_attn",
    )(proj, g, mem_kt, mem_vt)


def _swa_kernel(sink_ref, q_ref, g_ref, cos_ref, sin_ref, kp_ref, kc_ref, vp_ref, vc_ref, o_ref, qn_ref):
    i = pl.program_id(0)
    gmat = _group_ones()
    bb, tq, _ = q_ref.shape
    keys, vals = [], []
    for b in range(bb):
        for j in range(N_SLABS):
            sl = slice(j * LANES, (j + 1) * LANES)
            x = q_ref[b, :, sl]
            xn = x * lax.rsqrt(_head_sum(x * x, gmat) * (1.0 / HEAD_DIM) + NORM_EPS) * g_ref[...]
            qn_ref[b, :, sl] = xn * cos_ref[...] + _rot_half(xn) * sin_ref[...]
        keys.append(jnp.concatenate([kp_ref[b], kc_ref[b]], axis=0).astype(bf16))
        vals.append(jnp.concatenate([vp_ref[b], vc_ref[b]], axis=0).astype(bf16))
    rows = SWA_GROUP * tq
    row = _iota((rows, 2 * BLOCK), 0) & (tq - 1)
    col = _iota((rows, 2 * BLOCK), 1)
    mask = (col > row) & (col <= row + WINDOW) & ((i > 0) | (col >= BLOCK))
    grp = _iota((rows, 1), 0) // tq
    pairs = [(b, hk) for b in range(bb) for hk in range(SWA_KV_HEADS)]
    hs = [slice(hk * HEAD_DIM, (hk + 1) * HEAD_DIM) for hk in range(SWA_KV_HEADS)]
    q3s = [jnp.concatenate([qn_ref[b, :, (hk * SWA_GROUP + gq) * HEAD_DIM:(hk * SWA_GROUP + gq + 1) * HEAD_DIM]
                            for gq in range(SWA_GROUP)], axis=0).astype(bf16) for b, hk in pairs]
    ss = [jnp.where(mask, _dot_nt(q3, keys[b][:, hs[hk]]) * ATTN_SCALE, NEG_BIG) for q3, (b, hk) in zip(q3s, pairs)]
    sinks = [jnp.where(grp == 0, sink_ref[hk * SWA_GROUP],
                       jnp.where(grp == 1, sink_ref[hk * SWA_GROUP + 1], sink_ref[hk * SWA_GROUP + 2]))
             for hk in range(SWA_KV_HEADS)]
    ms = [jnp.maximum(jnp.max(s, axis=-1, keepdims=True), sinks[hk]) for s, (b, hk) in zip(ss, pairs)]
    es = [jnp.exp(s - m) for s, m in zip(ss, ms)]
    denoms = [jnp.sum(e, axis=-1, keepdims=True) + jnp.exp(sinks[hk] - m) for e, m, (b, hk) in zip(es, ms, pairs)]
    outs = [_dot(e.astype(bf16), vals[b][:, hs[hk]]) / d for e, d, (b, hk) in zip(es, denoms, pairs)]
    for o, (b, hk) in zip(outs, pairs):
        for gq in range(SWA_GROUP):
            hq = hk * SWA_GROUP + gq
            o_ref[b, :, hq * HEAD_DIM:(hq + 1) * HEAD_DIM] = o[gq * tq:(gq + 1) * tq].astype(o_ref.dtype)


def _swa(proj, g_qnorm, sinks, cos, sin, k, v):
    bsz, t, _ = proj.shape
    assert SWA_GROUP == 3
    tq = BLOCK
    g = jnp.tile(g_qnorm.reshape(1, HEAD_DIM), (1, HEADS_PER_SLAB))
    prev = pl.BlockSpec((bsz, BLOCK, KV_WIDTH), lambda i: (0, jnp.maximum(i - 1, 0), 0))
    cur = pl.BlockSpec((bsz, BLOCK, KV_WIDTH), lambda i: (0, i, 0))
    return pl.pallas_call(
        _swa_kernel,
        grid=(t // tq,),
        in_specs=[pl.BlockSpec(memory_space=pltpu.SMEM),
                  pl.BlockSpec((bsz, tq, MAIN_WIDTH), lambda i: (0, i, 0)),
                  pl.BlockSpec((1, LANES), lambda i: (0, 0)),
                  pl.BlockSpec((tq, LANES), lambda i: (i, 0)),
                  pl.BlockSpec((tq, LANES), lambda i: (i, 0)),
                  prev, cur, prev, cur],
        out_specs=pl.BlockSpec((bsz, tq, MAIN_WIDTH), lambda i: (0, i, 0)),
        out_shape=jax.ShapeDtypeStruct((bsz, t, MAIN_WIDTH), bf16),
        scratch_shapes=[pltpu.VMEM((bsz, tq, MAIN_WIDTH), f32)],
        compiler_params=_cp(("parallel",)),
        name="swa_attn",
    )(sinks, proj, g, cos, sin, k, k, v, v)


def _swa_step_kernel(sink_ref, q_ref, g_ref, cos_ref, sin_ref, kt_ref, vt_ref, kn_ref, vn_ref,
                     o_ref, kto_ref, vto_ref, qn_ref, *, n_new):
    gmat = _group_ones()
    bb, tq, _ = q_ref.shape
    row = _iota((SWA_GROUP * tq, 2 * BLOCK), 0) & (tq - 1)
    col = _iota((SWA_GROUP * tq, 2 * BLOCK), 1)
    mask = (col > row) & (col <= row + WINDOW)
    grp = _iota((SWA_GROUP * tq, 1), 0) // tq
    keep = _iota((KV_WIDTH, BLOCK), 1) < BLOCK - n_new
    pad = jnp.zeros((BLOCK - tq, KV_WIDTH), f32)
    ktb, vtb, knb, vnb = [], [], [], []
    for b in range(bb):
        for j in range(N_SLABS):
            sl = slice(j * LANES, (j + 1) * LANES)
            x = q_ref[b, :, sl]
            xn = x * lax.rsqrt(_head_sum(x * x, gmat) * (1.0 / HEAD_DIM) + NORM_EPS) * g_ref[...]
            qn_ref[b, :, sl] = xn * cos_ref[...] + _rot_half(xn) * sin_ref[...]
        kt, vt = kt_ref[b], vt_ref[b]
        kn = jnp.concatenate([kn_ref[b], pad], axis=0)
        vn = jnp.concatenate([vn_ref[b], pad], axis=0)
        kto_ref[b] = jnp.where(keep, pltpu.roll(kt, BLOCK - n_new, 1), pltpu.roll(kn.T, BLOCK - n_new, 1))
        vto_ref[b] = jnp.where(keep, pltpu.roll(vt, BLOCK - n_new, 1), pltpu.roll(vn.T, BLOCK - n_new, 1))
        ktb.append(kt.astype(bf16))
        vtb.append(vt.astype(bf16))
        knb.append(kn.astype(bf16))
        vnb.append(vn.astype(bf16))
    pairs = [(b, hk) for b in range(bb) for hk in range(SWA_KV_HEADS)]
    hs = [slice(hk * HEAD_DIM, (hk + 1) * HEAD_DIM) for hk in range(SWA_KV_HEADS)]
    q3s = [jnp.concatenate([qn_ref[b, :, (hk * SWA_GROUP + gq) * HEAD_DIM:(hk * SWA_GROUP + gq + 1) * HEAD_DIM]
                            for gq in range(SWA_GROUP)], axis=0).astype(bf16) for b, hk in pairs]
    ss = [jnp.where(mask, jnp.concatenate([_dot(q3, ktb[b][hs[hk], :]), _dot_nt(q3, knb[b][:, hs[hk]])], axis=1)
                    * ATTN_SCALE, NEG_BIG) for q3, (b, hk) in zip(q3s, pairs)]
    sinks = [jnp.where(grp == 0, sink_ref[hk * SWA_GROUP],
                       jnp.where(grp == 1, sink_ref[hk * SWA_GROUP + 1], sink_ref[hk * SWA_GROUP + 2]))
             for hk in range(SWA_KV_HEADS)]
    ms = [jnp.maximum(jnp.max(s, axis=-1, keepdims=True), sinks[hk]) for s, (b, hk) in zip(ss, pairs)]
    es = [jnp.exp(s - m) for s, m in zip(ss, ms)]
    denoms = [jnp.sum(e, axis=-1, keepdims=True) + jnp.exp(sinks[hk] - m) for e, m, (b, hk) in zip(es, ms, pairs)]
    ebs = [e.astype(bf16) for e in es]
    outs = [(_dot_nt(eb[:, :BLOCK], vtb[b][hs[hk], :]) + _dot(eb[:, BLOCK:], vnb[b][:, hs[hk]])) / d
            for eb, d, (b, hk) in zip(ebs, denoms, pairs)]
    for o, (b, hk) in zip(outs, pairs):
        for gq in range(SWA_GROUP):
            hq = hk * SWA_GROUP + gq
            o_ref[b, :, hq * HEAD_DIM:(hq + 1) * HEAD_DIM] = o[gq * tq:(gq + 1) * tq]


def _swa_step(proj, g_qnorm, sinks, cos, sin, cache_kt, cache_vt, k_new, v_new, n_new, bb):
    bsz, tq, _ = proj.shape
    assert SWA_GROUP == 3 and tq & (tq - 1) == 0
    g = jnp.tile(g_qnorm.reshape(1, HEAD_DIM), (1, HEADS_PER_SLAB))
    cache_spec = pl.BlockSpec((bb, KV_WIDTH, BLOCK), lambda b: (b, 0, 0))
    new_spec = pl.BlockSpec((bb, tq, KV_WIDTH), lambda b: (b, 0, 0))
    tab_spec = pl.BlockSpec((tq, LANES), lambda b: (0, 0))
    return pl.pallas_call(
        functools.partial(_swa_step_kernel, n_new=n_new),
        grid=(bsz // bb,),
        in_specs=[pl.BlockSpec(memory_space=pltpu.SMEM),
                  pl.BlockSpec((bb, tq, MAIN_WIDTH), lambda b: (b, 0, 0)),
                  pl.BlockSpec((1, LANES), lambda b: (0, 0)), tab_spec, tab_spec,
                  cache_spec, cache_spec, new_spec, new_spec],
        out_specs=[pl.BlockSpec((bb, tq, MAIN_WIDTH), lambda b: (b, 0, 0)), cache_spec, cache_spec],
        out_shape=[jax.ShapeDtypeStruct((bsz, tq, MAIN_WIDTH), f32),
                   jax.ShapeDtypeStruct(cache_kt.shape, f32), jax.ShapeDtypeStruct(cache_vt.shape, f32)],
        scratch_shapes=[pltpu.VMEM((bb, tq, MAIN_WIDTH), f32)],
        compiler_params=_cp(("parallel",)),
        name="swa_step",
    )(sinks, proj, g, cos, sin, cache_kt, cache_vt, k_new, v_new)


def _out_mlp_kernel(*refs, tf, with_next):
    if with_next:
        (h_ref, main_ref, mem_ref, woa_ref, wob_ref, g_ref, wup_ref, wdn_ref,
         gkv_ref, wkv_ref, gk_ref, cos_ref, sin_ref, gin_ref, win_ref, o_ref, k_ref, v_ref, p_ref) = refs
    else:
        h_ref, main_ref, mem_ref, woa_ref, wob_ref, g_ref, wup_ref, wdn_ref, o_ref = refs
    h1 = (h_ref[...] + _dot(main_ref[...].astype(bf16), woa_ref[...])
          + _dot(mem_ref[...].astype(bf16), wob_ref[...]))
    hn = (h1 * lax.rsqrt(jnp.mean(h1 * h1, axis=-1, keepdims=True) + NORM_EPS) * g_ref[...]).astype(bf16)
    acc = None
    for j in range(D_FF // tf):
        u = _dot(hn, wup_ref[:, j * tf:(j + 1) * tf])
        u = jnp.square(jnp.maximum(u, 0.0)).astype(bf16)
        d = _dot(u, wdn_ref[j * tf:(j + 1) * tf, :])
        acc = d if acc is None else acc + d
    h2 = h1 + acc
    o_ref[...] = h2
    if with_next:
        xhat = h2 * lax.rsqrt(jnp.mean(h2 * h2, axis=-1, keepdims=True) + NORM_EPS)
        kv = _dot((xhat * gkv_ref[...]).astype(bf16), wkv_ref[...])
        slabs = _head_norm_rope(kv[:, :KV_WIDTH], gk_ref[...], cos_ref[...], sin_ref[...], _group_ones())
        for j, slab in enumerate(slabs):
            k_ref[:, j * LANES:(j + 1) * LANES] = slab
        v_ref[...] = kv[:, KV_WIDTH:]
        p_ref[...] = _dot((xhat * gin_ref[...]).astype(bf16), win_ref[...])


def _out_mlp(h, main, mem_o, w_out_bf16, g_mlp, w_up_bf16, w_down_bf16, tm, tf=1024, nxt=None):
    n = h.shape[0]
    tm = min(tm, n)
    rows = lambda w: pl.BlockSpec((tm, w), lambda i: (i, 0))
    resident = lambda a: pl.BlockSpec(a.shape, lambda i: (0, 0))
    woa, wob = w_out_bf16[:MAIN_WIDTH], w_out_bf16[MAIN_WIDTH:]
    row = lambda x: x.reshape(1, -1)
    args = [h, main, mem_o, woa, wob, row(g_mlp), w_up_bf16, w_down_bf16]
    in_specs = [rows(D_MODEL), rows(MAIN_WIDTH), rows(MEM_WIDTH)] + [resident(a) for a in args[3:]]
    out_specs, out_shape = [rows(D_MODEL)], [jax.ShapeDtypeStruct((n, D_MODEL), f32)]
    if nxt is not None:
        kv_norm, w_kv, k_norm, cos, sin, in_norm, w_in = nxt
        nt = cos.shape[0] // tm
        table = pl.BlockSpec((tm, LANES), lambda i: (i % nt, 0))
        extra = [row(kv_norm), w_kv, jnp.tile(row(k_norm), (1, KV_WIDTH // HEAD_DIM)), cos, sin, row(in_norm), w_in]
        args += extra
        in_specs += [resident(extra[0]), resident(w_kv), resident(extra[2]), table, table,
                     resident(extra[5]), resident(w_in)]
        out_specs += [rows(KV_WIDTH), rows(KV_WIDTH), rows(D_MODEL)]
        out_shape += [jax.ShapeDtypeStruct((n, KV_WIDTH), f32)] * 2 + [jax.ShapeDtypeStruct((n, D_MODEL), f32)]
    res = pl.pallas_call(
        functools.partial(_out_mlp_kernel, tf=tf, with_next=nxt is not None),
        grid=(n // tm,),
        in_specs=in_specs,
        out_specs=out_specs,
        out_shape=out_shape,
        compiler_params=_cp(("parallel",)),
        name="out_mlp",
    )(*args)
    return res if nxt is not None else res[0]


def _rope_tables(pos, reps):
    half = HEAD_DIM // 2
    freqs = jnp.power(ROPE_THETA, -jnp.arange(half, dtype=f32) / half)
    ang = pos.astype(f32)[:, None] * freqs[None, :]
    cos, sin = jnp.cos(ang), jnp.sin(ang)
    cos_h = jnp.concatenate([cos, cos], axis=-1)
    sin_h = jnp.concatenate([-sin, sin], axis=-1)
    return jnp.tile(cos_h, (1, reps)), jnp.tile(sin_h, (1, reps))


def _trunk(x, pos, mem_kt, mem_vt, shift0, wkv0, swa_cache, t_real, W):
    bsz, tp, _ = x.shape
    n = bsz * tp
    prompt = swa_cache is None
    mem_bb = bsz if prompt else STEP_MEM_SEQS
    act_dtype = bf16 if prompt else f32
    tm = ROW_TILE
    flat = lambda a: a.reshape(n, a.shape[-1])
    unflat = lambda a: a.reshape(bsz, tp, a.shape[-1])
    cos2, sin2 = _rope_tables(pos, HEADS_PER_SLAB)

    prep_w = (W["shift_mu"], W["w_w2"], W["w0"], W["w_a2"], W["a0"], W["w_g2"], W["k_k"], W["k_a"], W["r_k"])
    if prompt:
        seqs, q_mem, new_shift = _proj_prep(x, W["norm_mix"][0], W["w_in_a"], shift0, prep_w, PREP_TILE, bf16)
        q_col = 0
        main, new_wkv = _wkv_chunked(*seqs, W["lnx_w"], W["lnx_b"])
    else:
        q_mem = unflat(_norm_matmul(flat(x), W["norm_mix"][0], W["w_in_a"], tm))
        q_col = RWKV_COLS // MEM_WIDTH
        new_shift = q_mem[:, t_real - 1, :RWKV_COLS]
        steps = jnp.transpose(q_mem[:, :t_real, :RWKV_COLS], (1, 0, 2)).reshape(1, t_real * bsz, RWKV_COLS)
        seqs = [s.reshape(t_real, bsz, MAIN_WIDTH) for s in _rwkv_prep(steps, shift0, prep_w)]
        main, new_wkv = _wkv_steps(wkv0, seqs, W["lnx_w"], W["lnx_b"], tp)
    mem_o = _mem_attn(q_mem, q_col, W["mem_qnorm"][0], mem_kt, mem_vt, 0, MEM_Q_TILE, mem_bb, act_dtype)
    tables = (cos2, sin2) if prompt else (jnp.tile(cos2, (tm // tp, 1)), jnp.tile(sin2, (tm // tp, 1)))
    h, k_sh, v_sh, proj = _out_mlp(flat(x), flat(main), flat(mem_o), W["w_out"][0], W["norm_mlp"][0], W["w_up"][0],
                                   W["w_down"][0], tm,
                                   nxt=(W["kv_norm"], W["w_kv"], W["swa_knorm"], *tables, W["norm_mix"][1], W["w_in_b"]))
    k_sh, v_sh, proj = unflat(k_sh), unflat(v_sh), unflat(proj)

    if prompt:
        main = _swa(proj, W["swa_qnorm"], W["sinks"], cos2, sin2, k_sh, v_sh)
        win = min(WINDOW, tp)
        heads = lambda a: a.reshape(bsz, a.shape[1], SWA_KV_HEADS, HEAD_DIM)
        k_state, v_state = heads(k_sh[:, tp - win:]), heads(v_sh[:, tp - win:])
    else:
        ckt, cvt = swa_cache
        main, kt_new, vt_new = _swa_step(proj, W["swa_qnorm"], W["sinks"], cos2, sin2, ckt, cvt, k_sh, v_sh,
                                         t_real, bb=STEP_SWA_SEQS)
        untransposed = lambda a: jnp.transpose(a.reshape(bsz, SWA_KV_HEADS, HEAD_DIM, -1), (0, 3, 1, 2))
        k_state, v_state = untransposed(kt_new), untransposed(vt_new)
    mem_o = _mem_attn(proj, MAIN_WIDTH // MEM_WIDTH, W["mem_qnorm"][1], mem_kt, mem_vt, 1, MEM_Q_TILE, mem_bb, act_dtype)
    y = _out_mlp(h, flat(main), flat(mem_o), W["w_out"][1], W["norm_mlp"][1], W["w_up"][1], W["w_down"][1], tm)
    return unflat(y)[:, :t_real], new_shift[None], new_wkv[None], k_state, v_state


def kernel(x_prompt, x_sample, state_rwkv_shift, state_rwkv_wkv, cache_swa_k, cache_swa_v, cache_mem_k,
           cache_mem_v, mem_prompt, norm_mix, norm_mlp, w_out, w_up, w_down, mem_norm, w_mem_kv, mem_qnorm,
           mem_knorm, w_in_a, shift_mu, w_w2, w0, w_a2, a0, w_g2, k_k, k_a, r_k, lnx_w, lnx_b, w_in_b,
           swa_qnorm, sinks, kv_norm, w_kv, swa_knorm):
    W = dict(norm_mix=norm_mix, norm_mlp=norm_mlp, w_out=w_out.astype(bf16), w_up=w_up.astype(bf16),
             w_down=w_down.astype(bf16), mem_qnorm=mem_qnorm, w_in_a=w_in_a[0].astype(bf16),
             shift_mu=shift_mu[0], w_w2=w_w2[0], w0=w0[0], w_a2=w_a2[0], a0=a0[0], w_g2=w_g2[0],
             k_k=k_k[0], k_a=k_a[0], r_k=r_k[0].reshape(-1), lnx_w=lnx_w[0], lnx_b=lnx_b[0],
             w_in_b=w_in_b[0].astype(bf16), swa_qnorm=swa_qnorm[0], sinks=sinks[0], kv_norm=kv_norm,
             w_kv=w_kv.astype(bf16), swa_knorm=swa_knorm)
    bp, tp, _ = x_prompt.shape
    bs, ts, _ = x_sample.shape
    depth = norm_mix.shape[0]

    mem_flat = mem_prompt.reshape(bp * N_MEM, D_MODEL)
    p_mem_k, p_mem_v = [], []
    for l in range(depth):
        kv = _norm_matmul(mem_flat, mem_norm[l], w_mem_kv[l].astype(bf16), ROW_TILE)
        mk, mv = _kv_post(kv, mem_knorm[l], N_MEM)
        p_mem_k.append(mk.reshape(bp, N_MEM, MEM_WIDTH))
        p_mem_v.append(mv.reshape(bp, N_MEM, MEM_WIDTH))
    transposed = lambda a: jnp.swapaxes(jnp.stack(a), 2, 3)
    y_p, p_shift, p_wkv, p_k, p_v = _trunk(
        x_prompt, jnp.arange(tp), transposed(p_mem_k), transposed(p_mem_v), jnp.zeros((bp, RWKV_COLS), f32),
        None, None, tp, W)
    mem_heads = lambda a: jnp.stack(a).reshape(depth, bp, N_MEM, MEM_HEADS, HEAD_DIM)
    cache_t = lambda a: jnp.transpose(a, (0, 1, 3, 4, 2)).reshape(depth, bs, MEM_WIDTH, N_MEM)

    tpad = -(-ts // SUBLANES) * SUBLANES
    x_s = jnp.pad(x_sample, ((0, 0), (0, tpad - ts), (0, 0)))
    win = cache_swa_k.shape[1]
    assert win == WINDOW
    swa_t = lambda a: jnp.transpose(a, (0, 2, 3, 1)).reshape(bs, KV_WIDTH, win)
    y_s, s_shift, s_wkv, s_k, s_v = _trunk(
        x_s, PAST_LEN + jnp.arange(tpad), cache_t(cache_mem_k), cache_t(cache_mem_v),
        state_rwkv_shift[0], state_rwkv_wkv[0],
        (swa_t(cache_swa_k), swa_t(cache_swa_v)), ts, W)

    return (y_p, y_s, p_shift, p_wkv, p_k, p_v, mem_heads(p_mem_k), mem_heads(p_mem_v),
            s_shift, s_wkv, s_k, s_v)
```

```python
import functools
import math

import jax
import jax.numpy as jnp
from jax import lax
from jax.experimental import pallas as pl
from jax.experimental.pallas import tpu as pltpu

f32 = jnp.float32
bf16 = jnp.bfloat16

D_MODEL = 1024
HEAD_DIM = 64
MEM_HEADS = 4
MEM_WIDTH = MEM_HEADS * HEAD_DIM
MAIN_WIDTH = D_MODEL - MEM_WIDTH
RWKV_HEADS = MAIN_WIDTH // HEAD_DIM
DECAY_LORA = 64
AAA_LORA = 64
GATE_LORA = 128
RWKV_COLS = 3 * MAIN_WIDTH + DECAY_LORA + AAA_LORA + GATE_LORA
A_IN_COLS = RWKV_COLS + MEM_WIDTH
SWA_Q_HEADS = MAIN_WIDTH // HEAD_DIM
SWA_KV_HEADS = 4
SWA_GROUP = SWA_Q_HEADS // SWA_KV_HEADS
KV_WIDTH = SWA_KV_HEADS * HEAD_DIM
WINDOW = 128
BLOCK = 128
N_MEM = 256
D_FF = 4 * D_MODEL
PAST_LEN = 16384
ROPE_THETA = 10000.0
NORM_EPS = 1e-6
LNX_EPS = 6.4e-4
L2_EPS = 1e-12
ATTN_SCALE = HEAD_DIM ** -0.5

LANES = 128
SUBLANES = 8
HEADS_PER_SLAB = LANES // HEAD_DIM
N_SLABS = MAIN_WIDTH // LANES
CHUNK = 64
WKV_CHUNKS_PER_STEP = 16
VMEM_LIMIT = 56 * 1024 * 1024
NEG_BIG = -1e30
ROW_TILE = 512
PREP_TILE = 512
MEM_Q_TILE = 512
MEM_Q_SUB = 256
STEP_MEM_SEQS = 8
STEP_SWA_SEQS = 4


def _cp(sem, vmem=VMEM_LIMIT):
    return pltpu.CompilerParams(dimension_semantics=sem, vmem_limit_bytes=vmem)


def _dot(a, b):
    return jnp.dot(a, b, preferred_element_type=f32)


def _dot_nt(a, b):
    return lax.dot_general(a, b, (((1,), (1,)), ((), ())), preferred_element_type=f32)


def _dot_tn(a, b):
    return lax.dot_general(a, b, (((0,), (0,)), ((), ())), preferred_element_type=f32)


def _iota(shape, dim):
    return lax.broadcasted_iota(jnp.int32, shape, dim)


def _group_ones():
    r = _iota((LANES, LANES), 0)
    c = _iota((LANES, LANES), 1)
    return jnp.where((r < HEAD_DIM) == (c < HEAD_DIM), 1.0, 0.0).astype(f32)


def _head_sum(x, gmat, passes=2):
    w = x.shape[-1]
    gb = gmat.astype(bf16)
    hi = x.astype(bf16)
    slabs = lambda y: [y[:, j * LANES:(j + 1) * LANES] for j in range(w // LANES)]
    parts = [_dot(h, gb) for h in slabs(hi)]
    if passes == 2:
        lo = (x - hi.astype(f32)).astype(bf16)
        parts = [p + _dot(l, gb) for p, l in zip(parts, slabs(lo))]
    return parts[0] if len(parts) == 1 else jnp.concatenate(parts, axis=-1)


def _cumsum_rows(x):
    rows = x.shape[0]
    row = _iota(x.shape, 0)
    step = 1
    while step < rows:
        x = x + jnp.where(row >= step, pltpu.roll(x, step, 0), 0.0)
        step *= 2
    return x


def _sigmoid(x):
    return 1.0 / (1.0 + jnp.exp(-x))


def _rot_half(x):
    lane = _iota(x.shape, 1)
    first = (lane & (HEAD_DIM - 1)) < (HEAD_DIM // 2)
    return jnp.where(first, pltpu.roll(x, LANES - HEAD_DIM // 2, 1), pltpu.roll(x, HEAD_DIM // 2, 1))


def _norm_matmul_kernel(x_ref, g_ref, w_ref, o_ref):
    x = x_ref[...]
    xn = x * lax.rsqrt(jnp.mean(x * x, axis=-1, keepdims=True) + NORM_EPS) * g_ref[...]
    o_ref[...] = _dot(xn.astype(bf16), w_ref[...])


def _norm_matmul(x, g, w_bf16, tm):
    n, d = x.shape
    m = w_bf16.shape[1]
    tm = min(tm, n)
    return pl.pallas_call(
        _norm_matmul_kernel,
        grid=(n // tm,),
        in_specs=[pl.BlockSpec((tm, d), lambda i: (i, 0)),
                  pl.BlockSpec((1, d), lambda i: (0, 0)),
                  pl.BlockSpec((d, m), lambda i: (0, 0))],
        out_specs=pl.BlockSpec((tm, m), lambda i: (i, 0)),
        out_shape=jax.ShapeDtypeStruct((n, m), f32),
        compiler_params=_cp(("parallel",)),
        name="norm_matmul",
    )(x, g.reshape(1, d), w_bf16)


def _head_norm_rope(x, g, cos, sin, gmat):
    xn = x * lax.rsqrt(_head_sum(x * x, gmat) * (1.0 / HEAD_DIM) + NORM_EPS) * g
    slabs = [xn[:, j * LANES:(j + 1) * LANES] for j in range(x.shape[-1] // LANES)]
    return [xs * cos + _rot_half(xs) * sin for xs in slabs]


def _kv_post_kernel(kv_ref, g_ref, k_ref, v_ref):
    gmat = _group_ones()
    x = kv_ref[:, :KV_WIDTH]
    k_ref[...] = x * lax.rsqrt(_head_sum(x * x, gmat) * (1.0 / HEAD_DIM) + NORM_EPS) * g_ref[...]
    v_ref[...] = kv_ref[:, KV_WIDTH:]


def _kv_post(kv, g_head, tm):
    n = kv.shape[0]
    tm = min(tm, n)
    g = jnp.tile(g_head.reshape(1, HEAD_DIM), (1, KV_WIDTH // HEAD_DIM))
    return pl.pallas_call(
        _kv_post_kernel,
        grid=(n // tm,),
        in_specs=[pl.BlockSpec((tm, 2 * KV_WIDTH), lambda i: (i, 0)),
                  pl.BlockSpec((1, KV_WIDTH), lambda i: (0, 0))],
        out_specs=[pl.BlockSpec((tm, KV_WIDTH), lambda i: (i, 0)),
                   pl.BlockSpec((tm, KV_WIDTH), lambda i: (i, 0))],
        out_shape=[jax.ShapeDtypeStruct((n, KV_WIDTH), f32)] * 2,
        compiler_params=_cp(("parallel",)),
        name="kv_post",
    )(kv, g)


def _time_mix_inputs(pf, prev, consts, outs):
    mu_ref, ww2_ref, w0_ref, wa2_ref, a0_ref, wg2_ref, kk_ref, ka_ref, rk_ref = consts
    r_o, lw_o, k_o, v_o, a_o, b_o, bonus_o, g_o = outs
    gmat = _group_ones()
    ps = pf + (prev - pf) * mu_ref[...]
    i1, i2, i3 = MAIN_WIDTH, 2 * MAIN_WIDTH, 3 * MAIN_WIDTH
    i4, i5 = i3 + DECAY_LORA, i3 + DECAY_LORA + AAA_LORA
    r, k, v = ps[:, :i1], ps[:, i1:i2], ps[:, i2:i3]
    wd, ad, gd = ps[:, i3:i4], ps[:, i4:i5], ps[:, i5:]
    lora = lambda x, w_ref: _dot(x.astype(bf16), w_ref[...].astype(bf16))
    lw = -math.exp(-0.5) * _sigmoid(w0_ref[...] + lora(jnp.tanh(wd), ww2_ref))
    a_sig = _sigmoid(a0_ref[...] + lora(ad, wa2_ref))
    g = lora(_sigmoid(gd), wg2_ref)
    kk = k * kk_ref[...]
    kk = kk * jnp.minimum(lax.rsqrt(_head_sum(kk * kk, gmat, passes=1)), 1.0 / L2_EPS)
    k2 = k * (1.0 + (a_sig - 1.0) * ka_ref[...])
    r_o[0] = r.astype(r_o.dtype)
    lw_o[0] = lw
    k_o[0] = k2.astype(k_o.dtype)
    v_o[0] = v.astype(v_o.dtype)
    a_o[0] = (-kk).astype(a_o.dtype)
    b_o[0] = (kk * a_sig).astype(b_o.dtype)
    bonus_o[0] = (_head_sum(r * k2 * rk_ref[...], gmat, passes=1) * v).astype(bonus_o.dtype)
    g_o[0] = g.astype(g_o.dtype)


def _rwkv_prep_kernel(p_ref, shift_ref, *refs):
    pf = p_ref[0]
    prev = jnp.concatenate([shift_ref[0], pf[:pf.shape[0] - shift_ref.shape[1]]], axis=0)
    _time_mix_inputs(pf, prev, refs[:9], refs[9:])


def _proj_prep_kernel(x_ref, gx_ref, w_ref, shift_ref, *refs):
    consts, outs, (q_o, last_o, carry_ref) = refs[:9], refs[9:17], refs[17:]
    i = pl.program_id(1)
    x = x_ref[0]
    xn = x * lax.rsqrt(jnp.mean(x * x, axis=-1, keepdims=True) + NORM_EPS) * gx_ref[...]
    proj = _dot(xn.astype(bf16), w_ref[...])
    q_o[0] = proj[:, RWKV_COLS:]
    pf = proj[:, :RWKV_COLS]
    first_prev = jnp.where(i == 0, shift_ref[0], carry_ref[...])
    prev = jnp.where(_iota(pf.shape, 0) == 0, first_prev, pltpu.roll(pf, 1, 0))
    last = pf[pf.shape[0] - 1:, :]
    carry_ref[...] = last
    last_o[0] = last
    _time_mix_inputs(pf, prev, consts, outs)


def _proj_prep(x, g_norm, w_in_bf16, shift_prev, prep_w, tt, seq_dtype):
    bsz, t, _ = x.shape
    row = lambda a: a.reshape(1, -1)
    full = lambda a: pl.BlockSpec(a.shape, lambda b, i: (0,) * a.ndim)
    mu, w_w2, w0, w_a2, a0, w_g2, k_k, k_a, r_k = prep_w
    consts = [row(mu), w_w2, row(w0), w_a2, row(a0), w_g2, row(k_k), row(k_a), row(r_k)]
    tile = lambda w: pl.BlockSpec((1, tt, w), lambda b, i: (b, i, 0))
    per_seq = pl.BlockSpec((1, 1, RWKV_COLS), lambda b, i: (b, 0, 0))
    res = pl.pallas_call(
        _proj_prep_kernel,
        grid=(bsz, t // tt),
        in_specs=[tile(D_MODEL), full(row(g_norm)), full(w_in_bf16), per_seq] + [full(c) for c in consts],
        out_specs=[tile(MAIN_WIDTH)] * 8 + [tile(MEM_WIDTH), per_seq],
        out_shape=[jax.ShapeDtypeStruct((bsz, t, MAIN_WIDTH), f32 if j == 1 else seq_dtype) for j in range(8)]
                  + [jax.ShapeDtypeStruct((bsz, t, MEM_WIDTH), f32), jax.ShapeDtypeStruct((bsz, 1, RWKV_COLS), f32)],
        scratch_shapes=[pltpu.VMEM((1, RWKV_COLS), f32)],
        compiler_params=_cp(("parallel", "arbitrary")),
        name="proj_prep",
    )(x, row(g_norm), w_in_bf16, shift_prev.reshape(bsz, 1, RWKV_COLS), *consts)
    return res[:8], res[8], res[9].reshape(bsz, RWKV_COLS)


def _rwkv_prep(steps, shift_prev, prep_w):
    _, t, _ = steps.shape
    row = lambda a: a.reshape(1, -1)
    full = lambda a: pl.BlockSpec(a.shape, lambda i: (0,) * a.ndim)
    mu, w_w2, w0, w_a2, a0, w_g2, k_k, k_a, r_k = prep_w
    consts = [row(mu), w_w2, row(w0), w_a2, row(a0), w_g2, row(k_k), row(k_a), row(r_k)]
    shift_rows = shift_prev[None]
    return pl.pallas_call(
        _rwkv_prep_kernel,
        grid=(1,),
        in_specs=[full(steps), full(shift_rows)] + [full(c) for c in consts],
        out_specs=[pl.BlockSpec((1, t, MAIN_WIDTH), lambda i: (0, 0, 0))] * 8,
        out_shape=[jax.ShapeDtypeStruct((1, t, MAIN_WIDTH), f32)] * 8,
        compiler_params=_cp(("arbitrary",)),
        name="rwkv_prep",
    )(steps, shift_rows, *consts)


def _wkv_chunk_kernel(r_ref, lw_ref, k_ref, v_ref, a_ref, b_ref, bonus_ref, g_ref, lnw_ref, lnb_ref,
                      o_ref, s_ref, h_ref, qt_ref, yv_ref, pp_ref, hv_ref, tc_ref):
    i = pl.program_id(2)
    c = CHUNK
    nc = r_ref.shape[1] // c
    n = range(nc)

    @pl.when(i == 0)
    def _():
        h_ref[...] = jnp.zeros_like(h_ref)
        qt_ref[...] = jnp.zeros_like(qt_ref)
        yv_ref[...] = jnp.zeros_like(yv_ref)
        pp_ref[...] = jnp.zeros_like(pp_ref)
        hv_ref[...] = jnp.zeros_like(hv_ref)
        tc_ref[...] = jnp.zeros_like(tc_ref)

    gmat = _group_ones()
    bd = gmat > 0.5
    row = _iota((LANES, LANES), 0)
    col = _iota((LANES, LANES), 1)
    t_row = row & (c - 1)
    s_col = col & (c - 1)
    m_strict = s_col < t_row
    m_incl = s_col <= t_row
    top = row < c
    lane_lo = _iota((1, LANES), 1) < HEAD_DIM
    eye = row == col
    zeros = jnp.zeros((c, LANES), f32)
    mm = lambda x: x.astype(bf16)
    gb = gmat.astype(bf16)
    sls = [slice(j * c, (j + 1) * c) for j in n]

    def pending_state():
        state = {"h": h_ref[...], "ys": []}

        def state_step(j):
            hb = mm(state["h"])
            state["ys"].append(_dot(qt_ref[j], hb) + yv_ref[j])
            state["h"] = state["h"] * tc_ref[j] + _dot(pp_ref[j], hb) + hv_ref[j]

        def finish():
            h_ref[...] = state["h"]
            means = [_dot(mm(y), gb) * (1.0 / HEAD_DIM) for y in state["ys"]]
            ycs = [state["ys"][j] - means[j] for j in n]
            vrs = [_dot(mm(yc * yc), gb) * (1.0 / HEAD_DIM) for yc in ycs]
            outs = []
            for j in n:
                yn = ycs[j] * lax.rsqrt(vrs[j] + LNX_EPS) * lnw_ref[...] + lnb_ref[...]
                outs.append(((yn + bonus_ref[0, sls[j], :].astype(f32))
                             * g_ref[0, sls[j], :].astype(f32)).astype(o_ref.dtype))
            return outs

        return state_step, finish

    def chunk_group(js):
        st = {}

        def decay_and_blocks():
            st["vs"], st["tots"], st["ats"], st["rts"], st["bkts"], gms = [], [], [], [], [], []
            for j in js:
                sl = sls[j]
                lw = lw_ref[0, sl, :]
                cum = _cumsum_rows(lw)
                tot = cum[c - 1:c, :]
                b, k = b_ref[0, sl, :].astype(f32), k_ref[0, sl, :].astype(f32)
                e_neg, e_rem = jnp.exp(-cum), jnp.exp(tot - cum)
                at = a_ref[0, sl, :].astype(f32) * jnp.exp(cum - lw)
                rt = r_ref[0, sl, :].astype(f32) * jnp.exp(cum)
                bk = mm(jnp.concatenate([b * e_neg, k * e_neg], axis=0))
                lhs = mm(jnp.concatenate([jnp.where(lane_lo, at, 0.0), jnp.where(lane_lo, 0.0, at),
                                          jnp.where(lane_lo, rt, 0.0), jnp.where(lane_lo, 0.0, rt)], axis=0))
                gms.append(_dot_nt(lhs, bk))
                st["vs"].append(v_ref[0, sl, :].astype(f32))
                st["tots"].append(tot)
                st["ats"].append(at)
                st["rts"].append(rt)
                st["bkts"].append(mm(jnp.concatenate([b * e_rem, k * e_rem], axis=0)))
            st["gms"] = gms

        def first_products():
            m = range(len(js))
            aaks = [jnp.where(m_strict, gm[:LANES], 0.0) for gm in st["gms"]]
            st["arks"] = [mm(jnp.where(m_incl, gm[LANES:], 0.0)) for gm in st["gms"]]
            aak_sws = [pltpu.roll(aak, HEAD_DIM, 1) for aak in aaks]
            st["ms"] = [jnp.where(bd, jnp.where(top, aaks[q], aak_sws[q]), 0.0) for q in m]
            aks = [mm(jnp.where(bd, jnp.where(top, aak_sws[q], aaks[q]), 0.0)) for q in m]
            vvs = [mm(jnp.concatenate([pltpu.roll(v, HEAD_DIM, 1)] * 2, axis=0)) for v in st["vs"]]
            akvs = [_dot(aks[q], vvs[q]) for q in m]
            st["xs"] = [jnp.where(bd, jnp.concatenate([st["ats"][q]] * 2, axis=0), akvs[q]) for q in m]

        def square(last):
            def stage():
                m = range(len(js))
                ms, xs = st["ms"], st["xs"]
                if not last:
                    zs = [_dot(mm(ms[q]), mm(jnp.concatenate([xs[q], ms[q]], axis=1))) for q in m]
                    st["xs"] = [xs[q] + zs[q][:, :LANES] for q in m]
                    st["ms"] = [zs[q][:, LANES:] for q in m]
                else:
                    zs = [_dot(mm(ms[q]), mm(xs[q])) for q in m]
                    st["xs"] = [xs[q] + zs[q] for q in m]
            return stage

        def last_products():
            m = range(len(js))
            rhss = []
            for q in m:
                x = st["xs"][q]
                ta = jnp.where(lane_lo, x[:c], x[c:])
                uv = pltpu.roll(jnp.where(lane_lo, x[c:], x[:c]), HEAD_DIM, 1)
                rhss.append(mm(jnp.concatenate([jnp.concatenate([ta, uv], axis=1),
                                                jnp.concatenate([zeros, st["vs"][q]], axis=1)], axis=0)))
            st["z2s"] = [_dot(st["arks"][q], rhss[q]) for q in m]
            st["pzs"] = [_dot_tn(st["bkts"][q], rhss[q]) for q in m]

        def make_pending():
            for q, j in enumerate(js):
                z2, pz = st["z2s"][q], st["pzs"][q]
                qt_ref[j] = mm(st["rts"][q] + jnp.where(lane_lo, z2[:c, :LANES], z2[c:, :LANES]))
                yv_ref[j] = jnp.where(lane_lo, z2[:c, LANES:], z2[c:, LANES:])
                pp_ref[j] = mm(jnp.where(bd, pz[:, :LANES], 0.0))
                hv_ref[j] = jnp.where(bd, pz[:, LANES:], 0.0)
                tc_ref[j] = jnp.sum(jnp.where(eye, jnp.exp(st["tots"][q]), 0.0), axis=1, keepdims=True)

        return ([decay_and_blocks, first_products] + [square(False)] * 5 + [square(True), last_products], make_pending)

    @pl.when(i < pl.num_programs(2) - 1)
    def _():
        half = nc // 2
        (stages_a, pend_a), (stages_b, pend_b) = chunk_group(list(range(half))), chunk_group(list(range(half, nc)))
        lag = 4
        state_step, finish = pending_state()
        outs = None
        for slot in range(max(len(stages_a) + lag, nc + 1)):
            if slot < nc:
                state_step(slot)
            if slot == nc:
                outs = finish()
            if slot < len(stages_a):
                stages_a[slot]()
            if 0 <= slot - lag < len(stages_b):
                stages_b[slot - lag]()
        for j in n:
            o_ref[0, sls[j], :] = outs[j]
        pend_a()
        pend_b()

    @pl.when(i == pl.num_programs(2) - 1)
    def _():
        state_step, finish = pending_state()
        for j in n:
            state_step(j)
        outs = finish()
        for j in n:
            o_ref[0, sls[j], :] = outs[j]

    @pl.when(i == pl.num_programs(2) - 1)
    def _():
        s_ref[0, 0] = h_ref[...].T


def _wkv_chunked(r, lw, k, v, a, b, bonus, g, lnx_w, lnx_b):
    bsz, t, _ = r.shape
    nc = WKV_CHUNKS_PER_STEP
    tt = nc * CHUNK
    nt = t // tt
    assert t % tt == 0
    cur = pl.BlockSpec((1, tt, LANES), lambda bb, hp, i: (bb, jnp.minimum(i, nt - 1), hp))
    pend = pl.BlockSpec((1, tt, LANES), lambda bb, hp, i: (bb, jnp.maximum(i - 1, 0), hp))
    vec = pl.BlockSpec((1, LANES), lambda bb, hp, i: (0, hp))
    out, st = pl.pallas_call(
        _wkv_chunk_kernel,
        grid=(bsz, N_SLABS, nt + 1),
        in_specs=[cur] * 6 + [pend, pend, vec, vec],
        out_specs=[pend, pl.BlockSpec((1, 1, LANES, LANES), lambda bb, hp, i: (bb, hp, 0, 0))],
        out_shape=[jax.ShapeDtypeStruct((bsz, t, MAIN_WIDTH), bf16),
                   jax.ShapeDtypeStruct((bsz, N_SLABS, LANES, LANES), f32)],
        scratch_shapes=[pltpu.VMEM((LANES, LANES), f32),
                        pltpu.VMEM((nc, CHUNK, LANES), bf16), pltpu.VMEM((nc, CHUNK, LANES), f32),
                        pltpu.VMEM((nc, LANES, LANES), bf16), pltpu.VMEM((nc, LANES, LANES), f32),
                        pltpu.VMEM((nc, LANES, 1), f32)],
        compiler_params=_cp(("parallel", "parallel", "arbitrary")),
        name="wkv_chunked",
    )(r, lw, k, v, a, b, bonus, g, lnx_w.reshape(1, -1), lnx_b.reshape(1, -1))
    st = jnp.stack([st[:, :, :HEAD_DIM, :HEAD_DIM], st[:, :, HEAD_DIM:, HEAD_DIM:]], axis=2)
    return out, st.reshape(bsz, RWKV_HEADS, HEAD_DIM, HEAD_DIM)


def _wkv_steps_kernel(s_ref, seq_ref, lnw_ref, lnb_ref, o_ref, so_ref, w_ref, y_ref, *, n_steps):
    i_r, i_lw, i_k, i_v, i_a, i_b, i_bonus, i_g = range(8)
    w_ref[...] = jnp.exp(seq_ref[i_lw])
    sub = _iota((SUBLANES, LANES), 0)

    def group(vg, carry):
        v0 = pl.multiple_of(vg * SUBLANES, SUBLANES)
        v_rows = [seq_ref[i_v, t, pl.ds(v0, SUBLANES), :] for t in range(n_steps)]
        ys = [jnp.zeros((SUBLANES, LANES), f32) for _ in range(n_steps)]
        for j in range(SUBLANES):
            sv = s_ref[0, v0 + j]
            for t in range(n_steps):
                sa = jnp.sum(sv * seq_ref[i_a, t], axis=0, keepdims=True)
                sv = sv * w_ref[t] + sa * seq_ref[i_b, t] + v_rows[t][j:j + 1, :] * seq_ref[i_k, t]
                y = jnp.sum(sv * seq_ref[i_r, t], axis=0, keepdims=True)
                ys[t] = jnp.where(sub == j, y, ys[t])
            so_ref[0, v0 + j] = sv
        for t in range(n_steps):
            y_ref[t, pl.ds(v0, SUBLANES), :] = ys[t]
        return carry

    lax.fori_loop(0, HEAD_DIM // SUBLANES, group, 0)
    y = y_ref[...]
    mean = jnp.mean(y, axis=1, keepdims=True)
    yc = y - mean
    var = jnp.mean(yc * yc, axis=1, keepdims=True)
    yn = yc * lax.rsqrt(var + LNX_EPS) * lnw_ref[...] + lnb_ref[...]
    o_ref[...] = (yn + seq_ref[i_bonus]) * seq_ref[i_g]


def _wkv_steps(state, seqs, lnx_w, lnx_b, tp):
    n_steps, bsz, _ = seqs[0].shape
    assert bsz % LANES == 0
    state_t = jnp.transpose(state, (1, 2, 3, 0))
    seq_t = jnp.transpose(jnp.stack(seqs), (0, 1, 3, 2))
    lanes = lambda x: jnp.broadcast_to(x.reshape(MAIN_WIDTH, 1), (MAIN_WIDTH, LANES))
    sspec = pl.BlockSpec((1, HEAD_DIM, HEAD_DIM, LANES), lambda h, bi: (h, 0, 0, bi))
    vec = pl.BlockSpec((HEAD_DIM, LANES), lambda h, bi: (h, 0))
    ospec = pl.BlockSpec((n_steps, HEAD_DIM, LANES), lambda h, bi: (0, h, bi))
    out, new_state = pl.pallas_call(
        functools.partial(_wkv_steps_kernel, n_steps=n_steps),
        grid=(RWKV_HEADS, bsz // LANES),
        in_specs=[sspec, pl.BlockSpec((8, n_steps, HEAD_DIM, LANES), lambda h, bi: (0, 0, h, bi)), vec, vec],
        out_specs=[ospec, sspec],
        out_shape=[jax.ShapeDtypeStruct((n_steps, MAIN_WIDTH, bsz), f32),
                   jax.ShapeDtypeStruct(state_t.shape, f32)],
        scratch_shapes=[pltpu.VMEM((n_steps, HEAD_DIM, LANES), f32), pltpu.VMEM((n_steps, HEAD_DIM, LANES), f32)],
        compiler_params=_cp(("parallel", "parallel")),
        name="wkv_steps",
    )(state_t, seq_t, lanes(lnx_w), lanes(lnx_b))
    out = jnp.pad(jnp.transpose(out, (2, 0, 1)), ((0, 0), (0, tp - n_steps), (0, 0)))
    return out, jnp.transpose(new_state, (3, 0, 1, 2))


def _mem_attn_kernel(q_ref, g_ref, kt_ref, vt_ref, o_ref, *, sub):
    gmat = _group_ones()
    bb, tq, _ = q_ref.shape
    head = _iota((1, MEM_WIDTH), 1) // HEAD_DIM
    tiles = [(b, pl.ds(r * sub, sub)) for b in range(bb) for r in range(tq // sub)]
    qs = []
    for b, rows in tiles:
        q = q_ref[b, rows, :]
        qn = q * lax.rsqrt(_head_sum(q * q, gmat) * (1.0 / HEAD_DIM) + NORM_EPS) * (g_ref[...] * ATTN_SCALE)
        qs.append(jnp.concatenate([jnp.where(head == h, qn, 0.0) for h in range(MEM_HEADS)], axis=0).astype(bf16))
    ss = [_dot(q4, kt_ref[0, b].astype(bf16)) for q4, (b, _) in zip(qs, tiles)]
    es = [jnp.exp(s - jnp.max(s, axis=-1, keepdims=True)) for s in ss]
    o4s = [_dot_nt(e.astype(bf16), vt_ref[0, b].astype(bf16)) / jnp.sum(e, axis=-1, keepdims=True)
           for e, (b, _) in zip(es, tiles)]
    for o4, (b, rows) in zip(o4s, tiles):
        o = jnp.zeros((sub, MEM_WIDTH), f32)
        for h in range(MEM_HEADS):
            o = o + jnp.where(head == h, o4[h * sub:(h + 1) * sub], 0.0)
        o_ref[b, rows, :] = o.astype(o_ref.dtype)


def _mem_attn(proj, col_block, g_qnorm, mem_kt, mem_vt, layer, tq, bb, out_dtype):
    bsz, t, _ = proj.shape
    tq = min(tq, t)
    g = jnp.tile(g_qnorm.reshape(1, HEAD_DIM), (1, MEM_HEADS))
    kv_spec = pl.BlockSpec((1, bb, MEM_WIDTH, N_MEM), lambda b, i: (layer, b, 0, 0))
    return pl.pallas_call(
        functools.partial(_mem_attn_kernel, sub=min(tq, MEM_Q_SUB)),
        grid=(bsz // bb, t // tq),
        in_specs=[pl.BlockSpec((bb, tq, MEM_WIDTH), lambda b, i: (b, i, col_block)),
                  pl.BlockSpec((1, MEM_WIDTH), lambda b, i: (0, 0)),
                  kv_spec, kv_spec],
        out_specs=pl.BlockSpec((bb, tq, MEM_WIDTH), lambda b, i: (b, i, 0)),
        out_shape=jax.ShapeDtypeStruct((bsz, t, MEM_WIDTH), out_dtype),
        compiler_params=_cp(("parallel", "parallel")),
        name="mem_attn",
    )(proj, g, mem_kt, mem_vt)


def _swa_kernel(sink_ref, q_ref, g_ref, cos_ref, sin_ref, kp_ref, kc_ref, vp_ref, vc_ref, o_ref, qn_ref):
    i = pl.program_id(0)
    gmat = _group_ones()
    bb, tq, _ = q_ref.shape
    keys, vals = [], []
    for b in range(bb):
        for j in range(N_SLABS):
            sl = slice(j * LANES, (j + 1) * LANES)
            x = q_ref[b, :, sl]
            xn = x * lax.rsqrt(_head_sum(x * x, gmat) * (1.0 / HEAD_DIM) + NORM_EPS) * g_ref[...]
            qn_ref[b, :, sl] = (xn * cos_ref[...] + _rot_half(xn) * sin_ref[...]) * ATTN_SCALE
        keys.append(jnp.concatenate([kp_ref[b], kc_ref[b]], axis=0).astype(bf16))
        vals.append(jnp.concatenate([vp_ref[b], vc_ref[b]], axis=0).astype(bf16))
    rows = SWA_GROUP * tq
    row = _iota((rows, 2 * BLOCK), 0) & (tq - 1)
    col = _iota((rows, 2 * BLOCK), 1)
    mask = (col > row) & (col <= row + WINDOW) & ((i > 0) | (col >= BLOCK))
    grp = _iota((rows, 1), 0) // tq
    pairs = [(b, hk) for b in range(bb) for hk in range(SWA_KV_HEADS)]
    hs = [slice(hk * HEAD_DIM, (hk + 1) * HEAD_DIM) for hk in range(SWA_KV_HEADS)]
    q3s = [jnp.concatenate([qn_ref[b, :, (hk * SWA_GROUP + gq) * HEAD_DIM:(hk * SWA_GROUP + gq + 1) * HEAD_DIM]
                            for gq in range(SWA_GROUP)], axis=0).astype(bf16) for b, hk in pairs]
    ss = [jnp.where(mask, _dot_nt(q3, keys[b][:, hs[hk]]), NEG_BIG) for q3, (b, hk) in zip(q3s, pairs)]
    sinks = [jnp.where(grp == 0, sink_ref[hk * SWA_GROUP],
                       jnp.where(grp == 1, sink_ref[hk * SWA_GROUP + 1], sink_ref[hk * SWA_GROUP + 2]))
             for hk in range(SWA_KV_HEADS)]
    ms = [jnp.maximum(jnp.max(s, axis=-1, keepdims=True), sinks[hk]) for s, (b, hk) in zip(ss, pairs)]
    es = [jnp.exp(s - m) for s, m in zip(ss, ms)]
    denoms = [jnp.sum(e, axis=-1, keepdims=True) + jnp.exp(sinks[hk] - m) for e, m, (b, hk) in zip(es, ms, pairs)]
    outs = [_dot(e.astype(bf16), vals[b][:, hs[hk]]) / d for e, d, (b, hk) in zip(es, denoms, pairs)]
    for o, (b, hk) in zip(outs, pairs):
        for gq in range(SWA_GROUP):
            hq = hk * SWA_GROUP + gq
            o_ref[b, :, hq * HEAD_DIM:(hq + 1) * HEAD_DIM] = o[gq * tq:(gq + 1) * tq].astype(o_ref.dtype)


def _swa(proj, g_qnorm, sinks, cos, sin, k, v):
    bsz, t, _ = proj.shape
    assert SWA_GROUP == 3
    tq = BLOCK
    g = jnp.tile(g_qnorm.reshape(1, HEAD_DIM), (1, HEADS_PER_SLAB))
    prev = pl.BlockSpec((bsz, BLOCK, KV_WIDTH), lambda i: (0, jnp.maximum(i - 1, 0), 0))
    cur = pl.BlockSpec((bsz, BLOCK, KV_WIDTH), lambda i: (0, i, 0))
    return pl.pallas_call(
        _swa_kernel,
        grid=(t // tq,),
        in_specs=[pl.BlockSpec(memory_space=pltpu.SMEM),
                  pl.BlockSpec((bsz, tq, MAIN_WIDTH), lambda i: (0, i, 0)),
                  pl.BlockSpec((1, LANES), lambda i: (0, 0)),
                  pl.BlockSpec((tq, LANES), lambda i: (i, 0)),
                  pl.BlockSpec((tq, LANES), lambda i: (i, 0)),
                  prev, cur, prev, cur],
        out_specs=pl.BlockSpec((bsz, tq, MAIN_WIDTH), lambda i: (0, i, 0)),
        out_shape=jax.ShapeDtypeStruct((bsz, t, MAIN_WIDTH), bf16),
        scratch_shapes=[pltpu.VMEM((bsz, tq, MAIN_WIDTH), f32)],
        compiler_params=_cp(("parallel",)),
        name="swa_attn",
    )(sinks, proj, g, cos, sin, k, k, v, v)


def _swa_step_kernel(sink_ref, q_ref, g_ref, cos_ref, sin_ref, kt_ref, vt_ref, kn_ref, vn_ref,
                     o_ref, kto_ref, vto_ref, qn_ref, *, n_new):
    gmat = _group_ones()
    bb, tq, _ = q_ref.shape
    row = _iota((SWA_GROUP * tq, 2 * BLOCK), 0) & (tq - 1)
    col = _iota((SWA_GROUP * tq, 2 * BLOCK), 1)
    mask = (col > row) & (col <= row + WINDOW)
    grp = _iota((SWA_GROUP * tq, 1), 0) // tq
    keep = _iota((KV_WIDTH, BLOCK), 1) < BLOCK - n_new
    pad = jnp.zeros((BLOCK - tq, KV_WIDTH), f32)
    ktb, vtb, knb, vnb = [], [], [], []
    for b in range(bb):
        for j in range(N_SLABS):
            sl = slice(j * LANES, (j + 1) * LANES)
            x = q_ref[b, :, sl]
            xn = x * lax.rsqrt(_head_sum(x * x, gmat) * (1.0 / HEAD_DIM) + NORM_EPS) * g_ref[...]
            qn_ref[b, :, sl] = (xn * cos_ref[...] + _rot_half(xn) * sin_ref[...]) * ATTN_SCALE
        kt, vt = kt_ref[b], vt_ref[b]
        kn = jnp.concatenate([kn_ref[b], pad], axis=0)
        vn = jnp.concatenate([vn_ref[b], pad], axis=0)
        kto_ref[b] = jnp.where(keep, pltpu.roll(kt, BLOCK - n_new, 1), pltpu.roll(kn.T, BLOCK - n_new, 1))
        vto_ref[b] = jnp.where(keep, pltpu.roll(vt, BLOCK - n_new, 1), pltpu.roll(vn.T, BLOCK - n_new, 1))
        ktb.append(kt.astype(bf16))
        vtb.append(vt.astype(bf16))
        knb.append(kn.astype(bf16))
        vnb.append(vn.astype(bf16))
    pairs = [(b, hk) for b in range(bb) for hk in range(SWA_KV_HEADS)]
    hs = [slice(hk * HEAD_DIM, (hk + 1) * HEAD_DIM) for hk in range(SWA_KV_HEADS)]
    q3s = [jnp.concatenate([qn_ref[b, :, (hk * SWA_GROUP + gq) * HEAD_DIM:(hk * SWA_GROUP + gq + 1) * HEAD_DIM]
                            for gq in range(SWA_GROUP)], axis=0).astype(bf16) for b, hk in pairs]
    ss = [jnp.where(mask, jnp.concatenate([_dot(q3, ktb[b][hs[hk], :]), _dot_nt(q3, knb[b][:, hs[hk]])], axis=1),
                    NEG_BIG) for q3, (b, hk) in zip(q3s, pairs)]
    sinks = [jnp.where(grp == 0, sink_ref[hk * SWA_GROUP],
                       jnp.where(grp == 1, sink_ref[hk * SWA_GROUP + 1], sink_ref[hk * SWA_GROUP + 2]))
             for hk in range(SWA_KV_HEADS)]
    ms = [jnp.maximum(jnp.max(s, axis=-1, keepdims=True), sinks[hk]) for s, (b, hk) in zip(ss, pairs)]
    es = [jnp.exp(s - m) for s, m in zip(ss, ms)]
    denoms = [jnp.sum(e, axis=-1, keepdims=True) + jnp.exp(sinks[hk] - m) for e, m, (b, hk) in zip(es, ms, pairs)]
    ebs = [e.astype(bf16) for e in es]
    outs = [(_dot_nt(eb[:, :BLOCK], vtb[b][hs[hk], :]) + _dot(eb[:, BLOCK:], vnb[b][:, hs[hk]])) / d
            for eb, d, (b, hk) in zip(ebs, denoms, pairs)]
    for o, (b, hk) in zip(outs, pairs):
        for gq in range(SWA_GROUP):
            hq = hk * SWA_GROUP + gq
            o_ref[b, :, hq * HEAD_DIM:(hq + 1) * HEAD_DIM] = o[gq * tq:(gq + 1) * tq]


def _swa_step(proj, g_qnorm, sinks, cos, sin, cache_kt, cache_vt, k_new, v_new, n_new, bb):
    bsz, tq, _ = proj.shape
    assert SWA_GROUP == 3 and tq & (tq - 1) == 0
    g = jnp.tile(g_qnorm.reshape(1, HEAD_DIM), (1, HEADS_PER_SLAB))
    cache_spec = pl.BlockSpec((bb, KV_WIDTH, BLOCK), lambda b: (b, 0, 0))
    new_spec = pl.BlockSpec((bb, tq, KV_WIDTH), lambda b: (b, 0, 0))
    tab_spec = pl.BlockSpec((tq, LANES), lambda b: (0, 0))
    return pl.pallas_call(
        functools.partial(_swa_step_kernel, n_new=n_new),
        grid=(bsz // bb,),
        in_specs=[pl.BlockSpec(memory_space=pltpu.SMEM),
                  pl.BlockSpec((bb, tq, MAIN_WIDTH), lambda b: (b, 0, 0)),
                  pl.BlockSpec((1, LANES), lambda b: (0, 0)), tab_spec, tab_spec,
                  cache_spec, cache_spec, new_spec, new_spec],
        out_specs=[pl.BlockSpec((bb, tq, MAIN_WIDTH), lambda b: (b, 0, 0)), cache_spec, cache_spec],
        out_shape=[jax.ShapeDtypeStruct((bsz, tq, MAIN_WIDTH), f32),
                   jax.ShapeDtypeStruct(cache_kt.shape, f32), jax.ShapeDtypeStruct(cache_vt.shape, f32)],
        scratch_shapes=[pltpu.VMEM((bb, tq, MAIN_WIDTH), f32)],
        compiler_params=_cp(("parallel",)),
        name="swa_step",
    )(sinks, proj, g, cos, sin, cache_kt, cache_vt, k_new, v_new)


def _out_mlp_kernel(*refs, tf, with_next):
    if with_next:
        (h_ref, main_ref, mem_ref, woa_ref, wob_ref, g_ref, wup_ref, wdn_ref,
         gkv_ref, wkv_ref, gk_ref, cos_ref, sin_ref, gin_ref, win_ref, o_ref, k_ref, v_ref, p_ref) = refs
    else:
        h_ref, main_ref, mem_ref, woa_ref, wob_ref, g_ref, wup_ref, wdn_ref, o_ref = refs
    h1 = (h_ref[...] + _dot(main_ref[...].astype(bf16), woa_ref[...])
          + _dot(mem_ref[...].astype(bf16), wob_ref[...]))
    hn = (h1 * lax.rsqrt(jnp.mean(h1 * h1, axis=-1, keepdims=True) + NORM_EPS) * g_ref[...]).astype(bf16)
    acc = None
    for j in range(D_FF // tf):
        u = _dot(hn, wup_ref[:, j * tf:(j + 1) * tf])
        u = jnp.square(jnp.maximum(u, 0.0)).astype(bf16)
        d = _dot(u, wdn_ref[j * tf:(j + 1) * tf, :])
        acc = d if acc is None else acc + d
    h2 = h1 + acc
    o_ref[...] = h2
    if with_next:
        xhat = h2 * lax.rsqrt(jnp.mean(h2 * h2, axis=-1, keepdims=True) + NORM_EPS)
        kv = _dot((xhat * gkv_ref[...]).astype(bf16), wkv_ref[...])
        slabs = _head_norm_rope(kv[:, :KV_WIDTH], gk_ref[...], cos_ref[...], sin_ref[...], _group_ones())
        for j, slab in enumerate(slabs):
            k_ref[:, j * LANES:(j + 1) * LANES] = slab
        v_ref[...] = kv[:, KV_WIDTH:]
        p_ref[...] = _dot((xhat * gin_ref[...]).astype(bf16), win_ref[...])


def _out_mlp(h, main, mem_o, w_out_bf16, g_mlp, w_up_bf16, w_down_bf16, tm, tf=1024, nxt=None):
    n = h.shape[0]
    tm = min(tm, n)
    rows = lambda w: pl.BlockSpec((tm, w), lambda i: (i, 0))
    resident = lambda a: pl.BlockSpec(a.shape, lambda i: (0, 0))
    woa, wob = w_out_bf16[:MAIN_WIDTH], w_out_bf16[MAIN_WIDTH:]
    row = lambda x: x.reshape(1, -1)
    args = [h, main, mem_o, woa, wob, row(g_mlp), w_up_bf16, w_down_bf16]
    in_specs = [rows(D_MODEL), rows(MAIN_WIDTH), rows(MEM_WIDTH)] + [resident(a) for a in args[3:]]
    out_specs, out_shape = [rows(D_MODEL)], [jax.ShapeDtypeStruct((n, D_MODEL), f32)]
    if nxt is not None:
        kv_norm, w_kv, k_norm, cos, sin, in_norm, w_in = nxt
        nt = cos.shape[0] // tm
        table = pl.BlockSpec((tm, LANES), lambda i: (i % nt, 0))
        extra = [row(kv_norm), w_kv, jnp.tile(row(k_norm), (1, KV_WIDTH // HEAD_DIM)), cos, sin, row(in_norm), w_in]
        args += extra
        in_specs += [resident(extra[0]), resident(w_kv), resident(extra[2]), table, table,
                     resident(extra[5]), resident(w_in)]
        out_specs += [rows(KV_WIDTH), rows(KV_WIDTH), rows(D_MODEL)]
        out_shape += [jax.ShapeDtypeStruct((n, KV_WIDTH), f32)] * 2 + [jax.ShapeDtypeStruct((n, D_MODEL), f32)]
    res = pl.pallas_call(
        functools.partial(_out_mlp_kernel, tf=tf, with_next=nxt is not None),
        grid=(n // tm,),
        in_specs=in_specs,
        out_specs=out_specs,
        out_shape=out_shape,
        compiler_params=_cp(("parallel",)),
        name="out_mlp",
    )(*args)
    return res if nxt is not None else res[0]


def _rope_tables(pos, reps):
    half = HEAD_DIM // 2
    freqs = jnp.power(ROPE_THETA, -jnp.arange(half, dtype=f32) / half)
    ang = pos.astype(f32)[:, None] * freqs[None, :]
    cos, sin = jnp.cos(ang), jnp.sin(ang)
    cos_h = jnp.concatenate([cos, cos], axis=-1)
    sin_h = jnp.concatenate([-sin, sin], axis=-1)
    return jnp.tile(cos_h, (1, reps)), jnp.tile(sin_h, (1, reps))


def _trunk(x, pos, mem_kt, mem_vt, shift0, wkv0, swa_cache, t_real, W):
    bsz, tp, _ = x.shape
    n = bsz * tp
    prompt = swa_cache is None
    mem_bb = bsz if prompt else STEP_MEM_SEQS
    act_dtype = bf16 if prompt else f32
    tm = ROW_TILE
    flat = lambda a: a.reshape(n, a.shape[-1])
    unflat = lambda a: a.reshape(bsz, tp, a.shape[-1])
    cos2, sin2 = _rope_tables(pos, HEADS_PER_SLAB)

    prep_w = (W["shift_mu"], W["w_w2"], W["w0"], W["w_a2"], W["a0"], W["w_g2"], W["k_k"], W["k_a"], W["r_k"])
    if prompt:
        seqs, q_mem, new_shift = _proj_prep(x, W["norm_mix"][0], W["w_in_a"], shift0, prep_w, PREP_TILE, bf16)
        q_col = 0
        main, new_wkv = _wkv_chunked(*seqs, W["lnx_w"], W["lnx_b"])
    else:
        q_mem = unflat(_norm_matmul(flat(x), W["norm_mix"][0], W["w_in_a"], tm))
        q_col = RWKV_COLS // MEM_WIDTH
        new_shift = q_mem[:, t_real - 1, :RWKV_COLS]
        steps = jnp.transpose(q_mem[:, :t_real, :RWKV_COLS], (1, 0, 2)).reshape(1, t_real * bsz, RWKV_COLS)
        seqs = [s.reshape(t_real, bsz, MAIN_WIDTH) for s in _rwkv_prep(steps, shift0, prep_w)]
        main, new_wkv = _wkv_steps(wkv0, seqs, W["lnx_w"], W["lnx_b"], tp)
    mem_o = _mem_attn(q_mem, q_col, W["mem_qnorm"][0], mem_kt, mem_vt, 0, MEM_Q_TILE, mem_bb, act_dtype)
    tables = (cos2, sin2) if prompt else (jnp.tile(cos2, (tm // tp, 1)), jnp.tile(sin2, (tm // tp, 1)))
    h, k_sh, v_sh, proj = _out_mlp(flat(x), flat(main), flat(mem_o), W["w_out"][0], W["norm_mlp"][0], W["w_up"][0],
                                   W["w_down"][0], tm,
                                   nxt=(W["kv_norm"], W["w_kv"], W["swa_knorm"], *tables, W["norm_mix"][1], W["w_in_b"]))
    k_sh, v_sh, proj = unflat(k_sh), unflat(v_sh), unflat(proj)

    if prompt:
        main = _swa(proj, W["swa_qnorm"], W["sinks"], cos2, sin2, k_sh, v_sh)
        win = min(WINDOW, tp)
        heads = lambda a: a.reshape(bsz, a.shape[1], SWA_KV_HEADS, HEAD_DIM)
        k_state, v_state = heads(k_sh[:, tp - win:]), heads(v_sh[:, tp - win:])
    else:
        ckt, cvt = swa_cache
        main, kt_new, vt_new = _swa_step(proj, W["swa_qnorm"], W["sinks"], cos2, sin2, ckt, cvt, k_sh, v_sh,
                                         t_real, bb=STEP_SWA_SEQS)
        untransposed = lambda a: jnp.transpose(a.reshape(bsz, SWA_KV_HEADS, HEAD_DIM, -1), (0, 3, 1, 2))
        k_state, v_state = untransposed(kt_new), untransposed(vt_new)
    mem_o = _mem_attn(proj, MAIN_WIDTH // MEM_WIDTH, W["mem_qnorm"][1], mem_kt, mem_vt, 1, MEM_Q_TILE, mem_bb, act_dtype)
    y = _out_mlp(h, flat(main), flat(mem_o), W["w_out"][1], W["norm_mlp"][1], W["w_up"][1], W["w_down"][1], tm)
    return unflat(y)[:, :t_real], new_shift[None], new_wkv[None], k_state, v_state


def kernel(x_prompt, x_sample, state_rwkv_shift, state_rwkv_wkv, cache_swa_k, cache_swa_v, cache_mem_k,
           cache_mem_v, mem_prompt, norm_mix, norm_mlp, w_out, w_up, w_down, mem_norm, w_mem_kv, mem_qnorm,
           mem_knorm, w_in_a, shift_mu, w_w2, w0, w_a2, a0, w_g2, k_k, k_a, r_k, lnx_w, lnx_b, w_in_b,
           swa_qnorm, sinks, kv_norm, w_kv, swa_knorm):
    W = dict(norm_mix=norm_mix, norm_mlp=norm_mlp, w_out=w_out.astype(bf16), w_up=w_up.astype(bf16),
             w_down=w_down.astype(bf16), mem_qnorm=mem_qnorm, w_in_a=w_in_a[0].astype(bf16),
             shift_mu=shift_mu[0], w_w2=w_w2[0], w0=w0[0], w_a2=w_a2[0], a0=a0[0], w_g2=w_g2[0],
             k_k=k_k[0], k_a=k_a[0], r_k=r_k[0].reshape(-1), lnx_w=lnx_w[0], lnx_b=lnx_b[0],
             w_in_b=w_in_b[0].astype(bf16), swa_qnorm=swa_qnorm[0], sinks=sinks[0], kv_norm=kv_norm,
             w_kv=w_kv.astype(bf16), swa_knorm=swa_knorm)
    bp, tp, _ = x_prompt.shape
    bs, ts, _ = x_sample.shape
    depth = norm_mix.shape[0]

    mem_flat = mem_prompt.reshape(bp * N_MEM, D_MODEL)
    p_mem_k, p_mem_v = [], []
    for l in range(depth):
        kv = _norm_matmul(mem_flat, mem_norm[l], w_mem_kv[l].astype(bf16), ROW_TILE)
        mk, mv = _kv_post(kv, mem_knorm[l], N_MEM)
        p_mem_k.append(mk.reshape(bp, N_MEM, MEM_WIDTH))
        p_mem_v.append(mv.reshape(bp, N_MEM, MEM_WIDTH))
    transposed = lambda a: jnp.swapaxes(jnp.stack(a), 2, 3)
    y_p, p_shift, p_wkv, p_k, p_v = _trunk(
        x_prompt, jnp.arange(tp), transposed(p_mem_k), transposed(p_mem_v), jnp.zeros((bp, RWKV_COLS), f32),
        None, None, tp, W)
    mem_heads = lambda a: jnp.stack(a).reshape(depth, bp, N_MEM, MEM_HEADS, HEAD_DIM)
    cache_t = lambda a: jnp.transpose(a, (0, 1, 3, 4, 2)).reshape(depth, bs, MEM_WIDTH, N_MEM)

    tpad = -(-ts // SUBLANES) * SUBLANES
    x_s = jnp.pad(x_sample, ((0, 0), (0, tpad - ts), (0, 0)))
    win = cache_swa_k.shape[1]
    assert win == WINDOW
    swa_t = lambda a: jnp.transpose(a, (0, 2, 3, 1)).reshape(bs, KV_WIDTH, win)
    y_s, s_shift, s_wkv, s_k, s_v = _trunk(
        x_s, PAST_LEN + jnp.arange(tpad), cache_t(cache_mem_k), cache_t(cache_mem_v),
        state_rwkv_shift[0], state_rwkv_wkv[0],
        (swa_t(cache_swa_k), swa_t(cache_swa_v)), ts, W)

    return (y_p, y_s, p_shift, p_wkv, p_k, p_v, mem_heads(p_mem_k), mem_heads(p_mem_v),
            s_shift, s_wkv, s_k, s_v)
```

```python
import functools
import math

import jax
import jax.numpy as jnp
from jax import lax
from jax.experimental import pallas as pl
from jax.experimental.pallas import tpu as pltpu

f32 = jnp.float32
bf16 = jnp.bfloat16

D_MODEL = 1024
HEAD_DIM = 64
MEM_HEADS = 4
MEM_WIDTH = MEM_HEADS * HEAD_DIM
MAIN_WIDTH = D_MODEL - MEM_WIDTH
RWKV_HEADS = MAIN_WIDTH // HEAD_DIM
DECAY_LORA = 64
AAA_LORA = 64
GATE_LORA = 128
RWKV_COLS = 3 * MAIN_WIDTH + DECAY_LORA + AAA_LORA + GATE_LORA
A_IN_COLS = RWKV_COLS + MEM_WIDTH
SWA_Q_HEADS = MAIN_WIDTH // HEAD_DIM
SWA_KV_HEADS = 4
SWA_GROUP = SWA_Q_HEADS // SWA_KV_HEADS
KV_WIDTH = SWA_KV_HEADS * HEAD_DIM
WINDOW = 128
BLOCK = 128
N_MEM = 256
D_FF = 4 * D_MODEL
PAST_LEN = 16384
ROPE_THETA = 10000.0
NORM_EPS = 1e-6
LNX_EPS = 6.4e-4
L2_EPS = 1e-12
ATTN_SCALE = HEAD_DIM ** -0.5

LANES = 128
SUBLANES = 8
HEADS_PER_SLAB = LANES // HEAD_DIM
N_SLABS = MAIN_WIDTH // LANES
CHUNK = 64
WKV_CHUNKS_PER_STEP = 16
VMEM_LIMIT = 56 * 1024 * 1024
NEG_BIG = -1e30
ROW_TILE = 512
PREP_TILE = 512
MEM_Q_TILE = 512
MEM_Q_SUB = 256
STEP_MEM_SEQS = 8
STEP_SWA_SEQS = 4


def _cp(sem, vmem=VMEM_LIMIT):
    return pltpu.CompilerParams(dimension_semantics=sem, vmem_limit_bytes=vmem)


def _dot(a, b):
    return jnp.dot(a, b, preferred_element_type=f32)


def _dot_nt(a, b):
    return lax.dot_general(a, b, (((1,), (1,)), ((), ())), preferred_element_type=f32)


def _dot_tn(a, b):
    return lax.dot_general(a, b, (((0,), (0,)), ((), ())), preferred_element_type=f32)


def _iota(shape, dim):
    return lax.broadcasted_iota(jnp.int32, shape, dim)


def _group_ones():
    r = _iota((LANES, LANES), 0)
    c = _iota((LANES, LANES), 1)
    return jnp.where((r < HEAD_DIM) == (c < HEAD_DIM), 1.0, 0.0).astype(f32)


def _head_sum(x, gmat, passes=2):
    w = x.shape[-1]
    gb = gmat.astype(bf16)
    hi = x.astype(bf16)
    slabs = lambda y: [y[:, j * LANES:(j + 1) * LANES] for j in range(w // LANES)]
    parts = [_dot(h, gb) for h in slabs(hi)]
    if passes == 2:
        lo = (x - hi.astype(f32)).astype(bf16)
        parts = [p + _dot(l, gb) for p, l in zip(parts, slabs(lo))]
    return parts[0] if len(parts) == 1 else jnp.concatenate(parts, axis=-1)


def _cumsum_rows(x):
    rows = x.shape[0]
    row = _iota(x.shape, 0)
    step = 1
    while step < rows:
        x = x + jnp.where(row >= step, pltpu.roll(x, step, 0), 0.0)
        step *= 2
    return x


def _sigmoid(x):
    return 1.0 / (1.0 + jnp.exp(-x))


def _rot_half(x):
    lane = _iota(x.shape, 1)
    first = (lane & (HEAD_DIM - 1)) < (HEAD_DIM // 2)
    return jnp.where(first, pltpu.roll(x, LANES - HEAD_DIM // 2, 1), pltpu.roll(x, HEAD_DIM // 2, 1))


def _norm_matmul_kernel(x_ref, g_ref, w_ref, o_ref):
    x = x_ref[...]
    xn = x * lax.rsqrt(jnp.mean(x * x, axis=-1, keepdims=True) + NORM_EPS) * g_ref[...]
    o_ref[...] = _dot(xn.astype(bf16), w_ref[...])


def _norm_matmul(x, g, w_bf16, tm):
    n, d = x.shape
    m = w_bf16.shape[1]
    tm = min(tm, n)
    return pl.pallas_call(
        _norm_matmul_kernel,
        grid=(n // tm,),
        in_specs=[pl.BlockSpec((tm, d), lambda i: (i, 0)),
                  pl.BlockSpec((1, d), lambda i: (0, 0)),
                  pl.BlockSpec((d, m), lambda i: (0, 0))],
        out_specs=pl.BlockSpec((tm, m), lambda i: (i, 0)),
        out_shape=jax.ShapeDtypeStruct((n, m), f32),
        compiler_params=_cp(("parallel",)),
        name="norm_matmul",
    )(x, g.reshape(1, d), w_bf16)


def _head_norm_rope(x, g, cos, sin, gmat):
    xn = x * lax.rsqrt(_head_sum(x * x, gmat) * (1.0 / HEAD_DIM) + NORM_EPS) * g
    slabs = [xn[:, j * LANES:(j + 1) * LANES] for j in range(x.shape[-1] // LANES)]
    return [xs * cos + _rot_half(xs) * sin for xs in slabs]


def _kv_post_kernel(kv_ref, g_ref, k_ref, v_ref):
    gmat = _group_ones()
    x = kv_ref[:, :KV_WIDTH]
    k_ref[...] = x * lax.rsqrt(_head_sum(x * x, gmat) * (1.0 / HEAD_DIM) + NORM_EPS) * g_ref[...]
    v_ref[...] = kv_ref[:, KV_WIDTH:]


def _kv_post(kv, g_head, tm):
    n = kv.shape[0]
    tm = min(tm, n)
    g = jnp.tile(g_head.reshape(1, HEAD_DIM), (1, KV_WIDTH // HEAD_DIM))
    return pl.pallas_call(
        _kv_post_kernel,
        grid=(n // tm,),
        in_specs=[pl.BlockSpec((tm, 2 * KV_WIDTH), lambda i: (i, 0)),
                  pl.BlockSpec((1, KV_WIDTH), lambda i: (0, 0))],
        out_specs=[pl.BlockSpec((tm, KV_WIDTH), lambda i: (i, 0)),
                   pl.BlockSpec((tm, KV_WIDTH), lambda i: (i, 0))],
        out_shape=[jax.ShapeDtypeStruct((n, KV_WIDTH), f32)] * 2,
        compiler_params=_cp(("parallel",)),
        name="kv_post",
    )(kv, g)


def _time_mix_inputs(pf, prev, consts, outs):
    mu_ref, ww2_ref, w0_ref, wa2_ref, a0_ref, wg2_ref, kk_ref, ka_ref, rk_ref = consts
    r_o, lw_o, k_o, v_o, a_o, b_o, bonus_o, g_o = outs
    gmat = _group_ones()
    ps = pf + (prev - pf) * mu_ref[...]
    i1, i2, i3 = MAIN_WIDTH, 2 * MAIN_WIDTH, 3 * MAIN_WIDTH
    i4, i5 = i3 + DECAY_LORA, i3 + DECAY_LORA + AAA_LORA
    r, k, v = ps[:, :i1], ps[:, i1:i2], ps[:, i2:i3]
    wd, ad, gd = ps[:, i3:i4], ps[:, i4:i5], ps[:, i5:]
    lora = lambda x, w_ref: _dot(x.astype(bf16), w_ref[...].astype(bf16))
    lw = -math.exp(-0.5) * _sigmoid(w0_ref[...] + lora(jnp.tanh(wd), ww2_ref))
    a_sig = _sigmoid(a0_ref[...] + lora(ad, wa2_ref))
    g = lora(_sigmoid(gd), wg2_ref)
    kk = k * kk_ref[...]
    kk = kk * jnp.minimum(lax.rsqrt(_head_sum(kk * kk, gmat, passes=1)), 1.0 / L2_EPS)
    k2 = k * (1.0 + (a_sig - 1.0) * ka_ref[...])
    r_o[0] = r.astype(r_o.dtype)
    lw_o[0] = lw
    k_o[0] = k2.astype(k_o.dtype)
    v_o[0] = v.astype(v_o.dtype)
    a_o[0] = (-kk).astype(a_o.dtype)
    b_o[0] = (kk * a_sig).astype(b_o.dtype)
    bonus_o[0] = (_head_sum(r * k2 * rk_ref[...], gmat, passes=1) * v).astype(bonus_o.dtype)
    g_o[0] = g.astype(g_o.dtype)


def _rwkv_prep_kernel(p_ref, shift_ref, *refs):
    pf = p_ref[0]
    prev = jnp.concatenate([shift_ref[0], pf[:pf.shape[0] - shift_ref.shape[1]]], axis=0)
    _time_mix_inputs(pf, prev, refs[:9], refs[9:])


def _proj_prep_kernel(x_ref, gx_ref, w_ref, shift_ref, *refs):
    consts, outs, (q_o, last_o, carry_ref) = refs[:9], refs[9:17], refs[17:]
    i = pl.program_id(1)
    x = x_ref[0]
    xn = x * lax.rsqrt(jnp.mean(x * x, axis=-1, keepdims=True) + NORM_EPS) * gx_ref[...]
    proj = _dot(xn.astype(bf16), w_ref[...])
    q_o[0] = proj[:, RWKV_COLS:]
    pf = proj[:, :RWKV_COLS]
    first_prev = jnp.where(i == 0, shift_ref[0], carry_ref[...])
    prev = jnp.where(_iota(pf.shape, 0) == 0, first_prev, pltpu.roll(pf, 1, 0))
    last = pf[pf.shape[0] - 1:, :]
    carry_ref[...] = last
    last_o[0] = last
    _time_mix_inputs(pf, prev, consts, outs)


def _proj_prep(x, g_norm, w_in_bf16, shift_prev, prep_w, tt, seq_dtype):
    bsz, t, _ = x.shape
    row = lambda a: a.reshape(1, -1)
    full = lambda a: pl.BlockSpec(a.shape, lambda b, i: (0,) * a.ndim)
    mu, w_w2, w0, w_a2, a0, w_g2, k_k, k_a, r_k = prep_w
    consts = [row(mu), w_w2, row(w0), w_a2, row(a0), w_g2, row(k_k), row(k_a), row(r_k)]
    tile = lambda w: pl.BlockSpec((1, tt, w), lambda b, i: (b, i, 0))
    per_seq = pl.BlockSpec((1, 1, RWKV_COLS), lambda b, i: (b, 0, 0))
    res = pl.pallas_call(
        _proj_prep_kernel,
        grid=(bsz, t // tt),
        in_specs=[tile(D_MODEL), full(row(g_norm)), full(w_in_bf16), per_seq] + [full(c) for c in consts],
        out_specs=[tile(MAIN_WIDTH)] * 8 + [tile(MEM_WIDTH), per_seq],
        out_shape=[jax.ShapeDtypeStruct((bsz, t, MAIN_WIDTH), f32 if j == 1 else seq_dtype) for j in range(8)]
                  + [jax.ShapeDtypeStruct((bsz, t, MEM_WIDTH), f32), jax.ShapeDtypeStruct((bsz, 1, RWKV_COLS), f32)],
        scratch_shapes=[pltpu.VMEM((1, RWKV_COLS), f32)],
        compiler_params=_cp(("parallel", "arbitrary")),
        name="proj_prep",
    )(x, row(g_norm), w_in_bf16, shift_prev.reshape(bsz, 1, RWKV_COLS), *consts)
    return res[:8], res[8], res[9].reshape(bsz, RWKV_COLS)


def _rwkv_prep(steps, shift_prev, prep_w):
    _, t, _ = steps.shape
    row = lambda a: a.reshape(1, -1)
    full = lambda a: pl.BlockSpec(a.shape, lambda i: (0,) * a.ndim)
    mu, w_w2, w0, w_a2, a0, w_g2, k_k, k_a, r_k = prep_w
    consts = [row(mu), w_w2, row(w0), w_a2, row(a0), w_g2, row(k_k), row(k_a), row(r_k)]
    shift_rows = shift_prev[None]
    return pl.pallas_call(
        _rwkv_prep_kernel,
        grid=(1,),
        in_specs=[full(steps), full(shift_rows)] + [full(c) for c in consts],
        out_specs=[pl.BlockSpec((1, t, MAIN_WIDTH), lambda i: (0, 0, 0))] * 8,
        out_shape=[jax.ShapeDtypeStruct((1, t, MAIN_WIDTH), f32)] * 8,
        compiler_params=_cp(("arbitrary",)),
        name="rwkv_prep",
    )(steps, shift_rows, *consts)


def _wkv_chunk_kernel(r_ref, lw_ref, k_ref, v_ref, a_ref, b_ref, bonus_ref, g_ref, lnw_ref, lnb_ref,
                      o_ref, s_ref, h_ref, qt_ref, yv_ref, pp_ref, hv_ref, tc_ref):
    i = pl.program_id(2)
    c = CHUNK
    nc = r_ref.shape[1] // c
    n = range(nc)

    @pl.when(i == 0)
    def _():
        h_ref[...] = jnp.zeros_like(h_ref)
        qt_ref[...] = jnp.zeros_like(qt_ref)
        yv_ref[...] = jnp.zeros_like(yv_ref)
        pp_ref[...] = jnp.zeros_like(pp_ref)
        hv_ref[...] = jnp.zeros_like(hv_ref)
        tc_ref[...] = jnp.zeros_like(tc_ref)

    gmat = _group_ones()
    bd = gmat > 0.5
    row = _iota((LANES, LANES), 0)
    col = _iota((LANES, LANES), 1)
    t_row = row & (c - 1)
    s_col = col & (c - 1)
    m_strict = s_col < t_row
    m_incl = s_col <= t_row
    top = row < c
    lane_lo = _iota((1, LANES), 1) < HEAD_DIM
    eye = row == col
    zeros = jnp.zeros((c, LANES), f32)
    mm = lambda x: x.astype(bf16)
    gb = gmat.astype(bf16)
    sls = [slice(j * c, (j + 1) * c) for j in n]

    def pending_state():
        state = {"h": h_ref[...], "ys": []}

        def state_step(j):
            hb = mm(state["h"])
            state["ys"].append(_dot(qt_ref[j], hb) + yv_ref[j])
            state["h"] = state["h"] * tc_ref[j] + _dot(pp_ref[j], hb) + hv_ref[j]

        def finish():
            h_ref[...] = state["h"]
            means = [_dot(mm(y), gb) * (1.0 / HEAD_DIM) for y in state["ys"]]
            ycs = [state["ys"][j] - means[j] for j in n]
            vrs = [_dot(mm(yc * yc), gb) * (1.0 / HEAD_DIM) for yc in ycs]
            outs = []
            for j in n:
                yn = ycs[j] * lax.rsqrt(vrs[j] + LNX_EPS) * lnw_ref[...] + lnb_ref[...]
                outs.append(((yn + bonus_ref[0, sls[j], :].astype(f32))
                             * g_ref[0, sls[j], :].astype(f32)).astype(o_ref.dtype))
            return outs

        return state_step, finish

    def chunk_group(js):
        st = {}

        def decay_and_blocks():
            st["vs"], st["tots"], st["ats"], st["rts"], st["bkts"], gms = [], [], [], [], [], []
            for j in js:
                sl = sls[j]
                lw = lw_ref[0, sl, :]
                cum = _cumsum_rows(lw)
                tot = cum[c - 1:c, :]
                b, k = b_ref[0, sl, :].astype(f32), k_ref[0, sl, :].astype(f32)
                e_neg, e_rem = jnp.exp(-cum), jnp.exp(tot - cum)
                at = a_ref[0, sl, :].astype(f32) * jnp.exp(cum - lw)
                rt = r_ref[0, sl, :].astype(f32) * jnp.exp(cum)
                bk = mm(jnp.concatenate([b * e_neg, k * e_neg], axis=0))
                lhs = mm(jnp.concatenate([jnp.where(lane_lo, at, 0.0), jnp.where(lane_lo, 0.0, at),
                                          jnp.where(lane_lo, rt, 0.0), jnp.where(lane_lo, 0.0, rt)], axis=0))
                gms.append(_dot_nt(lhs, bk))
                st["vs"].append(v_ref[0, sl, :].astype(f32))
                st["tots"].append(tot)
                st["ats"].append(at)
                st["rts"].append(rt)
                st["bkts"].append(mm(jnp.concatenate([b * e_rem, k * e_rem], axis=0)))
            st["gms"] = gms

        def first_products():
            m = range(len(js))
            aaks = [jnp.where(m_strict, gm[:LANES], 0.0) for gm in st["gms"]]
            st["arks"] = [mm(jnp.where(m_incl, gm[LANES:], 0.0)) for gm in st["gms"]]
            aak_sws = [pltpu.roll(aak, HEAD_DIM, 1) for aak in aaks]
            st["ms"] = [jnp.where(bd, jnp.where(top, aaks[q], aak_sws[q]), 0.0) for q in m]
            aks = [mm(jnp.where(bd, jnp.where(top, aak_sws[q], aaks[q]), 0.0)) for q in m]
            vvs = [mm(jnp.concatenate([pltpu.roll(v, HEAD_DIM, 1)] * 2, axis=0)) for v in st["vs"]]
            akvs = [_dot(aks[q], vvs[q]) for q in m]
            st["xs"] = [jnp.where(bd, jnp.concatenate([st["ats"][q]] * 2, axis=0), akvs[q]) for q in m]

        def square(last):
            def stage():
                m = range(len(js))
                ms, xs = st["ms"], st["xs"]
                if not last:
                    zs = [_dot(mm(ms[q]), mm(jnp.concatenate([xs[q], ms[q]], axis=1))) for q in m]
                    st["xs"] = [xs[q] + zs[q][:, :LANES] for q in m]
                    st["ms"] = [zs[q][:, LANES:] for q in m]
                else:
                    zs = [_dot(mm(ms[q]), mm(xs[q])) for q in m]
                    st["xs"] = [xs[q] + zs[q] for q in m]
            return stage

        def last_products():
            m = range(len(js))
            rhss = []
            for q in m:
                x = st["xs"][q]
                ta = jnp.where(lane_lo, x[:c], x[c:])
                uv = pltpu.roll(jnp.where(lane_lo, x[c:], x[:c]), HEAD_DIM, 1)
                rhss.append(mm(jnp.concatenate([jnp.concatenate([ta, uv], axis=1),
                                                jnp.concatenate([zeros, st["vs"][q]], axis=1)], axis=0)))
            st["z2s"] = [_dot(st["arks"][q], rhss[q]) for q in m]
            st["pzs"] = [_dot_tn(st["bkts"][q], rhss[q]) for q in m]

        def make_pending():
            for q, j in enumerate(js):
                z2, pz = st["z2s"][q], st["pzs"][q]
                qt_ref[j] = mm(st["rts"][q] + jnp.where(lane_lo, z2[:c, :LANES], z2[c:, :LANES]))
                yv_ref[j] = jnp.where(lane_lo, z2[:c, LANES:], z2[c:, LANES:])
                pp_ref[j] = mm(jnp.where(bd, pz[:, :LANES], 0.0))
                hv_ref[j] = jnp.where(bd, pz[:, LANES:], 0.0)
                tc_ref[j] = jnp.sum(jnp.where(eye, jnp.exp(st["tots"][q]), 0.0), axis=1, keepdims=True)

        return ([decay_and_blocks, first_products] + [square(False)] * 5 + [square(True), last_products], make_pending)

    @pl.when(i < pl.num_programs(2) - 1)
    def _():
        half = nc // 2
        (stages_a, pend_a), (stages_b, pend_b) = chunk_group(list(range(half))), chunk_group(list(range(half, nc)))
        lag = 4
        state_step, finish = pending_state()
        outs = None
        for slot in range(max(len(stages_a) + lag, nc + 1)):
            if slot < nc:
                state_step(slot)
            if slot == nc:
                outs = finish()
            if slot < len(stages_a):
                stages_a[slot]()
            if 0 <= slot - lag < len(stages_b):
                stages_b[slot - lag]()
        for j in n:
            o_ref[0, sls[j], :] = outs[j]
        pend_a()
        pend_b()

    @pl.when(i == pl.num_programs(2) - 1)
    def _():
        state_step, finish = pending_state()
        for j in n:
            state_step(j)
        outs = finish()
        for j in n:
            o_ref[0, sls[j], :] = outs[j]

    @pl.when(i == pl.num_programs(2) - 1)
    def _():
        s_ref[0, 0] = h_ref[...].T


def _wkv_chunked(r, lw, k, v, a, b, bonus, g, lnx_w, lnx_b):
    bsz, t, _ = r.shape
    nc = WKV_CHUNKS_PER_STEP
    tt = nc * CHUNK
    nt = t // tt
    assert t % tt == 0
    cur = pl.BlockSpec((1, tt, LANES), lambda bb, hp, i: (bb, jnp.minimum(i, nt - 1), hp))
    pend = pl.BlockSpec((1, tt, LANES), lambda bb, hp, i: (bb, jnp.maximum(i - 1, 0), hp))
    vec = pl.BlockSpec((1, LANES), lambda bb, hp, i: (0, hp))
    out, st = pl.pallas_call(
        _wkv_chunk_kernel,
        grid=(bsz, N_SLABS, nt + 1),
        in_specs=[cur] * 6 + [pend, pend, vec, vec],
        out_specs=[pend, pl.BlockSpec((1, 1, LANES, LANES), lambda bb, hp, i: (bb, hp, 0, 0))],
        out_shape=[jax.ShapeDtypeStruct((bsz, t, MAIN_WIDTH), bf16),
                   jax.ShapeDtypeStruct((bsz, N_SLABS, LANES, LANES), f32)],
        scratch_shapes=[pltpu.VMEM((LANES, LANES), f32),
                        pltpu.VMEM((nc, CHUNK, LANES), bf16), pltpu.VMEM((nc, CHUNK, LANES), f32),
                        pltpu.VMEM((nc, LANES, LANES), bf16), pltpu.VMEM((nc, LANES, LANES), f32),
                        pltpu.VMEM((nc, LANES, 1), f32)],
        compiler_params=_cp(("parallel", "parallel", "arbitrary")),
        name="wkv_chunked",
    )(r, lw, k, v, a, b, bonus, g, lnx_w.reshape(1, -1), lnx_b.reshape(1, -1))
    st = jnp.stack([st[:, :, :HEAD_DIM, :HEAD_DIM], st[:, :, HEAD_DIM:, HEAD_DIM:]], axis=2)
    return out, st.reshape(bsz, RWKV_HEADS, HEAD_DIM, HEAD_DIM)


def _wkv_steps_kernel(s_ref, seq_ref, lnw_ref, lnb_ref, o_ref, so_ref, w_ref, y_ref, *, n_steps):
    i_r, i_lw, i_k, i_v, i_a, i_b, i_bonus, i_g = range(8)
    w_ref[...] = jnp.exp(seq_ref[i_lw])
    sub = _iota((SUBLANES, LANES), 0)

    def group(vg, carry):
        v0 = pl.multiple_of(vg * SUBLANES, SUBLANES)
        v_rows = [seq_ref[i_v, t, pl.ds(v0, SUBLANES), :] for t in range(n_steps)]
        ys = [jnp.zeros((SUBLANES, LANES), f32) for _ in range(n_steps)]
        for j in range(SUBLANES):
            sv = s_ref[0, v0 + j]
            for t in range(n_steps):
                sa = jnp.sum(sv * seq_ref[i_a, t], axis=0, keepdims=True)
                sv = sv * w_ref[t] + sa * seq_ref[i_b, t] + v_rows[t][j:j + 1, :] * seq_ref[i_k, t]
                y = jnp.sum(sv * seq_ref[i_r, t], axis=0, keepdims=True)
                ys[t] = jnp.where(sub == j, y, ys[t])
            so_ref[0, v0 + j] = sv
        for t in range(n_steps):
            y_ref[t, pl.ds(v0, SUBLANES), :] = ys[t]
        return carry

    lax.fori_loop(0, HEAD_DIM // SUBLANES, group, 0)
    y = y_ref[...]
    mean = jnp.mean(y, axis=1, keepdims=True)
    yc = y - mean
    var = jnp.mean(yc * yc, axis=1, keepdims=True)
    yn = yc * lax.rsqrt(var + LNX_EPS) * lnw_ref[...] + lnb_ref[...]
    o_ref[...] = (yn + seq_ref[i_bonus]) * seq_ref[i_g]


def _wkv_steps(state, seqs, lnx_w, lnx_b, tp):
    n_steps, bsz, _ = seqs[0].shape
    assert bsz % LANES == 0
    state_t = jnp.transpose(state, (1, 2, 3, 0))
    seq_t = jnp.transpose(jnp.stack(seqs), (0, 1, 3, 2))
    lanes = lambda x: jnp.broadcast_to(x.reshape(MAIN_WIDTH, 1), (MAIN_WIDTH, LANES))
    sspec = pl.BlockSpec((1, HEAD_DIM, HEAD_DIM, LANES), lambda h, bi: (h, 0, 0, bi))
    vec = pl.BlockSpec((HEAD_DIM, LANES), lambda h, bi: (h, 0))
    ospec = pl.BlockSpec((n_steps, HEAD_DIM, LANES), lambda h, bi: (0, h, bi))
    out, new_state = pl.pallas_call(
        functools.partial(_wkv_steps_kernel, n_steps=n_steps),
        grid=(RWKV_HEADS, bsz // LANES),
        in_specs=[sspec, pl.BlockSpec((8, n_steps, HEAD_DIM, LANES), lambda h, bi: (0, 0, h, bi)), vec, vec],
        out_specs=[ospec, sspec],
        out_shape=[jax.ShapeDtypeStruct((n_steps, MAIN_WIDTH, bsz), f32),
                   jax.ShapeDtypeStruct(state_t.shape, f32)],
        scratch_shapes=[pltpu.VMEM((n_steps, HEAD_DIM, LANES), f32), pltpu.VMEM((n_steps, HEAD_DIM, LANES), f32)],
        compiler_params=_cp(("parallel", "parallel")),
        name="wkv_steps",
    )(state_t, seq_t, lanes(lnx_w), lanes(lnx_b))
    out = jnp.pad(jnp.transpose(out, (2, 0, 1)), ((0, 0), (0, tp - n_steps), (0, 0)))
    return out, jnp.transpose(new_state, (3, 0, 1, 2))


def _mem_attn_kernel(q_ref, g_ref, kt_ref, vt_ref, o_ref, *, sub):
    gmat = _group_ones()
    bb, tq, _ = q_ref.shape
    head = _iota((1, MEM_WIDTH), 1) // HEAD_DIM
    tiles = [(b, pl.ds(r * sub, sub)) for b in range(bb) for r in range(tq // sub)]
    qs = []
    for b, rows in tiles:
        q = q_ref[b, rows, :]
        qn = q * lax.rsqrt(_head_sum(q * q, gmat, passes=1) * (1.0 / HEAD_DIM) + NORM_EPS) * (g_ref[...] * ATTN_SCALE)
        qs.append(jnp.concatenate([jnp.where(head == h, qn, 0.0) for h in range(MEM_HEADS)], axis=0).astype(bf16))
    ss = [_dot(q4, kt_ref[0, b].astype(bf16)) for q4, (b, _) in zip(qs, tiles)]
    es = [jnp.exp(s - jnp.max(s, axis=-1, keepdims=True)) for s in ss]
    o4s = [_dot_nt(e.astype(bf16), vt_ref[0, b].astype(bf16)) / jnp.sum(e, axis=-1, keepdims=True)
           for e, (b, _) in zip(es, tiles)]
    for o4, (b, rows) in zip(o4s, tiles):
        o = jnp.zeros((sub, MEM_WIDTH), f32)
        for h in range(MEM_HEADS):
            o = o + jnp.where(head == h, o4[h * sub:(h + 1) * sub], 0.0)
        o_ref[b, rows, :] = o.astype(o_ref.dtype)


def _mem_attn(proj, col_block, g_qnorm, mem_kt, mem_vt, layer, tq, bb, out_dtype):
    bsz, t, _ = proj.shape
    tq = min(tq, t)
    g = jnp.tile(g_qnorm.reshape(1, HEAD_DIM), (1, MEM_HEADS))
    kv_spec = pl.BlockSpec((1, bb, MEM_WIDTH, N_MEM), lambda b, i: (layer, b, 0, 0))
    return pl.pallas_call(
        functools.partial(_mem_attn_kernel, sub=min(tq, MEM_Q_SUB)),
        grid=(bsz // bb, t // tq),
        in_specs=[pl.BlockSpec((bb, tq, MEM_WIDTH), lambda b, i: (b, i, col_block)),
                  pl.BlockSpec((1, MEM_WIDTH), lambda b, i: (0, 0)),
                  kv_spec, kv_spec],
        out_specs=pl.BlockSpec((bb, tq, MEM_WIDTH), lambda b, i: (b, i, 0)),
        out_shape=jax.ShapeDtypeStruct((bsz, t, MEM_WIDTH), out_dtype),
        compiler_params=_cp(("parallel", "parallel")),
        name="mem_attn",
    )(proj, g, mem_kt, mem_vt)


def _swa_kernel(sink_ref, q_ref, g_ref, cos_ref, sin_ref, kp_ref, kc_ref, vp_ref, vc_ref, o_ref, qn_ref):
    i = pl.program_id(0)
    gmat = _group_ones()
    bb, tq, _ = q_ref.shape
    keys, vals = [], []
    for b in range(bb):
        for j in range(N_SLABS):
            sl = slice(j * LANES, (j + 1) * LANES)
            x = q_ref[b, :, sl]
            xn = x * lax.rsqrt(_head_sum(x * x, gmat, passes=1) * (1.0 / HEAD_DIM) + NORM_EPS) * g_ref[...]
            qn_ref[b, :, sl] = (xn * cos_ref[...] + _rot_half(xn) * sin_ref[...]) * ATTN_SCALE
        keys.append(jnp.concatenate([kp_ref[b], kc_ref[b]], axis=0).astype(bf16))
        vals.append(jnp.concatenate([vp_ref[b], vc_ref[b]], axis=0).astype(bf16))
    rows = SWA_GROUP * tq
    row = _iota((rows, 2 * BLOCK), 0) & (tq - 1)
    col = _iota((rows, 2 * BLOCK), 1)
    mask = (col > row) & (col <= row + WINDOW) & ((i > 0) | (col >= BLOCK))
    grp = _iota((rows, 1), 0) // tq
    pairs = [(b, hk) for b in range(bb) for hk in range(SWA_KV_HEADS)]
    hs = [slice(hk * HEAD_DIM, (hk + 1) * HEAD_DIM) for hk in range(SWA_KV_HEADS)]
    q3s = [jnp.concatenate([qn_ref[b, :, (hk * SWA_GROUP + gq) * HEAD_DIM:(hk * SWA_GROUP + gq + 1) * HEAD_DIM]
                            for gq in range(SWA_GROUP)], axis=0).astype(bf16) for b, hk in pairs]
    ss = [jnp.where(mask, _dot_nt(q3, keys[b][:, hs[hk]]), NEG_BIG) for q3, (b, hk) in zip(q3s, pairs)]
    sinks = [jnp.where(grp == 0, sink_ref[hk * SWA_GROUP],
                       jnp.where(grp == 1, sink_ref[hk * SWA_GROUP + 1], sink_ref[hk * SWA_GROUP + 2]))
             for hk in range(SWA_KV_HEADS)]
    ms = [jnp.maximum(jnp.max(s, axis=-1, keepdims=True), sinks[hk]) for s, (b, hk) in zip(ss, pairs)]
    es = [jnp.exp(s - m) for s, m in zip(ss, ms)]
    denoms = [jnp.sum(e, axis=-1, keepdims=True) + jnp.exp(sinks[hk] - m) for e, m, (b, hk) in zip(es, ms, pairs)]
    outs = [_dot(e.astype(bf16), vals[b][:, hs[hk]]) / d for e, d, (b, hk) in zip(es, denoms, pairs)]
    for o, (b, hk) in zip(outs, pairs):
        for gq in range(SWA_GROUP):
            hq = hk * SWA_GROUP + gq
            o_ref[b, :, hq * HEAD_DIM:(hq + 1) * HEAD_DIM] = o[gq * tq:(gq + 1) * tq].astype(o_ref.dtype)


def _swa(proj, g_qnorm, sinks, cos, sin, k, v):
    bsz, t, _ = proj.shape
    assert SWA_GROUP == 3
    tq = BLOCK
    g = jnp.tile(g_qnorm.reshape(1, HEAD_DIM), (1, HEADS_PER_SLAB))
    prev = pl.BlockSpec((bsz, BLOCK, KV_WIDTH), lambda i: (0, jnp.maximum(i - 1, 0), 0))
    cur = pl.BlockSpec((bsz, BLOCK, KV_WIDTH), lambda i: (0, i, 0))
    return pl.pallas_call(
        _swa_kernel,
        grid=(t // tq,),
        in_specs=[pl.BlockSpec(memory_space=pltpu.SMEM),
                  pl.BlockSpec((bsz, tq, MAIN_WIDTH), lambda i: (0, i, 0)),
                  pl.BlockSpec((1, LANES), lambda i: (0, 0)),
                  pl.BlockSpec((tq, LANES), lambda i: (i, 0)),
                  pl.BlockSpec((tq, LANES), lambda i: (i, 0)),
                  prev, cur, prev, cur],
        out_specs=pl.BlockSpec((bsz, tq, MAIN_WIDTH), lambda i: (0, i, 0)),
        out_shape=jax.ShapeDtypeStruct((bsz, t, MAIN_WIDTH), bf16),
        scratch_shapes=[pltpu.VMEM((bsz, tq, MAIN_WIDTH), f32)],
        compiler_params=_cp(("parallel",)),
        name="swa_attn",
    )(sinks, proj, g, cos, sin, k, k, v, v)


def _swa_step_kernel(sink_ref, q_ref, g_ref, cos_ref, sin_ref, kt_ref, vt_ref, kn_ref, vn_ref,
                     o_ref, kto_ref, vto_ref, qn_ref, *, n_new):
    gmat = _group_ones()
    bb, tq, _ = q_ref.shape
    row = _iota((SWA_GROUP * tq, 2 * BLOCK), 0) & (tq - 1)
    col = _iota((SWA_GROUP * tq, 2 * BLOCK), 1)
    mask = (col > row) & (col <= row + WINDOW)
    grp = _iota((SWA_GROUP * tq, 1), 0) // tq
    keep = _iota((KV_WIDTH, BLOCK), 1) < BLOCK - n_new
    pad = jnp.zeros((BLOCK - tq, KV_WIDTH), f32)
    ktb, vtb, knb, vnb = [], [], [], []
    for b in range(bb):
        for j in range(N_SLABS):
            sl = slice(j * LANES, (j + 1) * LANES)
            x = q_ref[b, :, sl]
            xn = x * lax.rsqrt(_head_sum(x * x, gmat, passes=1) * (1.0 / HEAD_DIM) + NORM_EPS) * g_ref[...]
            qn_ref[b, :, sl] = (xn * cos_ref[...] + _rot_half(xn) * sin_ref[...]) * ATTN_SCALE
        kt, vt = kt_ref[b], vt_ref[b]
        kn = jnp.concatenate([kn_ref[b], pad], axis=0)
        vn = jnp.concatenate([vn_ref[b], pad], axis=0)
        kto_ref[b] = jnp.where(keep, pltpu.roll(kt, BLOCK - n_new, 1), pltpu.roll(kn.T, BLOCK - n_new, 1))
        vto_ref[b] = jnp.where(keep, pltpu.roll(vt, BLOCK - n_new, 1), pltpu.roll(vn.T, BLOCK - n_new, 1))
        ktb.append(kt.astype(bf16))
        vtb.append(vt.astype(bf16))
        knb.append(kn.astype(bf16))
        vnb.append(vn.astype(bf16))
    pairs = [(b, hk) for b in range(bb) for hk in range(SWA_KV_HEADS)]
    hs = [slice(hk * HEAD_DIM, (hk + 1) * HEAD_DIM) for hk in range(SWA_KV_HEADS)]
    q3s = [jnp.concatenate([qn_ref[b, :, (hk * SWA_GROUP + gq) * HEAD_DIM:(hk * SWA_GROUP + gq + 1) * HEAD_DIM]
                            for gq in range(SWA_GROUP)], axis=0).astype(bf16) for b, hk in pairs]
    ss = [jnp.where(mask, jnp.concatenate([_dot(q3, ktb[b][hs[hk], :]), _dot_nt(q3, knb[b][:, hs[hk]])], axis=1),
                    NEG_BIG) for q3, (b, hk) in zip(q3s, pairs)]
    sinks = [jnp.where(grp == 0, sink_ref[hk * SWA_GROUP],
                       jnp.where(grp == 1, sink_ref[hk * SWA_GROUP + 1], sink_ref[hk * SWA_GROUP + 2]))
             for hk in range(SWA_KV_HEADS)]
    ms = [jnp.maximum(jnp.max(s, axis=-1, keepdims=True), sinks[hk]) for s, (b, hk) in zip(ss, pairs)]
    es = [jnp.exp(s - m) for s, m in zip(ss, ms)]
    denoms = [jnp.sum(e, axis=-1, keepdims=True) + jnp.exp(sinks[hk] - m) for e, m, (b, hk) in zip(es, ms, pairs)]
    ebs = [e.astype(bf16) for e in es]
    outs = [(_dot_nt(eb[:, :BLOCK], vtb[b][hs[hk], :]) + _dot(eb[:, BLOCK:], vnb[b][:, hs[hk]])) / d
            for eb, d, (b, hk) in zip(ebs, denoms, pairs)]
    for o, (b, hk) in zip(outs, pairs):
        for gq in range(SWA_GROUP):
            hq = hk * SWA_GROUP + gq
            o_ref[b, :, hq * HEAD_DIM:(hq + 1) * HEAD_DIM] = o[gq * tq:(gq + 1) * tq]


def _swa_step(proj, g_qnorm, sinks, cos, sin, cache_kt, cache_vt, k_new, v_new, n_new, bb):
    bsz, tq, _ = proj.shape
    assert SWA_GROUP == 3 and tq & (tq - 1) == 0
    g = jnp.tile(g_qnorm.reshape(1, HEAD_DIM), (1, HEADS_PER_SLAB))
    cache_spec = pl.BlockSpec((bb, KV_WIDTH, BLOCK), lambda b: (b, 0, 0))
    new_spec = pl.BlockSpec((bb, tq, KV_WIDTH), lambda b: (b, 0, 0))
    tab_spec = pl.BlockSpec((tq, LANES), lambda b: (0, 0))
    return pl.pallas_call(
        functools.partial(_swa_step_kernel, n_new=n_new),
        grid=(bsz // bb,),
        in_specs=[pl.BlockSpec(memory_space=pltpu.SMEM),
                  pl.BlockSpec((bb, tq, MAIN_WIDTH), lambda b: (b, 0, 0)),
                  pl.BlockSpec((1, LANES), lambda b: (0, 0)), tab_spec, tab_spec,
                  cache_spec, cache_spec, new_spec, new_spec],
        out_specs=[pl.BlockSpec((bb, tq, MAIN_WIDTH), lambda b: (b, 0, 0)), cache_spec, cache_spec],
        out_shape=[jax.ShapeDtypeStruct((bsz, tq, MAIN_WIDTH), f32),
                   jax.ShapeDtypeStruct(cache_kt.shape, f32), jax.ShapeDtypeStruct(cache_vt.shape, f32)],
        scratch_shapes=[pltpu.VMEM((bb, tq, MAIN_WIDTH), f32)],
        compiler_params=_cp(("parallel",)),
        name="swa_step",
    )(sinks, proj, g, cos, sin, cache_kt, cache_vt, k_new, v_new)


def _out_mlp_kernel(*refs, tf, with_next):
    if with_next:
        (h_ref, main_ref, mem_ref, wo_ref, g_ref, wup_ref, wdn_ref,
         gkv_ref, wkv_ref, gk_ref, cos_ref, sin_ref, gin_ref, win_ref, o_ref, k_ref, v_ref, p_ref) = refs
    else:
        h_ref, main_ref, mem_ref, wo_ref, g_ref, wup_ref, wdn_ref, o_ref = refs
    h1 = (h_ref[...] + _dot(main_ref[...].astype(bf16), wo_ref[0, :MAIN_WIDTH, :])
          + _dot(mem_ref[...].astype(bf16), wo_ref[0, MAIN_WIDTH:, :]))
    hn = (h1 * lax.rsqrt(jnp.mean(h1 * h1, axis=-1, keepdims=True) + NORM_EPS) * g_ref[...]).astype(bf16)
    acc = None
    for j in range(D_FF // tf):
        u = _dot(hn, wup_ref[0, :, j * tf:(j + 1) * tf])
        u = jnp.square(jnp.maximum(u, 0.0)).astype(bf16)
        d = _dot(u, wdn_ref[0, j * tf:(j + 1) * tf, :])
        acc = d if acc is None else acc + d
    h2 = h1 + acc
    o_ref[...] = h2
    if with_next:
        xhat = h2 * lax.rsqrt(jnp.mean(h2 * h2, axis=-1, keepdims=True) + NORM_EPS)
        kv = _dot((xhat * gkv_ref[...]).astype(bf16), wkv_ref[...])
        slabs = _head_norm_rope(kv[:, :KV_WIDTH], gk_ref[...], cos_ref[...], sin_ref[...], _group_ones())
        for j, slab in enumerate(slabs):
            k_ref[:, j * LANES:(j + 1) * LANES] = slab
        v_ref[...] = kv[:, KV_WIDTH:]
        p_ref[...] = _dot((xhat * gin_ref[...]).astype(bf16), win_ref[...])


def _out_mlp(h, main, mem_o, layer, w_out_bf16, g_mlp, w_up_bf16, w_down_bf16, tm, tf=1024, nxt=None):
    n = h.shape[0]
    tm = min(tm, n)
    rows = lambda w: pl.BlockSpec((tm, w), lambda i: (i, 0))
    resident = lambda a: pl.BlockSpec(a.shape, lambda i: (0, 0))
    of_layer = lambda a: pl.BlockSpec((1,) + a.shape[1:], lambda i: (layer, 0, 0))
    row = lambda x: x.reshape(1, -1)
    args = [h, main, mem_o, w_out_bf16, row(g_mlp), w_up_bf16, w_down_bf16]
    in_specs = [rows(D_MODEL), rows(MAIN_WIDTH), rows(MEM_WIDTH), of_layer(w_out_bf16), resident(args[4]),
                of_layer(w_up_bf16), of_layer(w_down_bf16)]
    out_specs, out_shape = [rows(D_MODEL)], [jax.ShapeDtypeStruct((n, D_MODEL), f32)]
    if nxt is not None:
        kv_norm, w_kv, k_norm, cos, sin, in_norm, w_in = nxt
        nt = cos.shape[0] // tm
        table = pl.BlockSpec((tm, LANES), lambda i: (i % nt, 0))
        extra = [row(kv_norm), w_kv, jnp.tile(row(k_norm), (1, KV_WIDTH // HEAD_DIM)), cos, sin, row(in_norm), w_in]
        args += extra
        in_specs += [resident(extra[0]), resident(w_kv), resident(extra[2]), table, table,
                     resident(extra[5]), resident(w_in)]
        out_specs += [rows(KV_WIDTH), rows(KV_WIDTH), rows(D_MODEL)]
        out_shape += [jax.ShapeDtypeStruct((n, KV_WIDTH), f32)] * 2 + [jax.ShapeDtypeStruct((n, D_MODEL), f32)]
    res = pl.pallas_call(
        functools.partial(_out_mlp_kernel, tf=tf, with_next=nxt is not None),
        grid=(n // tm,),
        in_specs=in_specs,
        out_specs=out_specs,
        out_shape=out_shape,
        compiler_params=_cp(("parallel",)),
        name="out_mlp",
    )(*args)
    return res if nxt is not None else res[0]


def _rope_tables(pos, reps):
    half = HEAD_DIM // 2
    freqs = jnp.power(ROPE_THETA, -jnp.arange(half, dtype=f32) / half)
    ang = pos.astype(f32)[:, None] * freqs[None, :]
    cos, sin = jnp.cos(ang), jnp.sin(ang)
    cos_h = jnp.concatenate([cos, cos], axis=-1)
    sin_h = jnp.concatenate([-sin, sin], axis=-1)
    return jnp.tile(cos_h, (1, reps)), jnp.tile(sin_h, (1, reps))


def _trunk(x, pos, mem_kt, mem_vt, shift0, wkv0, swa_cache, t_real, W):
    bsz, tp, _ = x.shape
    n = bsz * tp
    prompt = swa_cache is None
    mem_bb = bsz if prompt else STEP_MEM_SEQS
    act_dtype = bf16 if prompt else f32
    tm = ROW_TILE
    flat = lambda a: a.reshape(n, a.shape[-1])
    unflat = lambda a: a.reshape(bsz, tp, a.shape[-1])
    cos2, sin2 = _rope_tables(pos, HEADS_PER_SLAB)

    prep_w = (W["shift_mu"], W["w_w2"], W["w0"], W["w_a2"], W["a0"], W["w_g2"], W["k_k"], W["k_a"], W["r_k"])
    if prompt:
        seqs, q_mem, new_shift = _proj_prep(x, W["norm_mix"][0], W["w_in_a"], shift0, prep_w, PREP_TILE, bf16)
        q_col = 0
        main, new_wkv = _wkv_chunked(*seqs, W["lnx_w"], W["lnx_b"])
    else:
        q_mem = unflat(_norm_matmul(flat(x), W["norm_mix"][0], W["w_in_a"], tm))
        q_col = RWKV_COLS // MEM_WIDTH
        new_shift = q_mem[:, t_real - 1, :RWKV_COLS]
        steps = jnp.transpose(q_mem[:, :t_real, :RWKV_COLS], (1, 0, 2)).reshape(1, t_real * bsz, RWKV_COLS)
        seqs = [s.reshape(t_real, bsz, MAIN_WIDTH) for s in _rwkv_prep(steps, shift0, prep_w)]
        main, new_wkv = _wkv_steps(wkv0, seqs, W["lnx_w"], W["lnx_b"], tp)
    mem_o = _mem_attn(q_mem, q_col, W["mem_qnorm"][0], mem_kt, mem_vt, 0, MEM_Q_TILE, mem_bb, act_dtype)
    tables = (cos2, sin2) if prompt else (jnp.tile(cos2, (tm // tp, 1)), jnp.tile(sin2, (tm // tp, 1)))
    h, k_sh, v_sh, proj = _out_mlp(flat(x), flat(main), flat(mem_o), 0, W["w_out"], W["norm_mlp"][0], W["w_up"],
                                   W["w_down"], tm,
                                   nxt=(W["kv_norm"], W["w_kv"], W["swa_knorm"], *tables, W["norm_mix"][1], W["w_in_b"]))
    k_sh, v_sh, proj = unflat(k_sh), unflat(v_sh), unflat(proj)

    if prompt:
        main = _swa(proj, W["swa_qnorm"], W["sinks"], cos2, sin2, k_sh, v_sh)
        win = min(WINDOW, tp)
        heads = lambda a: a.reshape(bsz, a.shape[1], SWA_KV_HEADS, HEAD_DIM)
        k_state, v_state = heads(k_sh[:, tp - win:]), heads(v_sh[:, tp - win:])
    else:
        ckt, cvt = swa_cache
        main, kt_new, vt_new = _swa_step(proj, W["swa_qnorm"], W["sinks"], cos2, sin2, ckt, cvt, k_sh, v_sh,
                                         t_real, bb=STEP_SWA_SEQS)
        untransposed = lambda a: jnp.transpose(a.reshape(bsz, SWA_KV_HEADS, HEAD_DIM, -1), (0, 3, 1, 2))
        k_state, v_state = untransposed(kt_new), untransposed(vt_new)
    mem_o = _mem_attn(proj, MAIN_WIDTH // MEM_WIDTH, W["mem_qnorm"][1], mem_kt, mem_vt, 1, MEM_Q_TILE, mem_bb, act_dtype)
    y = _out_mlp(h, flat(main), flat(mem_o), 1, W["w_out"], W["norm_mlp"][1], W["w_up"], W["w_down"], tm)
    return unflat(y)[:, :t_real], new_shift[None], new_wkv[None], k_state, v_state


def kernel(x_prompt, x_sample, state_rwkv_shift, state_rwkv_wkv, cache_swa_k, cache_swa_v, cache_mem_k,
           cache_mem_v, mem_prompt, norm_mix, norm_mlp, w_out, w_up, w_down, mem_norm, w_mem_kv, mem_qnorm,
           mem_knorm, w_in_a, shift_mu, w_w2, w0, w_a2, a0, w_g2, k_k, k_a, r_k, lnx_w, lnx_b, w_in_b,
           swa_qnorm, sinks, kv_norm, w_kv, swa_knorm):
    W = dict(norm_mix=norm_mix, norm_mlp=norm_mlp, w_out=w_out.astype(bf16), w_up=w_up.astype(bf16),
             w_down=w_down.astype(bf16), mem_qnorm=mem_qnorm, w_in_a=w_in_a[0].astype(bf16),
             shift_mu=shift_mu[0], w_w2=w_w2[0], w0=w0[0], w_a2=w_a2[0], a0=a0[0], w_g2=w_g2[0],
             k_k=k_k[0], k_a=k_a[0], r_k=r_k[0].reshape(-1), lnx_w=lnx_w[0], lnx_b=lnx_b[0],
             w_in_b=w_in_b[0].astype(bf16), swa_qnorm=swa_qnorm[0], sinks=sinks[0], kv_norm=kv_norm,
             w_kv=w_kv.astype(bf16), swa_knorm=swa_knorm)
    bp, tp, _ = x_prompt.shape
    bs, ts, _ = x_sample.shape
    depth = norm_mix.shape[0]

    mem_flat = mem_prompt.reshape(bp * N_MEM, D_MODEL)
    p_mem_k, p_mem_v = [], []
    for l in range(depth):
        kv = _norm_matmul(mem_flat, mem_norm[l], w_mem_kv[l].astype(bf16), ROW_TILE)
        mk, mv = _kv_post(kv, mem_knorm[l], N_MEM)
        p_mem_k.append(mk.reshape(bp, N_MEM, MEM_WIDTH))
        p_mem_v.append(mv.reshape(bp, N_MEM, MEM_WIDTH))
    transposed = lambda a: jnp.swapaxes(jnp.stack(a), 2, 3)
    y_p, p_shift, p_wkv, p_k, p_v = _trunk(
        x_prompt, jnp.arange(tp), transposed(p_mem_k), transposed(p_mem_v), jnp.zeros((bp, RWKV_COLS), f32),
        None, None, tp, W)
    mem_heads = lambda a: jnp.stack(a).reshape(depth, bp, N_MEM, MEM_HEADS, HEAD_DIM)
    cache_t = lambda a: jnp.transpose(a, (0, 1, 3, 4, 2)).reshape(depth, bs, MEM_WIDTH, N_MEM)

    tpad = -(-ts // SUBLANES) * SUBLANES
    x_s = jnp.pad(x_sample, ((0, 0), (0, tpad - ts), (0, 0)))
    win = cache_swa_k.shape[1]
    assert win == WINDOW
    swa_t = lambda a: jnp.transpose(a, (0, 2, 3, 1)).reshape(bs, KV_WIDTH, win)
    y_s, s_shift, s_wkv, s_k, s_v = _trunk(
        x_s, PAST_LEN + jnp.arange(tpad), cache_t(cache_mem_k), cache_t(cache_mem_v),
        state_rwkv_shift[0], state_rwkv_wkv[0],
        (swa_t(cache_swa_k), swa_t(cache_swa_v)), ts, W)

    return (y_p, y_s, p_shift, p_wkv, p_k, p_v, mem_heads(p_mem_k), mem_heads(p_mem_v),
            s_shift, s_wkv, s_k, s_v)
```

```python
import functools
import math

import jax
import jax.numpy as jnp
from jax import lax
from jax.experimental import pallas as pl
from jax.experimental.pallas import tpu as pltpu

f32 = jnp.float32
bf16 = jnp.bfloat16

D_MODEL = 1024
HEAD_DIM = 64
MEM_HEADS = 4
MEM_WIDTH = MEM_HEADS * HEAD_DIM
MAIN_WIDTH = D_MODEL - MEM_WIDTH
RWKV_HEADS = MAIN_WIDTH // HEAD_DIM
DECAY_LORA = 64
AAA_LORA = 64
GATE_LORA = 128
RWKV_COLS = 3 * MAIN_WIDTH + DECAY_LORA + AAA_LORA + GATE_LORA
A_IN_COLS = RWKV_COLS + MEM_WIDTH
SWA_Q_HEADS = MAIN_WIDTH // HEAD_DIM
SWA_KV_HEADS = 4
SWA_GROUP = SWA_Q_HEADS // SWA_KV_HEADS
KV_WIDTH = SWA_KV_HEADS * HEAD_DIM
WINDOW = 128
BLOCK = 128
N_MEM = 256
D_FF = 4 * D_MODEL
PAST_LEN = 16384
ROPE_THETA = 10000.0
NORM_EPS = 1e-6
LNX_EPS = 6.4e-4
L2_EPS = 1e-12
ATTN_SCALE = HEAD_DIM ** -0.5

LANES = 128
SUBLANES = 8
HEADS_PER_SLAB = LANES // HEAD_DIM
N_SLABS = MAIN_WIDTH // LANES
CHUNK = 64
WKV_CHUNKS_PER_STEP = 16
VMEM_LIMIT = 56 * 1024 * 1024
NEG_BIG = -1e30
ROW_TILE = 512
PREP_TILE = 512
MEM_Q_TILE = 512
MEM_Q_SUB = 256
STEP_MEM_SEQS = 16
STEP_SWA_SEQS = 8


def _cp(sem, vmem=VMEM_LIMIT):
    return pltpu.CompilerParams(dimension_semantics=sem, vmem_limit_bytes=vmem)


def _dot(a, b):
    return jnp.dot(a, b, preferred_element_type=f32)


def _dot_nt(a, b):
    return lax.dot_general(a, b, (((1,), (1,)), ((), ())), preferred_element_type=f32)


def _dot_tn(a, b):
    return lax.dot_general(a, b, (((0,), (0,)), ((), ())), preferred_element_type=f32)


def _iota(shape, dim):
    return lax.broadcasted_iota(jnp.int32, shape, dim)


def _group_ones():
    r = _iota((LANES, LANES), 0)
    c = _iota((LANES, LANES), 1)
    return jnp.where((r < HEAD_DIM) == (c < HEAD_DIM), 1.0, 0.0).astype(f32)


def _head_sum(x, gmat, passes=2):
    w = x.shape[-1]
    gb = gmat.astype(bf16)
    hi = x.astype(bf16)
    slabs = lambda y: [y[:, j * LANES:(j + 1) * LANES] for j in range(w // LANES)]
    parts = [_dot(h, gb) for h in slabs(hi)]
    if passes == 2:
        lo = (x - hi.astype(f32)).astype(bf16)
        parts = [p + _dot(l, gb) for p, l in zip(parts, slabs(lo))]
    return parts[0] if len(parts) == 1 else jnp.concatenate(parts, axis=-1)


def _cumsum_rows(x):
    rows = x.shape[0]
    row = _iota(x.shape, 0)
    step = 1
    while step < rows:
        x = x + jnp.where(row >= step, pltpu.roll(x, step, 0), 0.0)
        step *= 2
    return x


def _sigmoid(x):
    return 1.0 / (1.0 + jnp.exp(-x))


def _rot_half(x):
    lane = _iota(x.shape, 1)
    first = (lane & (HEAD_DIM - 1)) < (HEAD_DIM // 2)
    return jnp.where(first, pltpu.roll(x, LANES - HEAD_DIM // 2, 1), pltpu.roll(x, HEAD_DIM // 2, 1))


def _norm_matmul_kernel(x_ref, g_ref, w_ref, o_ref):
    x = x_ref[...]
    xn = x * lax.rsqrt(jnp.mean(x * x, axis=-1, keepdims=True) + NORM_EPS) * g_ref[...]
    o_ref[...] = _dot(xn.astype(bf16), w_ref[...])


def _norm_matmul(x, g, w_bf16, tm):
    n, d = x.shape
    m = w_bf16.shape[1]
    tm = min(tm, n)
    return pl.pallas_call(
        _norm_matmul_kernel,
        grid=(n // tm,),
        in_specs=[pl.BlockSpec((tm, d), lambda i: (i, 0)),
                  pl.BlockSpec((1, d), lambda i: (0, 0)),
                  pl.BlockSpec((d, m), lambda i: (0, 0))],
        out_specs=pl.BlockSpec((tm, m), lambda i: (i, 0)),
        out_shape=jax.ShapeDtypeStruct((n, m), f32),
        compiler_params=_cp(("parallel",)),
        name="norm_matmul",
    )(x, g.reshape(1, d), w_bf16)


def _head_norm_rope(x, g, cos, sin, gmat):
    xn = x * lax.rsqrt(_head_sum(x * x, gmat) * (1.0 / HEAD_DIM) + NORM_EPS) * g
    slabs = [xn[:, j * LANES:(j + 1) * LANES] for j in range(x.shape[-1] // LANES)]
    return [xs * cos + _rot_half(xs) * sin for xs in slabs]


def _kv_post_kernel(kv_ref, g_ref, k_ref, v_ref):
    gmat = _group_ones()
    x = kv_ref[:, :KV_WIDTH]
    k_ref[...] = x * lax.rsqrt(_head_sum(x * x, gmat) * (1.0 / HEAD_DIM) + NORM_EPS) * g_ref[...]
    v_ref[...] = kv_ref[:, KV_WIDTH:]


def _kv_post(kv, g_head, tm):
    n = kv.shape[0]
    tm = min(tm, n)
    g = jnp.tile(g_head.reshape(1, HEAD_DIM), (1, KV_WIDTH // HEAD_DIM))
    return pl.pallas_call(
        _kv_post_kernel,
        grid=(n // tm,),
        in_specs=[pl.BlockSpec((tm, 2 * KV_WIDTH), lambda i: (i, 0)),
                  pl.BlockSpec((1, KV_WIDTH), lambda i: (0, 0))],
        out_specs=[pl.BlockSpec((tm, KV_WIDTH), lambda i: (i, 0)),
                   pl.BlockSpec((tm, KV_WIDTH), lambda i: (i, 0))],
        out_shape=[jax.ShapeDtypeStruct((n, KV_WIDTH), f32)] * 2,
        compiler_params=_cp(("parallel",)),
        name="kv_post",
    )(kv, g)


def _time_mix_inputs(pf, prev, lora_pf, lora_prev, consts, outs):
    mu_ref, ww2_ref, w0_ref, wa2_ref, a0_ref, wg2_ref, kk_ref, ka_ref, rk_ref = consts
    r_o, lw_o, k_o, v_o, a_o, b_o, bonus_o, g_o = outs
    gmat = _group_ones()
    i1, i2, i3 = MAIN_WIDTH, 2 * MAIN_WIDTH, 3 * MAIN_WIDTH
    ls = lora_pf + (lora_prev - lora_pf) * mu_ref[:, i3:]
    wd, ad, gd = ls[:, :DECAY_LORA], ls[:, DECAY_LORA:DECAY_LORA + AAA_LORA], ls[:, DECAY_LORA + AAA_LORA:]
    lora = lambda x, w_ref: _dot(x.astype(bf16), w_ref[...].astype(bf16))
    lw_o[0] = -math.exp(-0.5) * _sigmoid(w0_ref[...] + lora(jnp.tanh(wd), ww2_ref))
    a_sig = _sigmoid(a0_ref[...] + lora(ad, wa2_ref))
    g_o[0] = lora(_sigmoid(gd), wg2_ref).astype(g_o.dtype)
    ps = pf + (prev - pf) * mu_ref[:, :i3]
    r, k, v = ps[:, :i1], ps[:, i1:i2], ps[:, i2:i3]
    kk = k * kk_ref[...]
    kk = kk * jnp.minimum(lax.rsqrt(_head_sum(kk * kk, gmat, passes=1)), 1.0 / L2_EPS)
    k2 = k * (1.0 + (a_sig - 1.0) * ka_ref[...])
    r_o[0] = r.astype(r_o.dtype)
    k_o[0] = k2.astype(k_o.dtype)
    v_o[0] = v.astype(v_o.dtype)
    a_o[0] = (-kk).astype(a_o.dtype)
    b_o[0] = (kk * a_sig).astype(b_o.dtype)
    bonus_o[0] = (_head_sum(r * k2 * rk_ref[...], gmat, passes=1) * v).astype(bonus_o.dtype)


def _rwkv_prep_kernel(p_ref, shift_ref, *refs):
    pf = p_ref[0]
    prev = jnp.concatenate([shift_ref[0], pf[:pf.shape[0] - shift_ref.shape[1]]], axis=0)
    i3 = 3 * MAIN_WIDTH
    _time_mix_inputs(pf[:, :i3], prev[:, :i3], pf[:, i3:], prev[:, i3:], refs[:9], refs[9:])


def _proj_prep_kernel(x_ref, gx_ref, w_ref, shift_ref, *refs):
    consts, outs, (q_o, last_o, carry_ref) = refs[:9], refs[9:17], refs[17:]
    i = pl.program_id(1)
    i3 = 3 * MAIN_WIDTH
    x = x_ref[0]
    xn = (x * lax.rsqrt(jnp.mean(x * x, axis=-1, keepdims=True) + NORM_EPS) * gx_ref[...]).astype(bf16)
    lora_pf = _dot(xn, w_ref[:, i3:RWKV_COLS])
    pf = _dot(xn, w_ref[:, :i3])
    q_o[0] = _dot(xn, w_ref[:, RWKV_COLS:])
    first_row = _iota((pf.shape[0], 1), 0) == 0
    first_prev = jnp.where(i == 0, shift_ref[0], carry_ref[...])

    def shifted(p, cols):
        return jnp.where(first_row, first_prev[:, cols], pltpu.roll(p, 1, 0))

    lora_prev, prev = shifted(lora_pf, slice(i3, RWKV_COLS)), shifted(pf, slice(0, i3))
    last = jnp.concatenate([pf[pf.shape[0] - 1:, :], lora_pf[pf.shape[0] - 1:, :]], axis=1)
    carry_ref[...] = last
    last_o[0] = last
    _time_mix_inputs(pf, prev, lora_pf, lora_prev, consts, outs)


def _proj_prep(x, g_norm, w_in_bf16, shift_prev, prep_w, tt, seq_dtype):
    bsz, t, _ = x.shape
    row = lambda a: a.reshape(1, -1)
    full = lambda a: pl.BlockSpec(a.shape, lambda b, i: (0,) * a.ndim)
    mu, w_w2, w0, w_a2, a0, w_g2, k_k, k_a, r_k = prep_w
    consts = [row(mu), w_w2, row(w0), w_a2, row(a0), w_g2, row(k_k), row(k_a), row(r_k)]
    tile = lambda w: pl.BlockSpec((1, tt, w), lambda b, i: (b, i, 0))
    per_seq = pl.BlockSpec((1, 1, RWKV_COLS), lambda b, i: (b, 0, 0))
    res = pl.pallas_call(
        _proj_prep_kernel,
        grid=(bsz, t // tt),
        in_specs=[tile(D_MODEL), full(row(g_norm)), full(w_in_bf16), per_seq] + [full(c) for c in consts],
        out_specs=[tile(MAIN_WIDTH)] * 8 + [tile(MEM_WIDTH), per_seq],
        out_shape=[jax.ShapeDtypeStruct((bsz, t, MAIN_WIDTH), f32 if j == 1 else seq_dtype) for j in range(8)]
                  + [jax.ShapeDtypeStruct((bsz, t, MEM_WIDTH), f32), jax.ShapeDtypeStruct((bsz, 1, RWKV_COLS), f32)],
        scratch_shapes=[pltpu.VMEM((1, RWKV_COLS), f32)],
        compiler_params=_cp(("parallel", "arbitrary")),
        name="proj_prep",
    )(x, row(g_norm), w_in_bf16, shift_prev.reshape(bsz, 1, RWKV_COLS), *consts)
    return res[:8], res[8], res[9].reshape(bsz, RWKV_COLS)


def _rwkv_prep(steps, shift_prev, prep_w):
    _, t, _ = steps.shape
    row = lambda a: a.reshape(1, -1)
    full = lambda a: pl.BlockSpec(a.shape, lambda i: (0,) * a.ndim)
    mu, w_w2, w0, w_a2, a0, w_g2, k_k, k_a, r_k = prep_w
    consts = [row(mu), w_w2, row(w0), w_a2, row(a0), w_g2, row(k_k), row(k_a), row(r_k)]
    shift_rows = shift_prev[None]
    return pl.pallas_call(
        _rwkv_prep_kernel,
        grid=(1,),
        in_specs=[full(steps), full(shift_rows)] + [full(c) for c in consts],
        out_specs=[pl.BlockSpec((1, t, MAIN_WIDTH), lambda i: (0, 0, 0))] * 8,
        out_shape=[jax.ShapeDtypeStruct((1, t, MAIN_WIDTH), f32)] * 8,
        compiler_params=_cp(("arbitrary",)),
        name="rwkv_prep",
    )(steps, shift_rows, *consts)


def _wkv_chunk_kernel(r_ref, lw_ref, k_ref, v_ref, a_ref, b_ref, bonus_ref, g_ref, lnw_ref, lnb_ref,
                      o_ref, s_ref, h_ref, qt_ref, yv_ref, pp_ref, hv_ref, tc_ref):
    i = pl.program_id(2)
    c = CHUNK
    nc = r_ref.shape[1] // c
    n = range(nc)

    @pl.when(i == 0)
    def _():
        h_ref[...] = jnp.zeros_like(h_ref)
        qt_ref[...] = jnp.zeros_like(qt_ref)
        yv_ref[...] = jnp.zeros_like(yv_ref)
        pp_ref[...] = jnp.zeros_like(pp_ref)
        hv_ref[...] = jnp.zeros_like(hv_ref)
        tc_ref[...] = jnp.zeros_like(tc_ref)

    gmat = _group_ones()
    bd = gmat > 0.5
    row = _iota((LANES, LANES), 0)
    col = _iota((LANES, LANES), 1)
    t_row = row & (c - 1)
    s_col = col & (c - 1)
    m_strict = s_col < t_row
    m_incl = s_col <= t_row
    top = row < c
    lane_lo = _iota((1, LANES), 1) < HEAD_DIM
    eye = row == col
    zeros = jnp.zeros((c, LANES), f32)
    mm = lambda x: x.astype(bf16)
    gb = gmat.astype(bf16)
    sls = [slice(j * c, (j + 1) * c) for j in n]

    def pending_state():
        state = {"h": h_ref[...], "ys": []}

        def state_step(j):
            hb = mm(state["h"])
            state["ys"].append(_dot(qt_ref[j], hb) + yv_ref[j])
            state["h"] = state["h"] * tc_ref[j] + _dot(pp_ref[j], hb) + hv_ref[j]

        def finish():
            h_ref[...] = state["h"]
            means = [_dot(mm(y), gb) * (1.0 / HEAD_DIM) for y in state["ys"]]
            ycs = [state["ys"][j] - means[j] for j in n]
            vrs = [_dot(mm(yc * yc), gb) * (1.0 / HEAD_DIM) for yc in ycs]
            outs = []
            for j in n:
                yn = ycs[j] * lax.rsqrt(vrs[j] + LNX_EPS) * lnw_ref[...] + lnb_ref[...]
                outs.append(((yn + bonus_ref[0, sls[j], :].astype(f32))
                             * g_ref[0, sls[j], :].astype(f32)).astype(o_ref.dtype))
            return outs

        return state_step, finish

    def chunk_group(js):
        st = {}

        def decay_and_blocks():
            st["vs"], st["tots"], st["ats"], st["rts"], st["bkts"], gms = [], [], [], [], [], []
            for j in js:
                sl = sls[j]
                lw = lw_ref[0, sl, :]
                cum = _cumsum_rows(lw)
                tot = cum[c - 1:c, :]
                b, k = b_ref[0, sl, :].astype(f32), k_ref[0, sl, :].astype(f32)
                e_neg, e_rem = jnp.exp(-cum), jnp.exp(tot - cum)
                at = a_ref[0, sl, :].astype(f32) * jnp.exp(cum - lw)
                rt = r_ref[0, sl, :].astype(f32) * jnp.exp(cum)
                bk = mm(jnp.concatenate([b * e_neg, k * e_neg], axis=0))
                lhs = mm(jnp.concatenate([jnp.where(lane_lo, at, 0.0), jnp.where(lane_lo, 0.0, at),
                                          jnp.where(lane_lo, rt, 0.0), jnp.where(lane_lo, 0.0, rt)], axis=0))
                gms.append(_dot_nt(lhs, bk))
                st["vs"].append(v_ref[0, sl, :].astype(f32))
                st["tots"].append(tot)
                st["ats"].append(at)
                st["rts"].append(rt)
                st["bkts"].append(mm(jnp.concatenate([b * e_rem, k * e_rem], axis=0)))
            st["gms"] = gms

        def first_products():
            m = range(len(js))
            aaks = [jnp.where(m_strict, gm[:LANES], 0.0) for gm in st["gms"]]
            st["arks"] = [mm(jnp.where(m_incl, gm[LANES:], 0.0)) for gm in st["gms"]]
            aak_sws = [pltpu.roll(aak, HEAD_DIM, 1) for aak in aaks]
            st["ms"] = [jnp.where(bd, jnp.where(top, aaks[q], aak_sws[q]), 0.0) for q in m]
            aks = [mm(jnp.where(bd, jnp.where(top, aak_sws[q], aaks[q]), 0.0)) for q in m]
            vvs = [mm(jnp.concatenate([pltpu.roll(v, HEAD_DIM, 1)] * 2, axis=0)) for v in st["vs"]]
            akvs = [_dot(aks[q], vvs[q]) for q in m]
            st["xs"] = [jnp.where(bd, jnp.concatenate([st["ats"][q]] * 2, axis=0), akvs[q]) for q in m]

        def square(last):
            def stage():
                m = range(len(js))
                ms, xs = st["ms"], st["xs"]
                if not last:
                    zs = [_dot(mm(ms[q]), mm(jnp.concatenate([xs[q], ms[q]], axis=1))) for q in m]
                    st["xs"] = [xs[q] + zs[q][:, :LANES] for q in m]
                    st["ms"] = [zs[q][:, LANES:] for q in m]
                else:
                    zs = [_dot(mm(ms[q]), mm(xs[q])) for q in m]
                    st["xs"] = [xs[q] + zs[q] for q in m]
            return stage

        def last_products():
            m = range(len(js))
            rhss = []
            for q in m:
                x = st["xs"][q]
                ta = jnp.where(lane_lo, x[:c], x[c:])
                uv = pltpu.roll(jnp.where(lane_lo, x[c:], x[:c]), HEAD_DIM, 1)
                rhss.append(mm(jnp.concatenate([jnp.concatenate([ta, uv], axis=1),
                                                jnp.concatenate([zeros, st["vs"][q]], axis=1)], axis=0)))
            st["z2s"] = [_dot(st["arks"][q], rhss[q]) for q in m]
            st["pzs"] = [_dot_tn(st["bkts"][q], rhss[q]) for q in m]

        def make_pending():
            for q, j in enumerate(js):
                z2, pz = st["z2s"][q], st["pzs"][q]
                qt_ref[j] = mm(st["rts"][q] + jnp.where(lane_lo, z2[:c, :LANES], z2[c:, :LANES]))
                yv_ref[j] = jnp.where(lane_lo, z2[:c, LANES:], z2[c:, LANES:])
                pp_ref[j] = mm(jnp.where(bd, pz[:, :LANES], 0.0))
                hv_ref[j] = jnp.where(bd, pz[:, LANES:], 0.0)
                tc_ref[j] = jnp.sum(jnp.where(eye, jnp.exp(st["tots"][q]), 0.0), axis=1, keepdims=True)

        return ([decay_and_blocks, first_products] + [square(False)] * 5 + [square(True), last_products], make_pending)

    @pl.when(i < pl.num_programs(2) - 1)
    def _():
        half = nc // 2
        (stages_a, pend_a), (stages_b, pend_b) = chunk_group(list(range(half))), chunk_group(list(range(half, nc)))
        lag = 4
        state_step, finish = pending_state()
        outs = None
        for slot in range(max(len(stages_a) + lag, nc + 1)):
            if slot < nc:
                state_step(slot)
            if slot == nc:
                outs = finish()
            if slot < len(stages_a):
                stages_a[slot]()
            if 0 <= slot - lag < len(stages_b):
                stages_b[slot - lag]()
        for j in n:
            o_ref[0, sls[j], :] = outs[j]
        pend_a()
        pend_b()

    @pl.when(i == pl.num_programs(2) - 1)
    def _():
        state_step, finish = pending_state()
        for j in n:
            state_step(j)
        outs = finish()
        for j in n:
            o_ref[0, sls[j], :] = outs[j]

    @pl.when(i == pl.num_programs(2) - 1)
    def _():
        s_ref[0, 0] = h_ref[...].T


def _wkv_chunked(r, lw, k, v, a, b, bonus, g, lnx_w, lnx_b):
    bsz, t, _ = r.shape
    nc = WKV_CHUNKS_PER_STEP
    tt = nc * CHUNK
    nt = t // tt
    assert t % tt == 0
    cur = pl.BlockSpec((1, tt, LANES), lambda bb, hp, i: (bb, jnp.minimum(i, nt - 1), hp))
    pend = pl.BlockSpec((1, tt, LANES), lambda bb, hp, i: (bb, jnp.maximum(i - 1, 0), hp))
    vec = pl.BlockSpec((1, LANES), lambda bb, hp, i: (0, hp))
    out, st = pl.pallas_call(
        _wkv_chunk_kernel,
        grid=(bsz, N_SLABS, nt + 1),
        in_specs=[cur] * 6 + [pend, pend, vec, vec],
        out_specs=[pend, pl.BlockSpec((1, 1, LANES, LANES), lambda bb, hp, i: (bb, hp, 0, 0))],
        out_shape=[jax.ShapeDtypeStruct((bsz, t, MAIN_WIDTH), bf16),
                   jax.ShapeDtypeStruct((bsz, N_SLABS, LANES, LANES), f32)],
        scratch_shapes=[pltpu.VMEM((LANES, LANES), f32),
                        pltpu.VMEM((nc, CHUNK, LANES), bf16), pltpu.VMEM((nc, CHUNK, LANES), f32),
                        pltpu.VMEM((nc, LANES, LANES), bf16), pltpu.VMEM((nc, LANES, LANES), f32),
                        pltpu.VMEM((nc, LANES, 1), f32)],
        compiler_params=_cp(("parallel", "parallel", "arbitrary")),
        name="wkv_chunked",
    )(r, lw, k, v, a, b, bonus, g, lnx_w.reshape(1, -1), lnx_b.reshape(1, -1))
    st = jnp.stack([st[:, :, :HEAD_DIM, :HEAD_DIM], st[:, :, HEAD_DIM:, HEAD_DIM:]], axis=2)
    return out, st.reshape(bsz, RWKV_HEADS, HEAD_DIM, HEAD_DIM)


def _wkv_steps_kernel(s_ref, seq_ref, lnw_ref, lnb_ref, o_ref, so_ref, w_ref, y_ref, *, n_steps):
    i_r, i_lw, i_k, i_v, i_a, i_b, i_bonus, i_g = range(8)
    w_ref[...] = jnp.exp(seq_ref[i_lw])
    sub = _iota((SUBLANES, LANES), 0)

    def group(vg, carry):
        v0 = pl.multiple_of(vg * SUBLANES, SUBLANES)
        v_rows = [seq_ref[i_v, t, pl.ds(v0, SUBLANES), :] for t in range(n_steps)]
        ys = [jnp.zeros((SUBLANES, LANES), f32) for _ in range(n_steps)]
        for j in range(SUBLANES):
            sv = s_ref[0, v0 + j]
            for t in range(n_steps):
                sa = jnp.sum(sv * seq_ref[i_a, t], axis=0, keepdims=True)
                sv = sv * w_ref[t] + sa * seq_ref[i_b, t] + v_rows[t][j:j + 1, :] * seq_ref[i_k, t]
                y = jnp.sum(sv * seq_ref[i_r, t], axis=0, keepdims=True)
                ys[t] = jnp.where(sub == j, y, ys[t])
            so_ref[0, v0 + j] = sv
        for t in range(n_steps):
            y_ref[t, pl.ds(v0, SUBLANES), :] = ys[t]
        return carry

    lax.fori_loop(0, HEAD_DIM // SUBLANES, group, 0)
    y = y_ref[...]
    mean = jnp.mean(y, axis=1, keepdims=True)
    yc = y - mean
    var = jnp.mean(yc * yc, axis=1, keepdims=True)
    yn = yc * lax.rsqrt(var + LNX_EPS) * lnw_ref[...] + lnb_ref[...]
    o_ref[...] = (yn + seq_ref[i_bonus]) * seq_ref[i_g]


def _wkv_steps(state, seqs, lnx_w, lnx_b, tp):
    n_steps, bsz, _ = seqs[0].shape
    assert bsz % LANES == 0
    state_t = jnp.transpose(state, (1, 2, 3, 0))
    seq_t = jnp.transpose(jnp.stack(seqs), (0, 1, 3, 2))
    lanes = lambda x: jnp.broadcast_to(x.reshape(MAIN_WIDTH, 1), (MAIN_WIDTH, LANES))
    sspec = pl.BlockSpec((1, HEAD_DIM, HEAD_DIM, LANES), lambda h, bi: (h, 0, 0, bi))
    vec = pl.BlockSpec((HEAD_DIM, LANES), lambda h, bi: (h, 0))
    ospec = pl.BlockSpec((n_steps, HEAD_DIM, LANES), lambda h, bi: (0, h, bi))
    out, new_state = pl.pallas_call(
        functools.partial(_wkv_steps_kernel, n_steps=n_steps),
        grid=(RWKV_HEADS, bsz // LANES),
        in_specs=[sspec, pl.BlockSpec((8, n_steps, HEAD_DIM, LANES), lambda h, bi: (0, 0, h, bi)), vec, vec],
        out_specs=[ospec, sspec],
        out_shape=[jax.ShapeDtypeStruct((n_steps, MAIN_WIDTH, bsz), f32),
                   jax.ShapeDtypeStruct(state_t.shape, f32)],
        scratch_shapes=[pltpu.VMEM((n_steps, HEAD_DIM, LANES), f32), pltpu.VMEM((n_steps, HEAD_DIM, LANES), f32)],
        compiler_params=_cp(("parallel", "parallel")),
        name="wkv_steps",
    )(state_t, seq_t, lanes(lnx_w), lanes(lnx_b))
    out = jnp.pad(jnp.transpose(out, (2, 0, 1)), ((0, 0), (0, tp - n_steps), (0, 0)))
    return out, jnp.transpose(new_state, (3, 0, 1, 2))


def _mem_attn_kernel(q_ref, g_ref, kt_ref, vt_ref, o_ref, *, sub):
    gmat = _group_ones()
    bb, tq, _ = q_ref.shape
    head = _iota((1, MEM_WIDTH), 1) // HEAD_DIM
    tiles = [(b, pl.ds(r * sub, sub)) for b in range(bb) for r in range(tq // sub)]
    qs = []
    for b, rows in tiles:
        q = q_ref[b, rows, :]
        qn = q * lax.rsqrt(_head_sum(q * q, gmat, passes=1) * (1.0 / HEAD_DIM) + NORM_EPS) * (g_ref[...] * ATTN_SCALE)
        qs.append(jnp.concatenate([jnp.where(head == h, qn, 0.0) for h in range(MEM_HEADS)], axis=0).astype(bf16))
    ss = [_dot(q4, kt_ref[0, b].astype(bf16)) for q4, (b, _) in zip(qs, tiles)]
    es = [jnp.exp(s - jnp.max(s, axis=-1, keepdims=True)) for s in ss]
    o4s = [_dot_nt(e.astype(bf16), vt_ref[0, b].astype(bf16)) / jnp.sum(e, axis=-1, keepdims=True)
           for e, (b, _) in zip(es, tiles)]
    for o4, (b, rows) in zip(o4s, tiles):
        o = jnp.zeros((sub, MEM_WIDTH), f32)
        for h in range(MEM_HEADS):
            o = o + jnp.where(head == h, o4[h * sub:(h + 1) * sub], 0.0)
        o_ref[b, rows, :] = o.astype(o_ref.dtype)


def _mem_attn(proj, col_block, g_qnorm, mem_kt, mem_vt, layer, tq, bb, out_dtype):
    bsz, t, _ = proj.shape
    tq = min(tq, t)
    g = jnp.tile(g_qnorm.reshape(1, HEAD_DIM), (1, MEM_HEADS))
    kv_spec = pl.BlockSpec((1, bb, MEM_WIDTH, N_MEM), lambda b, i: (layer, b, 0, 0))
    return pl.pallas_call(
        functools.partial(_mem_attn_kernel, sub=min(tq, MEM_Q_SUB)),
        grid=(bsz // bb, t // tq),
        in_specs=[pl.BlockSpec((bb, tq, MEM_WIDTH), lambda b, i: (b, i, col_block)),
                  pl.BlockSpec((1, MEM_WIDTH), lambda b, i: (0, 0)),
                  kv_spec, kv_spec],
        out_specs=pl.BlockSpec((bb, tq, MEM_WIDTH), lambda b, i: (b, i, 0)),
        out_shape=jax.ShapeDtypeStruct((bsz, t, MEM_WIDTH), out_dtype),
        compiler_params=_cp(("parallel", "parallel")),
        name="mem_attn",
    )(proj, g, mem_kt, mem_vt)


def _swa_kernel(sink_ref, q_ref, g_ref, cos_ref, sin_ref, kp_ref, kc_ref, vp_ref, vc_ref, o_ref, qn_ref):
    i = pl.program_id(0)
    gmat = _group_ones()
    bb, tq, _ = q_ref.shape
    keys, vals = [], []
    for b in range(bb):
        for j in range(N_SLABS):
            sl = slice(j * LANES, (j + 1) * LANES)
            x = q_ref[b, :, sl]
            xn = x * lax.rsqrt(_head_sum(x * x, gmat, passes=1) * (1.0 / HEAD_DIM) + NORM_EPS) * g_ref[...]
            qn_ref[b, :, sl] = (xn * cos_ref[...] + _rot_half(xn) * sin_ref[...]) * ATTN_SCALE
        keys.append(jnp.concatenate([kp_ref[b], kc_ref[b]], axis=0).astype(bf16))
        vals.append(jnp.concatenate([vp_ref[b], vc_ref[b]], axis=0).astype(bf16))
    rows = SWA_GROUP * tq
    row = _iota((rows, 2 * BLOCK), 0) & (tq - 1)
    col = _iota((rows, 2 * BLOCK), 1)
    mask = (col > row) & (col <= row + WINDOW) & ((i > 0) | (col >= BLOCK))
    grp = _iota((rows, 1), 0) // tq
    pairs = [(b, hk) for b in range(bb) for hk in range(SWA_KV_HEADS)]
    hs = [slice(hk * HEAD_DIM, (hk + 1) * HEAD_DIM) for hk in range(SWA_KV_HEADS)]
    q3s = [jnp.concatenate([qn_ref[b, :, (hk * SWA_GROUP + gq) * HEAD_DIM:(hk * SWA_GROUP + gq + 1) * HEAD_DIM]
                            for gq in range(SWA_GROUP)], axis=0).astype(bf16) for b, hk in pairs]
    ss = [jnp.where(mask, _dot_nt(q3, keys[b][:, hs[hk]]), NEG_BIG) for q3, (b, hk) in zip(q3s, pairs)]
    sinks = [jnp.where(grp == 0, sink_ref[hk * SWA_GROUP],
                       jnp.where(grp == 1, sink_ref[hk * SWA_GROUP + 1], sink_ref[hk * SWA_GROUP + 2]))
             for hk in range(SWA_KV_HEADS)]
    ms = [jnp.maximum(jnp.max(s, axis=-1, keepdims=True), sinks[hk]) for s, (b, hk) in zip(ss, pairs)]
    es = [jnp.exp(s - m) for s, m in zip(ss, ms)]
    denoms = [jnp.sum(e, axis=-1, keepdims=True) + jnp.exp(sinks[hk] - m) for e, m, (b, hk) in zip(es, ms, pairs)]
    outs = [_dot(e.astype(bf16), vals[b][:, hs[hk]]) / d for e, d, (b, hk) in zip(es, denoms, pairs)]
    for o, (b, hk) in zip(outs, pairs):
        for gq in range(SWA_GROUP):
            hq = hk * SWA_GROUP + gq
            o_ref[b, :, hq * HEAD_DIM:(hq + 1) * HEAD_DIM] = o[gq * tq:(gq + 1) * tq].astype(o_ref.dtype)


def _swa(proj, g_qnorm, sinks, cos, sin, k, v):
    bsz, t, _ = proj.shape
    assert SWA_GROUP == 3
    tq = BLOCK
    g = jnp.tile(g_qnorm.reshape(1, HEAD_DIM), (1, HEADS_PER_SLAB))
    prev = pl.BlockSpec((bsz, BLOCK, KV_WIDTH), lambda i: (0, jnp.maximum(i - 1, 0), 0))
    cur = pl.BlockSpec((bsz, BLOCK, KV_WIDTH), lambda i: (0, i, 0))
    return pl.pallas_call(
        _swa_kernel,
        grid=(t // tq,),
        in_specs=[pl.BlockSpec(memory_space=pltpu.SMEM),
                  pl.BlockSpec((bsz, tq, MAIN_WIDTH), lambda i: (0, i, 0)),
                  pl.BlockSpec((1, LANES), lambda i: (0, 0)),
                  pl.BlockSpec((tq, LANES), lambda i: (i, 0)),
                  pl.BlockSpec((tq, LANES), lambda i: (i, 0)),
                  prev, cur, prev, cur],
        out_specs=pl.BlockSpec((bsz, tq, MAIN_WIDTH), lambda i: (0, i, 0)),
        out_shape=jax.ShapeDtypeStruct((bsz, t, MAIN_WIDTH), bf16),
        scratch_shapes=[pltpu.VMEM((bsz, tq, MAIN_WIDTH), f32)],
        compiler_params=_cp(("parallel",)),
        name="swa_attn",
    )(sinks, proj, g, cos, sin, k, k, v, v)


def _swa_step_kernel(sink_ref, q_ref, g_ref, cos_ref, sin_ref, kt_ref, vt_ref, kn_ref, vn_ref,
                     o_ref, kto_ref, vto_ref, qn_ref, *, n_new):
    gmat = _group_ones()
    bb, tq, _ = q_ref.shape
    row = _iota((SWA_GROUP * tq, 2 * BLOCK), 0) & (tq - 1)
    col = _iota((SWA_GROUP * tq, 2 * BLOCK), 1)
    mask = (col > row) & (col <= row + WINDOW)
    grp = _iota((SWA_GROUP * tq, 1), 0) // tq
    keep = _iota((KV_WIDTH, BLOCK), 1) < BLOCK - n_new
    pad = jnp.zeros((BLOCK - tq, KV_WIDTH), f32)
    ktb, vtb, knb, vnb = [], [], [], []
    for b in range(bb):
        for j in range(N_SLABS):
            sl = slice(j * LANES, (j + 1) * LANES)
            x = q_ref[b, :, sl]
            xn = x * lax.rsqrt(_head_sum(x * x, gmat, passes=1) * (1.0 / HEAD_DIM) + NORM_EPS) * g_ref[...]
            qn_ref[b, :, sl] = (xn * cos_ref[...] + _rot_half(xn) * sin_ref[...]) * ATTN_SCALE
        kt, vt = kt_ref[b], vt_ref[b]
        kn = jnp.concatenate([kn_ref[b], pad], axis=0)
        vn = jnp.concatenate([vn_ref[b], pad], axis=0)
        kto_ref[b] = jnp.where(keep, pltpu.roll(kt, BLOCK - n_new, 1), pltpu.roll(kn.T, BLOCK - n_new, 1))
        vto_ref[b] = jnp.where(keep, pltpu.roll(vt, BLOCK - n_new, 1), pltpu.roll(vn.T, BLOCK - n_new, 1))
        ktb.append(kt.astype(bf16))
        vtb.append(vt.astype(bf16))
        knb.append(kn.astype(bf16))
        vnb.append(vn.astype(bf16))
    pairs = [(b, hk) for b in range(bb) for hk in range(SWA_KV_HEADS)]
    hs = [slice(hk * HEAD_DIM, (hk + 1) * HEAD_DIM) for hk in range(SWA_KV_HEADS)]
    q3s = [jnp.concatenate([qn_ref[b, :, (hk * SWA_GROUP + gq) * HEAD_DIM:(hk * SWA_GROUP + gq + 1) * HEAD_DIM]
                            for gq in range(SWA_GROUP)], axis=0).astype(bf16) for b, hk in pairs]
    ss = [jnp.where(mask, jnp.concatenate([_dot(q3, ktb[b][hs[hk], :]), _dot_nt(q3, knb[b][:, hs[hk]])], axis=1),
                    NEG_BIG) for q3, (b, hk) in zip(q3s, pairs)]
    sinks = [jnp.where(grp == 0, sink_ref[hk * SWA_GROUP],
                       jnp.where(grp == 1, sink_ref[hk * SWA_GROUP + 1], sink_ref[hk * SWA_GROUP + 2]))
             for hk in range(SWA_KV_HEADS)]
    ms = [jnp.maximum(jnp.max(s, axis=-1, keepdims=True), sinks[hk]) for s, (b, hk) in zip(ss, pairs)]
    es = [jnp.exp(s - m) for s, m in zip(ss, ms)]
    denoms = [jnp.sum(e, axis=-1, keepdims=True) + jnp.exp(sinks[hk] - m) for e, m, (b, hk) in zip(es, ms, pairs)]
    ebs = [e.astype(bf16) for e in es]
    outs = [(_dot_nt(eb[:, :BLOCK], vtb[b][hs[hk], :]) + _dot(eb[:, BLOCK:], vnb[b][:, hs[hk]])) / d
            for eb, d, (b, hk) in zip(ebs, denoms, pairs)]
    for o, (b, hk) in zip(outs, pairs):
        for gq in range(SWA_GROUP):
            hq = hk * SWA_GROUP + gq
            o_ref[b, :, hq * HEAD_DIM:(hq + 1) * HEAD_DIM] = o[gq * tq:(gq + 1) * tq]


def _swa_step(proj, g_qnorm, sinks, cos, sin, cache_kt, cache_vt, k_new, v_new, n_new, bb):
    bsz, tq, _ = proj.shape
    assert SWA_GROUP == 3 and tq & (tq - 1) == 0
    g = jnp.tile(g_qnorm.reshape(1, HEAD_DIM), (1, HEADS_PER_SLAB))
    cache_spec = pl.BlockSpec((bb, KV_WIDTH, BLOCK), lambda b: (b, 0, 0))
    new_spec = pl.BlockSpec((bb, tq, KV_WIDTH), lambda b: (b, 0, 0))
    tab_spec = pl.BlockSpec((tq, LANES), lambda b: (0, 0))
    return pl.pallas_call(
        functools.partial(_swa_step_kernel, n_new=n_new),
        grid=(bsz // bb,),
        in_specs=[pl.BlockSpec(memory_space=pltpu.SMEM),
                  pl.BlockSpec((bb, tq, MAIN_WIDTH), lambda b: (b, 0, 0)),
                  pl.BlockSpec((1, LANES), lambda b: (0, 0)), tab_spec, tab_spec,
                  cache_spec, cache_spec, new_spec, new_spec],
        out_specs=[pl.BlockSpec((bb, tq, MAIN_WIDTH), lambda b: (b, 0, 0)), cache_spec, cache_spec],
        out_shape=[jax.ShapeDtypeStruct((bsz, tq, MAIN_WIDTH), f32),
                   jax.ShapeDtypeStruct(cache_kt.shape, f32), jax.ShapeDtypeStruct(cache_vt.shape, f32)],
        scratch_shapes=[pltpu.VMEM((bb, tq, MAIN_WIDTH), f32)],
        compiler_params=_cp(("parallel",)),
        name="swa_step",
    )(sinks, proj, g, cos, sin, cache_kt, cache_vt, k_new, v_new)


def _out_mlp_kernel(*refs, tf, with_next):
    if with_next:
        (h_ref, main_ref, mem_ref, wo_ref, g_ref, wup_ref, wdn_ref,
         gkv_ref, wkv_ref, gk_ref, cos_ref, sin_ref, gin_ref, win_ref, o_ref, k_ref, v_ref, p_ref) = refs
    else:
        h_ref, main_ref, mem_ref, wo_ref, g_ref, wup_ref, wdn_ref, o_ref = refs
    h1 = (h_ref[...] + _dot(main_ref[...].astype(bf16), wo_ref[0, :MAIN_WIDTH, :])
          + _dot(mem_ref[...].astype(bf16), wo_ref[0, MAIN_WIDTH:, :]))
    hn = (h1 * lax.rsqrt(jnp.mean(h1 * h1, axis=-1, keepdims=True) + NORM_EPS) * g_ref[...]).astype(bf16)
    acc = None
    for j in range(D_FF // tf):
        u = _dot(hn, wup_ref[0, :, j * tf:(j + 1) * tf])
        u = jnp.square(jnp.maximum(u, 0.0)).astype(bf16)
        d = _dot(u, wdn_ref[0, j * tf:(j + 1) * tf, :])
        acc = d if acc is None else acc + d
    h2 = h1 + acc
    o_ref[...] = h2
    if with_next:
        xhat = h2 * lax.rsqrt(jnp.mean(h2 * h2, axis=-1, keepdims=True) + NORM_EPS)
        kv = _dot((xhat * gkv_ref[...]).astype(bf16), wkv_ref[...])
        slabs = _head_norm_rope(kv[:, :KV_WIDTH], gk_ref[...], cos_ref[...], sin_ref[...], _group_ones())
        for j, slab in enumerate(slabs):
            k_ref[:, j * LANES:(j + 1) * LANES] = slab
        v_ref[...] = kv[:, KV_WIDTH:]
        p_ref[...] = _dot((xhat * gin_ref[...]).astype(bf16), win_ref[...])


def _out_mlp(h, main, mem_o, layer, w_out_bf16, g_mlp, w_up_bf16, w_down_bf16, tm, tf=1024, nxt=None):
    n = h.shape[0]
    tm = min(tm, n)
    rows = lambda w: pl.BlockSpec((tm, w), lambda i: (i, 0))
    resident = lambda a: pl.BlockSpec(a.shape, lambda i: (0, 0))
    of_layer = lambda a: pl.BlockSpec((1,) + a.shape[1:], lambda i: (layer, 0, 0))
    row = lambda x: x.reshape(1, -1)
    args = [h, main, mem_o, w_out_bf16, row(g_mlp), w_up_bf16, w_down_bf16]
    in_specs = [rows(D_MODEL), rows(MAIN_WIDTH), rows(MEM_WIDTH), of_layer(w_out_bf16), resident(args[4]),
                of_layer(w_up_bf16), of_layer(w_down_bf16)]
    out_specs, out_shape = [rows(D_MODEL)], [jax.ShapeDtypeStruct((n, D_MODEL), f32)]
    if nxt is not None:
        kv_norm, w_kv, k_norm, cos, sin, in_norm, w_in = nxt
        nt = cos.shape[0] // tm
        table = pl.BlockSpec((tm, LANES), lambda i: (i % nt, 0))
        extra = [row(kv_norm), w_kv, jnp.tile(row(k_norm), (1, KV_WIDTH // HEAD_DIM)), cos, sin, row(in_norm), w_in]
        args += extra
        in_specs += [resident(extra[0]), resident(w_kv), resident(extra[2]), table, table,
                     resident(extra[5]), resident(w_in)]
        out_specs += [rows(KV_WIDTH), rows(KV_WIDTH), rows(D_MODEL)]
        out_shape += [jax.ShapeDtypeStruct((n, KV_WIDTH), f32)] * 2 + [jax.ShapeDtypeStruct((n, D_MODEL), f32)]
    res = pl.pallas_call(
        functools.partial(_out_mlp_kernel, tf=tf, with_next=nxt is not None),
        grid=(n // tm,),
        in_specs=in_specs,
        out_specs=out_specs,
        out_shape=out_shape,
        compiler_params=_cp(("parallel",)),
        name="out_mlp",
    )(*args)
    return res if nxt is not None else res[0]


def _rope_tables(pos, reps):
    half = HEAD_DIM // 2
    freqs = jnp.power(ROPE_THETA, -jnp.arange(half, dtype=f32) / half)
    ang = pos.astype(f32)[:, None] * freqs[None, :]
    cos, sin = jnp.cos(ang), jnp.sin(ang)
    cos_h = jnp.concatenate([cos, cos], axis=-1)
    sin_h = jnp.concatenate([-sin, sin], axis=-1)
    return jnp.tile(cos_h, (1, reps)), jnp.tile(sin_h, (1, reps))


def _trunk(x, pos, mem_kt, mem_vt, shift0, wkv0, swa_cache, t_real, W):
    bsz, tp, _ = x.shape
    n = bsz * tp
    prompt = swa_cache is None
    mem_bb = bsz if prompt else STEP_MEM_SEQS
    act_dtype = bf16 if prompt else f32
    tm = ROW_TILE
    flat = lambda a: a.reshape(n, a.shape[-1])
    unflat = lambda a: a.reshape(bsz, tp, a.shape[-1])
    cos2, sin2 = _rope_tables(pos, HEADS_PER_SLAB)

    prep_w = (W["shift_mu"], W["w_w2"], W["w0"], W["w_a2"], W["a0"], W["w_g2"], W["k_k"], W["k_a"], W["r_k"])
    if prompt:
        seqs, q_mem, new_shift = _proj_prep(x, W["norm_mix"][0], W["w_in_a"], shift0, prep_w, PREP_TILE, bf16)
        q_col = 0
        main, new_wkv = _wkv_chunked(*seqs, W["lnx_w"], W["lnx_b"])
    else:
        q_mem = unflat(_norm_matmul(flat(x), W["norm_mix"][0], W["w_in_a"], tm))
        q_col = RWKV_COLS // MEM_WIDTH
        new_shift = q_mem[:, t_real - 1, :RWKV_COLS]
        steps = jnp.transpose(q_mem[:, :t_real, :RWKV_COLS], (1, 0, 2)).reshape(1, t_real * bsz, RWKV_COLS)
        seqs = [s.reshape(t_real, bsz, MAIN_WIDTH) for s in _rwkv_prep(steps, shift0, prep_w)]
        main, new_wkv = _wkv_steps(wkv0, seqs, W["lnx_w"], W["lnx_b"], tp)
    mem_o = _mem_attn(q_mem, q_col, W["mem_qnorm"][0], mem_kt, mem_vt, 0, MEM_Q_TILE, mem_bb, act_dtype)
    tables = (cos2, sin2) if prompt else (jnp.tile(cos2, (tm // tp, 1)), jnp.tile(sin2, (tm // tp, 1)))
    h, k_sh, v_sh, proj = _out_mlp(flat(x), flat(main), flat(mem_o), 0, W["w_out"], W["norm_mlp"][0], W["w_up"],
                                   W["w_down"], tm,
                                   nxt=(W["kv_norm"], W["w_kv"], W["swa_knorm"], *tables, W["norm_mix"][1], W["w_in_b"]))
    k_sh, v_sh, proj = unflat(k_sh), unflat(v_sh), unflat(proj)

    if prompt:
        main = _swa(proj, W["swa_qnorm"], W["sinks"], cos2, sin2, k_sh, v_sh)
        win = min(WINDOW, tp)
        heads = lambda a: a.reshape(bsz, a.shape[1], SWA_KV_HEADS, HEAD_DIM)
        k_state, v_state = heads(k_sh[:, tp - win:]), heads(v_sh[:, tp - win:])
    else:
        ckt, cvt = swa_cache
        main, kt_new, vt_new = _swa_step(proj, W["swa_qnorm"], W["sinks"], cos2, sin2, ckt, cvt, k_sh, v_sh,
                                         t_real, bb=STEP_SWA_SEQS)
        untransposed = lambda a: jnp.transpose(a.reshape(bsz, SWA_KV_HEADS, HEAD_DIM, -1), (0, 3, 1, 2))
        k_state, v_state = untransposed(kt_new), untransposed(vt_new)
    mem_o = _mem_attn(proj, MAIN_WIDTH // MEM_WIDTH, W["mem_qnorm"][1], mem_kt, mem_vt, 1, MEM_Q_TILE, mem_bb, act_dtype)
    y = _out_mlp(h, flat(main), flat(mem_o), 1, W["w_out"], W["norm_mlp"][1], W["w_up"], W["w_down"], tm)
    return unflat(y)[:, :t_real], new_shift[None], new_wkv[None], k_state, v_state


def kernel(x_prompt, x_sample, state_rwkv_shift, state_rwkv_wkv, cache_swa_k, cache_swa_v, cache_mem_k,
           cache_mem_v, mem_prompt, norm_mix, norm_mlp, w_out, w_up, w_down, mem_norm, w_mem_kv, mem_qnorm,
           mem_knorm, w_in_a, shift_mu, w_w2, w0, w_a2, a0, w_g2, k_k, k_a, r_k, lnx_w, lnx_b, w_in_b,
           swa_qnorm, sinks, kv_norm, w_kv, swa_knorm):
    W = dict(norm_mix=norm_mix, norm_mlp=norm_mlp, w_out=w_out.astype(bf16), w_up=w_up.astype(bf16),
             w_down=w_down.astype(bf16), mem_qnorm=mem_qnorm, w_in_a=w_in_a[0].astype(bf16),
             shift_mu=shift_mu[0], w_w2=w_w2[0], w0=w0[0], w_a2=w_a2[0], a0=a0[0], w_g2=w_g2[0],
             k_k=k_k[0], k_a=k_a[0], r_k=r_k[0].reshape(-1), lnx_w=lnx_w[0], lnx_b=lnx_b[0],
             w_in_b=w_in_b[0].astype(bf16), swa_qnorm=swa_qnorm[0], sinks=sinks[0], kv_norm=kv_norm,
             w_kv=w_kv.astype(bf16), swa_knorm=swa_knorm)
    bp, tp, _ = x_prompt.shape
    bs, ts, _ = x_sample.shape
    depth = norm_mix.shape[0]

    mem_flat = mem_prompt.reshape(bp * N_MEM, D_MODEL)
    p_mem_k, p_mem_v = [], []
    for l in range(depth):
        kv = _norm_matmul(mem_flat, mem_norm[l], w_mem_kv[l].astype(bf16), ROW_TILE)
        mk, mv = _kv_post(kv, mem_knorm[l], N_MEM)
        p_mem_k.append(mk.reshape(bp, N_MEM, MEM_WIDTH))
        p_mem_v.append(mv.reshape(bp, N_MEM, MEM_WIDTH))
    transposed = lambda a: jnp.swapaxes(jnp.stack(a), 2, 3)
    y_p, p_shift, p_wkv, p_k, p_v = _trunk(
        x_prompt, jnp.arange(tp), transposed(p_mem_k), transposed(p_mem_v), jnp.zeros((bp, RWKV_COLS), f32),
        None, None, tp, W)
    mem_heads = lambda a: jnp.stack(a).reshape(depth, bp, N_MEM, MEM_HEADS, HEAD_DIM)
    cache_t = lambda a: jnp.transpose(a, (0, 1, 3, 4, 2)).reshape(depth, bs, MEM_WIDTH, N_MEM)

    tpad = -(-ts // SUBLANES) * SUBLANES
    x_s = jnp.pad(x_sample, ((0, 0), (0, tpad - ts), (0, 0)))
    win = cache_swa_k.shape[1]
    assert win == WINDOW
    swa_t = lambda a: jnp.transpose(a, (0, 2, 3, 1)).reshape(bs, KV_WIDTH, win)
    y_s, s_shift, s_wkv, s_k, s_v = _trunk(
        x_s, PAST_LEN + jnp.arange(tpad), cache_t(cache_mem_k), cache_t(cache_mem_v),
        state_rwkv_shift[0], state_rwkv_wkv[0],
        (swa_t(cache_swa_k), swa_t(cache_swa_v)), ts, W)

    return (y_p, y_s, p_shift, p_wkv, p_k, p_v, mem_heads(p_mem_k), mem_heads(p_mem_v),
            s_shift, s_wkv, s_k, s_v)
```

```python
import functools
import math

import jax
import jax.numpy as jnp
from jax import lax
from jax.experimental import pallas as pl
from jax.experimental.pallas import tpu as pltpu

f32 = jnp.float32
bf16 = jnp.bfloat16

D_MODEL = 1024
HEAD_DIM = 64
MEM_HEADS = 4
MEM_WIDTH = MEM_HEADS * HEAD_DIM
MAIN_WIDTH = D_MODEL - MEM_WIDTH
RWKV_HEADS = MAIN_WIDTH // HEAD_DIM
DECAY_LORA = 64
AAA_LORA = 64
GATE_LORA = 128
RWKV_COLS = 3 * MAIN_WIDTH + DECAY_LORA + AAA_LORA + GATE_LORA
A_IN_COLS = RWKV_COLS + MEM_WIDTH
SWA_Q_HEADS = MAIN_WIDTH // HEAD_DIM
SWA_KV_HEADS = 4
SWA_GROUP = SWA_Q_HEADS // SWA_KV_HEADS
KV_WIDTH = SWA_KV_HEADS * HEAD_DIM
WINDOW = 128
BLOCK = 128
N_MEM = 256
D_FF = 4 * D_MODEL
PAST_LEN = 16384
ROPE_THETA = 10000.0
NORM_EPS = 1e-6
LNX_EPS = 6.4e-4
L2_EPS = 1e-12
ATTN_SCALE = HEAD_DIM ** -0.5

LANES = 128
SUBLANES = 8
HEADS_PER_SLAB = LANES // HEAD_DIM
N_SLABS = MAIN_WIDTH // LANES
CHUNK = 64
WKV_CHUNKS_PER_STEP = 16
VMEM_LIMIT = 56 * 1024 * 1024
NEG_BIG = -1e30
ROW_TILE = 512
PREP_TILE = 512
MEM_Q_TILE = 512
MEM_Q_SUB = 256
STEP_MEM_SEQS = 16
STEP_SWA_SEQS = 8


def _cp(sem, vmem=VMEM_LIMIT):
    return pltpu.CompilerParams(dimension_semantics=sem, vmem_limit_bytes=vmem)


def _dot(a, b):
    return jnp.dot(a, b, preferred_element_type=f32)


def _dot_nt(a, b):
    return lax.dot_general(a, b, (((1,), (1,)), ((), ())), preferred_element_type=f32)


def _dot_tn(a, b):
    return lax.dot_general(a, b, (((0,), (0,)), ((), ())), preferred_element_type=f32)


def _iota(shape, dim):
    return lax.broadcasted_iota(jnp.int32, shape, dim)


def _group_ones():
    r = _iota((LANES, LANES), 0)
    c = _iota((LANES, LANES), 1)
    return jnp.where((r < HEAD_DIM) == (c < HEAD_DIM), 1.0, 0.0).astype(f32)


def _head_sum(x, gmat, passes=2):
    w = x.shape[-1]
    gb = gmat.astype(bf16)
    hi = x.astype(bf16)
    slabs = lambda y: [y[:, j * LANES:(j + 1) * LANES] for j in range(w // LANES)]
    parts = [_dot(h, gb) for h in slabs(hi)]
    if passes == 2:
        lo = (x - hi.astype(f32)).astype(bf16)
        parts = [p + _dot(l, gb) for p, l in zip(parts, slabs(lo))]
    return parts[0] if len(parts) == 1 else jnp.concatenate(parts, axis=-1)


def _cumsum_rows(x):
    rows = x.shape[0]
    row = _iota(x.shape, 0)
    step = 1
    while step < rows:
        x = x + jnp.where(row >= step, pltpu.roll(x, step, 0), 0.0)
        step *= 2
    return x


def _sigmoid(x):
    return 1.0 / (1.0 + jnp.exp(-x))


def _rot_half(x):
    lane = _iota(x.shape, 1)
    first = (lane & (HEAD_DIM - 1)) < (HEAD_DIM // 2)
    return jnp.where(first, pltpu.roll(x, LANES - HEAD_DIM // 2, 1), pltpu.roll(x, HEAD_DIM // 2, 1))


def _norm_matmul_kernel(x_ref, g_ref, w_ref, o_ref):
    x = x_ref[...]
    xn = x * lax.rsqrt(jnp.mean(x * x, axis=-1, keepdims=True) + NORM_EPS) * g_ref[...]
    o_ref[...] = _dot(xn.astype(bf16), w_ref[...])


def _norm_matmul(x, g, w_bf16, tm):
    n, d = x.shape
    m = w_bf16.shape[1]
    tm = min(tm, n)
    return pl.pallas_call(
        _norm_matmul_kernel,
        grid=(n // tm,),
        in_specs=[pl.BlockSpec((tm, d), lambda i: (i, 0)),
                  pl.BlockSpec((1, d), lambda i: (0, 0)),
                  pl.BlockSpec((d, m), lambda i: (0, 0))],
        out_specs=pl.BlockSpec((tm, m), lambda i: (i, 0)),
        out_shape=jax.ShapeDtypeStruct((n, m), f32),
        compiler_params=_cp(("parallel",)),
        name="norm_matmul",
    )(x, g.reshape(1, d), w_bf16)


def _head_norm_rope(x, g, cos, sin, gmat):
    xn = x * lax.rsqrt(_head_sum(x * x, gmat) * (1.0 / HEAD_DIM) + NORM_EPS) * g
    slabs = [xn[:, j * LANES:(j + 1) * LANES] for j in range(x.shape[-1] // LANES)]
    return [xs * cos + _rot_half(xs) * sin for xs in slabs]


def _kv_post_kernel(kv_ref, g_ref, k_ref, v_ref):
    gmat = _group_ones()
    x = kv_ref[:, :KV_WIDTH]
    k_ref[...] = x * lax.rsqrt(_head_sum(x * x, gmat) * (1.0 / HEAD_DIM) + NORM_EPS) * g_ref[...]
    v_ref[...] = kv_ref[:, KV_WIDTH:]


def _kv_post(kv, g_head, tm):
    n = kv.shape[0]
    tm = min(tm, n)
    g = jnp.tile(g_head.reshape(1, HEAD_DIM), (1, KV_WIDTH // HEAD_DIM))
    return pl.pallas_call(
        _kv_post_kernel,
        grid=(n // tm,),
        in_specs=[pl.BlockSpec((tm, 2 * KV_WIDTH), lambda i: (i, 0)),
                  pl.BlockSpec((1, KV_WIDTH), lambda i: (0, 0))],
        out_specs=[pl.BlockSpec((tm, KV_WIDTH), lambda i: (i, 0)),
                   pl.BlockSpec((tm, KV_WIDTH), lambda i: (i, 0))],
        out_shape=[jax.ShapeDtypeStruct((n, KV_WIDTH), f32)] * 2,
        compiler_params=_cp(("parallel",)),
        name="kv_post",
    )(kv, g)


def _time_mix_inputs(pf, prev, lora_pf, lora_prev, consts, outs):
    mu_ref, ww2_ref, w0_ref, wa2_ref, a0_ref, wg2_ref, kk_ref, ka_ref, rk_ref = consts
    r_o, lw_o, k_o, v_o, a_o, b_o, bonus_o, g_o = outs
    gmat = _group_ones()
    i1, i2, i3 = MAIN_WIDTH, 2 * MAIN_WIDTH, 3 * MAIN_WIDTH
    ls = lora_pf + (lora_prev - lora_pf) * mu_ref[:, i3:]
    wd, ad, gd = ls[:, :DECAY_LORA], ls[:, DECAY_LORA:DECAY_LORA + AAA_LORA], ls[:, DECAY_LORA + AAA_LORA:]
    lora = lambda x, w_ref: _dot(x.astype(bf16), w_ref[...].astype(bf16))
    lw_o[0] = -math.exp(-0.5) * _sigmoid(w0_ref[...] + lora(jnp.tanh(wd), ww2_ref))
    a_sig = _sigmoid(a0_ref[...] + lora(ad, wa2_ref))
    g_o[0] = lora(_sigmoid(gd), wg2_ref).astype(g_o.dtype)
    ps = pf + (prev - pf) * mu_ref[:, :i3]
    r, k, v = ps[:, :i1], ps[:, i1:i2], ps[:, i2:i3]
    kk = k * kk_ref[...]
    kk = kk * jnp.minimum(lax.rsqrt(_head_sum(kk * kk, gmat, passes=1)), 1.0 / L2_EPS)
    k2 = k * (1.0 + (a_sig - 1.0) * ka_ref[...])
    r_o[0] = r.astype(r_o.dtype)
    k_o[0] = k2.astype(k_o.dtype)
    v_o[0] = v.astype(v_o.dtype)
    a_o[0] = (-kk).astype(a_o.dtype)
    b_o[0] = (kk * a_sig).astype(b_o.dtype)
    bonus_o[0] = (_head_sum(r * k2 * rk_ref[...], gmat, passes=1) * v).astype(bonus_o.dtype)


def _rwkv_prep_kernel(p_ref, shift_ref, *refs):
    pf = p_ref[0]
    prev = jnp.concatenate([shift_ref[0], pf[:pf.shape[0] - shift_ref.shape[1]]], axis=0)
    i3 = 3 * MAIN_WIDTH
    _time_mix_inputs(pf[:, :i3], prev[:, :i3], pf[:, i3:], prev[:, i3:], refs[:9], refs[9:])


def _proj_prep_kernel(x_ref, gx_ref, w_ref, shift_ref, *refs):
    consts, outs, (q_o, last_o, carry_ref) = refs[:9], refs[9:17], refs[17:]
    i = pl.program_id(1)
    i3 = 3 * MAIN_WIDTH
    x = x_ref[0]
    xn = (x * lax.rsqrt(jnp.mean(x * x, axis=-1, keepdims=True) + NORM_EPS) * gx_ref[...]).astype(bf16)
    lora_pf = _dot(xn, w_ref[:, i3:RWKV_COLS])
    pf = _dot(xn, w_ref[:, :i3])
    q_o[0] = _dot(xn, w_ref[:, RWKV_COLS:])
    first_row = _iota((pf.shape[0], 1), 0) == 0
    first_prev = jnp.where(i == 0, shift_ref[0], carry_ref[...])

    def shifted(p, cols):
        return jnp.where(first_row, first_prev[:, cols], pltpu.roll(p, 1, 0))

    lora_prev, prev = shifted(lora_pf, slice(i3, RWKV_COLS)), shifted(pf, slice(0, i3))
    last = jnp.concatenate([pf[pf.shape[0] - 1:, :], lora_pf[pf.shape[0] - 1:, :]], axis=1)
    carry_ref[...] = last
    last_o[0] = last
    _time_mix_inputs(pf, prev, lora_pf, lora_prev, consts, outs)


def _proj_prep(x, g_norm, w_in_bf16, shift_prev, prep_w, tt, seq_dtype):
    bsz, t, _ = x.shape
    row = lambda a: a.reshape(1, -1)
    full = lambda a: pl.BlockSpec(a.shape, lambda b, i: (0,) * a.ndim)
    mu, w_w2, w0, w_a2, a0, w_g2, k_k, k_a, r_k = prep_w
    consts = [row(mu), w_w2, row(w0), w_a2, row(a0), w_g2, row(k_k), row(k_a), row(r_k)]
    tile = lambda w: pl.BlockSpec((1, tt, w), lambda b, i: (b, i, 0))
    per_seq = pl.BlockSpec((1, 1, RWKV_COLS), lambda b, i: (b, 0, 0))
    res = pl.pallas_call(
        _proj_prep_kernel,
        grid=(bsz, t // tt),
        in_specs=[tile(D_MODEL), full(row(g_norm)), full(w_in_bf16), per_seq] + [full(c) for c in consts],
        out_specs=[tile(MAIN_WIDTH)] * 8 + [tile(MEM_WIDTH), per_seq],
        out_shape=[jax.ShapeDtypeStruct((bsz, t, MAIN_WIDTH), f32 if j == 1 else seq_dtype) for j in range(8)]
                  + [jax.ShapeDtypeStruct((bsz, t, MEM_WIDTH), f32), jax.ShapeDtypeStruct((bsz, 1, RWKV_COLS), f32)],
        scratch_shapes=[pltpu.VMEM((1, RWKV_COLS), f32)],
        compiler_params=_cp(("parallel", "arbitrary")),
        name="proj_prep",
    )(x, row(g_norm), w_in_bf16, shift_prev.reshape(bsz, 1, RWKV_COLS), *consts)
    return res[:8], res[8], res[9].reshape(bsz, RWKV_COLS)


def _rwkv_prep(steps, shift_prev, prep_w):
    _, t, _ = steps.shape
    row = lambda a: a.reshape(1, -1)
    full = lambda a: pl.BlockSpec(a.shape, lambda i: (0,) * a.ndim)
    mu, w_w2, w0, w_a2, a0, w_g2, k_k, k_a, r_k = prep_w
    consts = [row(mu), w_w2, row(w0), w_a2, row(a0), w_g2, row(k_k), row(k_a), row(r_k)]
    shift_rows = shift_prev[None]
    return pl.pallas_call(
        _rwkv_prep_kernel,
        grid=(1,),
        in_specs=[full(steps), full(shift_rows)] + [full(c) for c in consts],
        out_specs=[pl.BlockSpec((1, t, MAIN_WIDTH), lambda i: (0, 0, 0))] * 8,
        out_shape=[jax.ShapeDtypeStruct((1, t, MAIN_WIDTH), f32)] * 8,
        compiler_params=_cp(("arbitrary",)),
        name="rwkv_prep",
    )(steps, shift_rows, *consts)


def _wkv_chunk_kernel(r_ref, lw_ref, k_ref, v_ref, a_ref, b_ref, bonus_ref, g_ref, lnw_ref, lnb_ref,
                      o_ref, s_ref, h_ref, qt_ref, yv_ref, pp_ref, hv_ref, tc_ref):
    i = pl.program_id(2)
    c = CHUNK
    nc = r_ref.shape[1] // c
    n = range(nc)

    @pl.when(i == 0)
    def _():
        h_ref[...] = jnp.zeros_like(h_ref)
        qt_ref[...] = jnp.zeros_like(qt_ref)
        yv_ref[...] = jnp.zeros_like(yv_ref)
        pp_ref[...] = jnp.zeros_like(pp_ref)
        hv_ref[...] = jnp.zeros_like(hv_ref)
        tc_ref[...] = jnp.zeros_like(tc_ref)

    gmat = _group_ones()
    bd = gmat > 0.5
    row = _iota((LANES, LANES), 0)
    col = _iota((LANES, LANES), 1)
    t_row = row & (c - 1)
    s_col = col & (c - 1)
    m_strict = s_col < t_row
    m_incl = s_col <= t_row
    top = row < c
    lane_lo = _iota((1, LANES), 1) < HEAD_DIM
    eye = row == col
    zeros = jnp.zeros((c, LANES), f32)
    mm = lambda x: x.astype(bf16)
    gb = gmat.astype(bf16)
    sls = [slice(j * c, (j + 1) * c) for j in n]

    def pending_state():
        state = {"h": h_ref[...], "ys": []}

        def state_step(j):
            hb = mm(state["h"])
            state["ys"].append(_dot(qt_ref[j], hb) + yv_ref[j])
            state["h"] = state["h"] * tc_ref[j] + _dot(pp_ref[j], hb) + hv_ref[j]

        def finish():
            h_ref[...] = state["h"]
            means = [_dot(mm(y), gb) * (1.0 / HEAD_DIM) for y in state["ys"]]
            ycs = [state["ys"][j] - means[j] for j in n]
            vrs = [_dot(mm(yc * yc), gb) * (1.0 / HEAD_DIM) for yc in ycs]
            outs = []
            for j in n:
                yn = ycs[j] * lax.rsqrt(vrs[j] + LNX_EPS) * lnw_ref[...] + lnb_ref[...]
                outs.append(((yn + bonus_ref[0, sls[j], :].astype(f32))
                             * g_ref[0, sls[j], :].astype(f32)).astype(o_ref.dtype))
            return outs

        return state_step, finish

    def chunk_group(js):
        st = {}

        def decay_and_blocks():
            st["vs"], st["tots"], st["ats"], st["rts"], st["bkts"], gms = [], [], [], [], [], []
            for j in js:
                sl = sls[j]
                lw = lw_ref[0, sl, :]
                cum = _cumsum_rows(lw)
                tot = cum[c - 1:c, :]
                b, k = b_ref[0, sl, :].astype(f32), k_ref[0, sl, :].astype(f32)
                e_neg, e_rem = jnp.exp(-cum), jnp.exp(tot - cum)
                at = a_ref[0, sl, :].astype(f32) * jnp.exp(cum - lw)
                rt = r_ref[0, sl, :].astype(f32) * jnp.exp(cum)
                bk = mm(jnp.concatenate([b * e_neg, k * e_neg], axis=0))
                lhs = mm(jnp.concatenate([jnp.where(lane_lo, at, 0.0), jnp.where(lane_lo, 0.0, at),
                                          jnp.where(lane_lo, rt, 0.0), jnp.where(lane_lo, 0.0, rt)], axis=0))
                gms.append(_dot_nt(lhs, bk))
                st["vs"].append(v_ref[0, sl, :].astype(f32))
                st["tots"].append(tot)
                st["ats"].append(at)
                st["rts"].append(rt)
                st["bkts"].append(mm(jnp.concatenate([b * e_rem, k * e_rem], axis=0)))
            st["gms"] = gms

        def first_products():
            m = range(len(js))
            aaks = [jnp.where(m_strict, gm[:LANES], 0.0) for gm in st["gms"]]
            st["arks"] = [mm(jnp.where(m_incl, gm[LANES:], 0.0)) for gm in st["gms"]]
            aak_sws = [pltpu.roll(aak, HEAD_DIM, 1) for aak in aaks]
            st["ms"] = [jnp.where(bd, jnp.where(top, aaks[q], aak_sws[q]), 0.0) for q in m]
            aks = [mm(jnp.where(bd, jnp.where(top, aak_sws[q], aaks[q]), 0.0)) for q in m]
            vvs = [mm(jnp.concatenate([pltpu.roll(v, HEAD_DIM, 1)] * 2, axis=0)) for v in st["vs"]]
            akvs = [_dot(aks[q], vvs[q]) for q in m]
            st["xs"] = [jnp.where(bd, jnp.concatenate([st["ats"][q]] * 2, axis=0), akvs[q]) for q in m]

        def square(last):
            def stage():
                m = range(len(js))
                ms, xs = st["ms"], st["xs"]
                if not last:
                    zs = [_dot(mm(ms[q]), mm(jnp.concatenate([xs[q], ms[q]], axis=1))) for q in m]
                    st["xs"] = [xs[q] + zs[q][:, :LANES] for q in m]
                    st["ms"] = [zs[q][:, LANES:] for q in m]
                else:
                    zs = [_dot(mm(ms[q]), mm(xs[q])) for q in m]
                    st["xs"] = [xs[q] + zs[q] for q in m]
            return stage

        def last_products():
            m = range(len(js))
            rhss = []
            for q in m:
                x = st["xs"][q]
                ta = jnp.where(lane_lo, x[:c], x[c:])
                uv = pltpu.roll(jnp.where(lane_lo, x[c:], x[:c]), HEAD_DIM, 1)
                rhss.append(mm(jnp.concatenate([jnp.concatenate([ta, uv], axis=1),
                                                jnp.concatenate([zeros, st["vs"][q]], axis=1)], axis=0)))
            st["z2s"] = [_dot(st["arks"][q], rhss[q]) for q in m]
            st["pzs"] = [_dot_tn(st["bkts"][q], rhss[q]) for q in m]

        def make_pending():
            for q, j in enumerate(js):
                z2, pz = st["z2s"][q], st["pzs"][q]
                qt_ref[j] = mm(st["rts"][q] + jnp.where(lane_lo, z2[:c, :LANES], z2[c:, :LANES]))
                yv_ref[j] = jnp.where(lane_lo, z2[:c, LANES:], z2[c:, LANES:])
                pp_ref[j] = mm(jnp.where(bd, pz[:, :LANES], 0.0))
                hv_ref[j] = jnp.where(bd, pz[:, LANES:], 0.0)
                tc_ref[j] = jnp.sum(jnp.where(eye, jnp.exp(st["tots"][q]), 0.0), axis=1, keepdims=True)

        return ([decay_and_blocks, first_products] + [square(False)] * 5 + [square(True), last_products], make_pending)

    @pl.when(i < pl.num_programs(2) - 1)
    def _():
        half = nc // 2
        (stages_a, pend_a), (stages_b, pend_b) = chunk_group(list(range(half))), chunk_group(list(range(half, nc)))
        lag = 4
        state_step, finish = pending_state()
        outs = None
        for slot in range(max(len(stages_a) + lag, nc + 1)):
            if slot < nc:
                state_step(slot)
            if slot == nc:
                outs = finish()
            if slot < len(stages_a):
                stages_a[slot]()
            if 0 <= slot - lag < len(stages_b):
                stages_b[slot - lag]()
        for j in n:
            o_ref[0, sls[j], :] = outs[j]
        pend_a()
        pend_b()

    @pl.when(i == pl.num_programs(2) - 1)
    def _():
        state_step, finish = pending_state()
        for j in n:
            state_step(j)
        outs = finish()
        for j in n:
            o_ref[0, sls[j], :] = outs[j]

    @pl.when(i == pl.num_programs(2) - 1)
    def _():
        s_ref[0, 0] = h_ref[...].T


def _wkv_chunked(r, lw, k, v, a, b, bonus, g, lnx_w, lnx_b):
    bsz, t, _ = r.shape
    nc = WKV_CHUNKS_PER_STEP
    tt = nc * CHUNK
    nt = t // tt
    assert t % tt == 0
    cur = pl.BlockSpec((1, tt, LANES), lambda bb, hp, i: (bb, jnp.minimum(i, nt - 1), hp))
    pend = pl.BlockSpec((1, tt, LANES), lambda bb, hp, i: (bb, jnp.maximum(i - 1, 0), hp))
    vec = pl.BlockSpec((1, LANES), lambda bb, hp, i: (0, hp))
    out, st = pl.pallas_call(
        _wkv_chunk_kernel,
        grid=(bsz, N_SLABS, nt + 1),
        in_specs=[cur] * 6 + [pend, pend, vec, vec],
        out_specs=[pend, pl.BlockSpec((1, 1, LANES, LANES), lambda bb, hp, i: (bb, hp, 0, 0))],
        out_shape=[jax.ShapeDtypeStruct((bsz, t, MAIN_WIDTH), bf16),
                   jax.ShapeDtypeStruct((bsz, N_SLABS, LANES, LANES), f32)],
        scratch_shapes=[pltpu.VMEM((LANES, LANES), f32),
                        pltpu.VMEM((nc, CHUNK, LANES), bf16), pltpu.VMEM((nc, CHUNK, LANES), f32),
                        pltpu.VMEM((nc, LANES, LANES), bf16), pltpu.VMEM((nc, LANES, LANES), f32),
                        pltpu.VMEM((nc, LANES, 1), f32)],
        compiler_params=_cp(("parallel", "parallel", "arbitrary")),
        name="wkv_chunked",
    )(r, lw, k, v, a, b, bonus, g, lnx_w.reshape(1, -1), lnx_b.reshape(1, -1))
    st = jnp.stack([st[:, :, :HEAD_DIM, :HEAD_DIM], st[:, :, HEAD_DIM:, HEAD_DIM:]], axis=2)
    return out, st.reshape(bsz, RWKV_HEADS, HEAD_DIM, HEAD_DIM)


def _wkv_steps_kernel(s_ref, seq_ref, lnw_ref, lnb_ref, o_ref, so_ref, w_ref, y_ref, *, n_steps):
    i_r, i_lw, i_k, i_v, i_a, i_b, i_bonus, i_g = range(8)
    w_ref[...] = jnp.exp(seq_ref[i_lw])
    sub = _iota((SUBLANES, LANES), 0)

    def group(vg, carry):
        v0 = pl.multiple_of(vg * SUBLANES, SUBLANES)
        v_rows = [seq_ref[i_v, t, pl.ds(v0, SUBLANES), :] for t in range(n_steps)]
        ys = [jnp.zeros((SUBLANES, LANES), f32) for _ in range(n_steps)]
        for j in range(SUBLANES):
            sv = s_ref[0, v0 + j]
            for t in range(n_steps):
                sa = jnp.sum(sv * seq_ref[i_a, t], axis=0, keepdims=True)
                sv = sv * w_ref[t] + sa * seq_ref[i_b, t] + v_rows[t][j:j + 1, :] * seq_ref[i_k, t]
                y = jnp.sum(sv * seq_ref[i_r, t], axis=0, keepdims=True)
                ys[t] = jnp.where(sub == j, y, ys[t])
            so_ref[0, v0 + j] = sv
        for t in range(n_steps):
            y_ref[t, pl.ds(v0, SUBLANES), :] = ys[t]
        return carry

    lax.fori_loop(0, HEAD_DIM // SUBLANES, group, 0)
    y = y_ref[...]
    mean = jnp.mean(y, axis=1, keepdims=True)
    yc = y - mean
    var = jnp.mean(yc * yc, axis=1, keepdims=True)
    yn = yc * lax.rsqrt(var + LNX_EPS) * lnw_ref[...] + lnb_ref[...]
    o_ref[...] = (yn + seq_ref[i_bonus]) * seq_ref[i_g]


def _wkv_steps(state, seqs, lnx_w, lnx_b, tp):
    n_steps, bsz, _ = seqs[0].shape
    assert bsz % LANES == 0
    state_t = jnp.transpose(state, (1, 2, 3, 0))
    seq_t = jnp.transpose(jnp.stack(seqs), (0, 1, 3, 2))
    lanes = lambda x: jnp.broadcast_to(x.reshape(MAIN_WIDTH, 1), (MAIN_WIDTH, LANES))
    sspec = pl.BlockSpec((1, HEAD_DIM, HEAD_DIM, LANES), lambda h, bi: (h, 0, 0, bi))
    vec = pl.BlockSpec((HEAD_DIM, LANES), lambda h, bi: (h, 0))
    ospec = pl.BlockSpec((n_steps, HEAD_DIM, LANES), lambda h, bi: (0, h, bi))
    out, new_state = pl.pallas_call(
        functools.partial(_wkv_steps_kernel, n_steps=n_steps),
        grid=(RWKV_HEADS, bsz // LANES),
        in_specs=[sspec, pl.BlockSpec((8, n_steps, HEAD_DIM, LANES), lambda h, bi: (0, 0, h, bi)), vec, vec],
        out_specs=[ospec, sspec],
        out_shape=[jax.ShapeDtypeStruct((n_steps, MAIN_WIDTH, bsz), f32),
                   jax.ShapeDtypeStruct(state_t.shape, f32)],
        scratch_shapes=[pltpu.VMEM((n_steps, HEAD_DIM, LANES), f32), pltpu.VMEM((n_steps, HEAD_DIM, LANES), f32)],
        compiler_params=_cp(("parallel", "parallel")),
        name="wkv_steps",
    )(state_t, seq_t, lanes(lnx_w), lanes(lnx_b))
    out = jnp.pad(jnp.transpose(out, (2, 0, 1)), ((0, 0), (0, tp - n_steps), (0, 0)))
    return out, jnp.transpose(new_state, (3, 0, 1, 2))


def _mem_attn_kernel(q_ref, g_ref, kt_ref, vt_ref, o_ref, *, sub):
    gmat = _group_ones()
    bb, tq, _ = q_ref.shape
    head = _iota((1, MEM_WIDTH), 1) // HEAD_DIM
    tiles = [(b, pl.ds(r * sub, sub)) for b in range(bb) for r in range(tq // sub)]
    qs = []
    for b, rows in tiles:
        q = q_ref[b, rows, :]
        qn = q * lax.rsqrt(_head_sum(q * q, gmat, passes=1) * (1.0 / HEAD_DIM) + NORM_EPS) * (g_ref[...] * ATTN_SCALE)
        qs.append(jnp.concatenate([jnp.where(head == h, qn, 0.0) for h in range(MEM_HEADS)], axis=0).astype(bf16))
    ss = [_dot(q4, kt_ref[0, b].astype(bf16)) for q4, (b, _) in zip(qs, tiles)]
    es = [jnp.exp(s - jnp.max(s, axis=-1, keepdims=True)) for s in ss]
    o4s = [_dot_nt(e.astype(bf16), vt_ref[0, b].astype(bf16)) / jnp.sum(e, axis=-1, keepdims=True)
           for e, (b, _) in zip(es, tiles)]
    for o4, (b, rows) in zip(o4s, tiles):
        o = jnp.zeros((sub, MEM_WIDTH), f32)
        for h in range(MEM_HEADS):
            o = o + jnp.where(head == h, o4[h * sub:(h + 1) * sub], 0.0)
        o_ref[b, rows, :] = o.astype(o_ref.dtype)


def _mem_attn(proj, col_block, g_qnorm, mem_kt, mem_vt, layer, tq, bb, out_dtype):
    bsz, t, _ = proj.shape
    tq = min(tq, t)
    g = jnp.tile(g_qnorm.reshape(1, HEAD_DIM), (1, MEM_HEADS))
    kv_spec = pl.BlockSpec((1, bb, MEM_WIDTH, N_MEM), lambda b, i: (layer, b, 0, 0))
    return pl.pallas_call(
        functools.partial(_mem_attn_kernel, sub=min(tq, MEM_Q_SUB)),
        grid=(bsz // bb, t // tq),
        in_specs=[pl.BlockSpec((bb, tq, MEM_WIDTH), lambda b, i: (b, i, col_block)),
                  pl.BlockSpec((1, MEM_WIDTH), lambda b, i: (0, 0)),
                  kv_spec, kv_spec],
        out_specs=pl.BlockSpec((bb, tq, MEM_WIDTH), lambda b, i: (b, i, 0)),
        out_shape=jax.ShapeDtypeStruct((bsz, t, MEM_WIDTH), out_dtype),
        compiler_params=_cp(("parallel", "parallel")),
        name="mem_attn",
    )(proj, g, mem_kt, mem_vt)


def _swa_kernel(sink_ref, q_ref, g_ref, cos_ref, sin_ref, kp_ref, kc_ref, vp_ref, vc_ref, gm_ref, mkt_ref, mvt_ref,
                o_ref, mo_ref, qn_ref):
    i = pl.program_id(0)
    gmat = _group_ones()
    bb, tq, _ = q_ref.shape
    head = _iota((1, MEM_WIDTH), 1) // HEAD_DIM
    mqs = []
    for b in range(bb):
        q = q_ref[b, :, MAIN_WIDTH:]
        qn = q * lax.rsqrt(_head_sum(q * q, gmat, passes=1) * (1.0 / HEAD_DIM) + NORM_EPS) * (gm_ref[...] * ATTN_SCALE)
        mqs.append(jnp.concatenate([jnp.where(head == h, qn, 0.0) for h in range(MEM_HEADS)], axis=0).astype(bf16))
    mss = [_dot(mqs[b], mkt_ref[0, b].astype(bf16)) for b in range(bb)]
    mes = [jnp.exp(ms_ - jnp.max(ms_, axis=-1, keepdims=True)) for ms_ in mss]
    mo4 = [_dot_nt(mes[b].astype(bf16), mvt_ref[0, b].astype(bf16)) / jnp.sum(mes[b], axis=-1, keepdims=True)
           for b in range(bb)]
    for b in range(bb):
        mo = jnp.zeros((tq, MEM_WIDTH), f32)
        for h in range(MEM_HEADS):
            mo = mo + jnp.where(head == h, mo4[b][h * tq:(h + 1) * tq], 0.0)
        mo_ref[b] = mo.astype(mo_ref.dtype)
    keys, vals = [], []
    for b in range(bb):
        for j in range(N_SLABS):
            sl = slice(j * LANES, (j + 1) * LANES)
            x = q_ref[b, :, sl]
            xn = x * lax.rsqrt(_head_sum(x * x, gmat, passes=1) * (1.0 / HEAD_DIM) + NORM_EPS) * g_ref[...]
            qn_ref[b, :, sl] = (xn * cos_ref[...] + _rot_half(xn) * sin_ref[...]) * ATTN_SCALE
        keys.append(jnp.concatenate([kp_ref[b], kc_ref[b]], axis=0).astype(bf16))
        vals.append(jnp.concatenate([vp_ref[b], vc_ref[b]], axis=0).astype(bf16))
    rows = SWA_GROUP * tq
    row = _iota((rows, 2 * BLOCK), 0) & (tq - 1)
    col = _iota((rows, 2 * BLOCK), 1)
    mask = (col > row) & (col <= row + WINDOW) & ((i > 0) | (col >= BLOCK))
    grp = _iota((rows, 1), 0) // tq
    pairs = [(b, hk) for b in range(bb) for hk in range(SWA_KV_HEADS)]
    hs = [slice(hk * HEAD_DIM, (hk + 1) * HEAD_DIM) for hk in range(SWA_KV_HEADS)]
    q3s = [jnp.concatenate([qn_ref[b, :, (hk * SWA_GROUP + gq) * HEAD_DIM:(hk * SWA_GROUP + gq + 1) * HEAD_DIM]
                            for gq in range(SWA_GROUP)], axis=0).astype(bf16) for b, hk in pairs]
    ss = [jnp.where(mask, _dot_nt(q3, keys[b][:, hs[hk]]), NEG_BIG) for q3, (b, hk) in zip(q3s, pairs)]
    sinks = [jnp.where(grp == 0, sink_ref[hk * SWA_GROUP],
                       jnp.where(grp == 1, sink_ref[hk * SWA_GROUP + 1], sink_ref[hk * SWA_GROUP + 2]))
             for hk in range(SWA_KV_HEADS)]
    ms = [jnp.maximum(jnp.max(s, axis=-1, keepdims=True), sinks[hk]) for s, (b, hk) in zip(ss, pairs)]
    es = [jnp.exp(s - m) for s, m in zip(ss, ms)]
    denoms = [jnp.sum(e, axis=-1, keepdims=True) + jnp.exp(sinks[hk] - m) for e, m, (b, hk) in zip(es, ms, pairs)]
    outs = [_dot(e.astype(bf16), vals[b][:, hs[hk]]) / d for e, d, (b, hk) in zip(es, denoms, pairs)]
    for o, (b, hk) in zip(outs, pairs):
        for gq in range(SWA_GROUP):
            hq = hk * SWA_GROUP + gq
            o_ref[b, :, hq * HEAD_DIM:(hq + 1) * HEAD_DIM] = o[gq * tq:(gq + 1) * tq].astype(o_ref.dtype)


def _swa(proj, g_qnorm, sinks, cos, sin, k, v, g_mem_qnorm, mem_kt, mem_vt, layer):
    bsz, t, _ = proj.shape
    assert SWA_GROUP == 3
    tq = BLOCK
    g = jnp.tile(g_qnorm.reshape(1, HEAD_DIM), (1, HEADS_PER_SLAB))
    gm = jnp.tile(g_mem_qnorm.reshape(1, HEAD_DIM), (1, MEM_HEADS))
    prev = pl.BlockSpec((bsz, BLOCK, KV_WIDTH), lambda i: (0, jnp.maximum(i - 1, 0), 0))
    cur = pl.BlockSpec((bsz, BLOCK, KV_WIDTH), lambda i: (0, i, 0))
    mem = pl.BlockSpec((1, bsz, MEM_WIDTH, N_MEM), lambda i: (layer, 0, 0, 0))
    return pl.pallas_call(
        _swa_kernel,
        grid=(t // tq,),
        in_specs=[pl.BlockSpec(memory_space=pltpu.SMEM),
                  pl.BlockSpec((bsz, tq, D_MODEL), lambda i: (0, i, 0)),
                  pl.BlockSpec((1, LANES), lambda i: (0, 0)),
                  pl.BlockSpec((tq, LANES), lambda i: (i, 0)),
                  pl.BlockSpec((tq, LANES), lambda i: (i, 0)),
                  prev, cur, prev, cur,
                  pl.BlockSpec((1, MEM_WIDTH), lambda i: (0, 0)), mem, mem],
        out_specs=[pl.BlockSpec((bsz, tq, MAIN_WIDTH), lambda i: (0, i, 0)),
                   pl.BlockSpec((bsz, tq, MEM_WIDTH), lambda i: (0, i, 0))],
        out_shape=[jax.ShapeDtypeStruct((bsz, t, MAIN_WIDTH), bf16), jax.ShapeDtypeStruct((bsz, t, MEM_WIDTH), bf16)],
        scratch_shapes=[pltpu.VMEM((bsz, tq, MAIN_WIDTH), f32)],
        compiler_params=_cp(("parallel",)),
        name="swa_attn",
    )(sinks, proj, g, cos, sin, k, k, v, v, gm, mem_kt, mem_vt)


def _swa_step_kernel(sink_ref, q_ref, g_ref, cos_ref, sin_ref, kt_ref, vt_ref, kn_ref, vn_ref,
                     o_ref, kto_ref, vto_ref, qn_ref, *, n_new):
    gmat = _group_ones()
    bb, tq, _ = q_ref.shape
    row = _iota((SWA_GROUP * tq, 2 * BLOCK), 0) & (tq - 1)
    col = _iota((SWA_GROUP * tq, 2 * BLOCK), 1)
    mask = (col > row) & (col <= row + WINDOW)
    grp = _iota((SWA_GROUP * tq, 1), 0) // tq
    keep = _iota((KV_WIDTH, BLOCK), 1) < BLOCK - n_new
    pad = jnp.zeros((BLOCK - tq, KV_WIDTH), f32)
    ktb, vtb, knb, vnb = [], [], [], []
    for b in range(bb):
        for j in range(N_SLABS):
            sl = slice(j * LANES, (j + 1) * LANES)
            x = q_ref[b, :, sl]
            xn = x * lax.rsqrt(_head_sum(x * x, gmat, passes=1) * (1.0 / HEAD_DIM) + NORM_EPS) * g_ref[...]
            qn_ref[b, :, sl] = (xn * cos_ref[...] + _rot_half(xn) * sin_ref[...]) * ATTN_SCALE
        kt, vt = kt_ref[b], vt_ref[b]
        kn = jnp.concatenate([kn_ref[b], pad], axis=0)
        vn = jnp.concatenate([vn_ref[b], pad], axis=0)
        kto_ref[b] = jnp.where(keep, pltpu.roll(kt, BLOCK - n_new, 1), pltpu.roll(kn.T, BLOCK - n_new, 1))
        vto_ref[b] = jnp.where(keep, pltpu.roll(vt, BLOCK - n_new, 1), pltpu.roll(vn.T, BLOCK - n_new, 1))
        ktb.append(kt.astype(bf16))
        vtb.append(vt.astype(bf16))
        knb.append(kn.astype(bf16))
        vnb.append(vn.astype(bf16))
    pairs = [(b, hk) for b in range(bb) for hk in range(SWA_KV_HEADS)]
    hs = [slice(hk * HEAD_DIM, (hk + 1) * HEAD_DIM) for hk in range(SWA_KV_HEADS)]
    q3s = [jnp.concatenate([qn_ref[b, :, (hk * SWA_GROUP + gq) * HEAD_DIM:(hk * SWA_GROUP + gq + 1) * HEAD_DIM]
                            for gq in range(SWA_GROUP)], axis=0).astype(bf16) for b, hk in pairs]
    ss = [jnp.where(mask, jnp.concatenate([_dot(q3, ktb[b][hs[hk], :]), _dot_nt(q3, knb[b][:, hs[hk]])], axis=1),
                    NEG_BIG) for q3, (b, hk) in zip(q3s, pairs)]
    sinks = [jnp.where(grp == 0, sink_ref[hk * SWA_GROUP],
                       jnp.where(grp == 1, sink_ref[hk * SWA_GROUP + 1], sink_ref[hk * SWA_GROUP + 2]))
             for hk in range(SWA_KV_HEADS)]
    ms = [jnp.maximum(jnp.max(s, axis=-1, keepdims=True), sinks[hk]) for s, (b, hk) in zip(ss, pairs)]
    es = [jnp.exp(s - m) for s, m in zip(ss, ms)]
    denoms = [jnp.sum(e, axis=-1, keepdims=True) + jnp.exp(sinks[hk] - m) for e, m, (b, hk) in zip(es, ms, pairs)]
    ebs = [e.astype(bf16) for e in es]
    outs = [(_dot_nt(eb[:, :BLOCK], vtb[b][hs[hk], :]) + _dot(eb[:, BLOCK:], vnb[b][:, hs[hk]])) / d
            for eb, d, (b, hk) in zip(ebs, denoms, pairs)]
    for o, (b, hk) in zip(outs, pairs):
        for gq in range(SWA_GROUP):
            hq = hk * SWA_GROUP + gq
            o_ref[b, :, hq * HEAD_DIM:(hq + 1) * HEAD_DIM] = o[gq * tq:(gq + 1) * tq]


def _swa_step(proj, g_qnorm, sinks, cos, sin, cache_kt, cache_vt, k_new, v_new, n_new, bb):
    bsz, tq, _ = proj.shape
    assert SWA_GROUP == 3 and tq & (tq - 1) == 0
    g = jnp.tile(g_qnorm.reshape(1, HEAD_DIM), (1, HEADS_PER_SLAB))
    cache_spec = pl.BlockSpec((bb, KV_WIDTH, BLOCK), lambda b: (b, 0, 0))
    new_spec = pl.BlockSpec((bb, tq, KV_WIDTH), lambda b: (b, 0, 0))
    tab_spec = pl.BlockSpec((tq, LANES), lambda b: (0, 0))
    return pl.pallas_call(
        functools.partial(_swa_step_kernel, n_new=n_new),
        grid=(bsz // bb,),
        in_specs=[pl.BlockSpec(memory_space=pltpu.SMEM),
                  pl.BlockSpec((bb, tq, MAIN_WIDTH), lambda b: (b, 0, 0)),
                  pl.BlockSpec((1, LANES), lambda b: (0, 0)), tab_spec, tab_spec,
                  cache_spec, cache_spec, new_spec, new_spec],
        out_specs=[pl.BlockSpec((bb, tq, MAIN_WIDTH), lambda b: (b, 0, 0)), cache_spec, cache_spec],
        out_shape=[jax.ShapeDtypeStruct((bsz, tq, MAIN_WIDTH), f32),
                   jax.ShapeDtypeStruct(cache_kt.shape, f32), jax.ShapeDtypeStruct(cache_vt.shape, f32)],
        scratch_shapes=[pltpu.VMEM((bb, tq, MAIN_WIDTH), f32)],
        compiler_params=_cp(("parallel",)),
        name="swa_step",
    )(sinks, proj, g, cos, sin, cache_kt, cache_vt, k_new, v_new)


def _out_mlp_kernel(*refs, tf, with_next):
    if with_next:
        (h_ref, main_ref, mem_ref, wo_ref, g_ref, wup_ref, wdn_ref,
         gkv_ref, wkv_ref, gk_ref, cos_ref, sin_ref, gin_ref, win_ref, o_ref, k_ref, v_ref, p_ref) = refs
    else:
        h_ref, main_ref, mem_ref, wo_ref, g_ref, wup_ref, wdn_ref, o_ref = refs
    h1 = (h_ref[...] + _dot(main_ref[...].astype(bf16), wo_ref[0, :MAIN_WIDTH, :])
          + _dot(mem_ref[...].astype(bf16), wo_ref[0, MAIN_WIDTH:, :]))
    hn = (h1 * lax.rsqrt(jnp.mean(h1 * h1, axis=-1, keepdims=True) + NORM_EPS) * g_ref[...]).astype(bf16)
    acc = None
    for j in range(D_FF // tf):
        u = _dot(hn, wup_ref[0, :, j * tf:(j + 1) * tf])
        u = jnp.square(jnp.maximum(u, 0.0)).astype(bf16)
        d = _dot(u, wdn_ref[0, j * tf:(j + 1) * tf, :])
        acc = d if acc is None else acc + d
    h2 = h1 + acc
    o_ref[...] = h2
    if with_next:
        xhat = h2 * lax.rsqrt(jnp.mean(h2 * h2, axis=-1, keepdims=True) + NORM_EPS)
        kv = _dot((xhat * gkv_ref[...]).astype(bf16), wkv_ref[...])
        slabs = _head_norm_rope(kv[:, :KV_WIDTH], gk_ref[...], cos_ref[...], sin_ref[...], _group_ones())
        for j, slab in enumerate(slabs):
            k_ref[:, j * LANES:(j + 1) * LANES] = slab
        v_ref[...] = kv[:, KV_WIDTH:]
        p_ref[...] = _dot((xhat * gin_ref[...]).astype(bf16), win_ref[...])


def _out_mlp(h, main, mem_o, layer, w_out_bf16, g_mlp, w_up_bf16, w_down_bf16, tm, tf=1024, nxt=None):
    n = h.shape[0]
    tm = min(tm, n)
    rows = lambda w: pl.BlockSpec((tm, w), lambda i: (i, 0))
    resident = lambda a: pl.BlockSpec(a.shape, lambda i: (0, 0))
    of_layer = lambda a: pl.BlockSpec((1,) + a.shape[1:], lambda i: (layer, 0, 0))
    row = lambda x: x.reshape(1, -1)
    args = [h, main, mem_o, w_out_bf16, row(g_mlp), w_up_bf16, w_down_bf16]
    in_specs = [rows(D_MODEL), rows(MAIN_WIDTH), rows(MEM_WIDTH), of_layer(w_out_bf16), resident(args[4]),
                of_layer(w_up_bf16), of_layer(w_down_bf16)]
    out_specs, out_shape = [rows(D_MODEL)], [jax.ShapeDtypeStruct((n, D_MODEL), f32)]
    if nxt is not None:
        kv_norm, w_kv, k_norm, cos, sin, in_norm, w_in = nxt
        nt = cos.shape[0] // tm
        table = pl.BlockSpec((tm, LANES), lambda i: (i % nt, 0))
        extra = [row(kv_norm), w_kv, jnp.tile(row(k_norm), (1, KV_WIDTH // HEAD_DIM)), cos, sin, row(in_norm), w_in]
        args += extra
        in_specs += [resident(extra[0]), resident(w_kv), resident(extra[2]), table, table,
                     resident(extra[5]), resident(w_in)]
        out_specs += [rows(KV_WIDTH), rows(KV_WIDTH), rows(D_MODEL)]
        out_shape += [jax.ShapeDtypeStruct((n, KV_WIDTH), f32)] * 2 + [jax.ShapeDtypeStruct((n, D_MODEL), f32)]
    res = pl.pallas_call(
        functools.partial(_out_mlp_kernel, tf=tf, with_next=nxt is not None),
        grid=(n // tm,),
        in_specs=in_specs,
        out_specs=out_specs,
        out_shape=out_shape,
        compiler_params=_cp(("parallel",)),
        name="out_mlp",
    )(*args)
    return res if nxt is not None else res[0]


def _rope_tables(pos, reps):
    half = HEAD_DIM // 2
    freqs = jnp.power(ROPE_THETA, -jnp.arange(half, dtype=f32) / half)
    ang = pos.astype(f32)[:, None] * freqs[None, :]
    cos, sin = jnp.cos(ang), jnp.sin(ang)
    cos_h = jnp.concatenate([cos, cos], axis=-1)
    sin_h = jnp.concatenate([-sin, sin], axis=-1)
    return jnp.tile(cos_h, (1, reps)), jnp.tile(sin_h, (1, reps))


def _trunk(x, pos, mem_kt, mem_vt, shift0, wkv0, swa_cache, t_real, W):
    bsz, tp, _ = x.shape
    n = bsz * tp
    prompt = swa_cache is None
    mem_bb = bsz if prompt else STEP_MEM_SEQS
    act_dtype = bf16 if prompt else f32
    tm = ROW_TILE
    flat = lambda a: a.reshape(n, a.shape[-1])
    unflat = lambda a: a.reshape(bsz, tp, a.shape[-1])
    cos2, sin2 = _rope_tables(pos, HEADS_PER_SLAB)

    prep_w = (W["shift_mu"], W["w_w2"], W["w0"], W["w_a2"], W["a0"], W["w_g2"], W["k_k"], W["k_a"], W["r_k"])
    if prompt:
        seqs, q_mem, new_shift = _proj_prep(x, W["norm_mix"][0], W["w_in_a"], shift0, prep_w, PREP_TILE, bf16)
        q_col = 0
        main, new_wkv = _wkv_chunked(*seqs, W["lnx_w"], W["lnx_b"])
    else:
        q_mem = unflat(_norm_matmul(flat(x), W["norm_mix"][0], W["w_in_a"], tm))
        q_col = RWKV_COLS // MEM_WIDTH
        new_shift = q_mem[:, t_real - 1, :RWKV_COLS]
        steps = jnp.transpose(q_mem[:, :t_real, :RWKV_COLS], (1, 0, 2)).reshape(1, t_real * bsz, RWKV_COLS)
        seqs = [s.reshape(t_real, bsz, MAIN_WIDTH) for s in _rwkv_prep(steps, shift0, prep_w)]
        main, new_wkv = _wkv_steps(wkv0, seqs, W["lnx_w"], W["lnx_b"], tp)
    mem_o = _mem_attn(q_mem, q_col, W["mem_qnorm"][0], mem_kt, mem_vt, 0, MEM_Q_TILE, mem_bb, act_dtype)
    tables = (cos2, sin2) if prompt else (jnp.tile(cos2, (tm // tp, 1)), jnp.tile(sin2, (tm // tp, 1)))
    h, k_sh, v_sh, proj = _out_mlp(flat(x), flat(main), flat(mem_o), 0, W["w_out"], W["norm_mlp"][0], W["w_up"],
                                   W["w_down"], tm,
                                   nxt=(W["kv_norm"], W["w_kv"], W["swa_knorm"], *tables, W["norm_mix"][1], W["w_in_b"]))
    k_sh, v_sh, proj = unflat(k_sh), unflat(v_sh), unflat(proj)

    if prompt:
        main, mem_o = _swa(proj, W["swa_qnorm"], W["sinks"], cos2, sin2, k_sh, v_sh, W["mem_qnorm"][1], mem_kt, mem_vt, 1)
        win = min(WINDOW, tp)
        heads = lambda a: a.reshape(bsz, a.shape[1], SWA_KV_HEADS, HEAD_DIM)
        k_state, v_state = heads(k_sh[:, tp - win:]), heads(v_sh[:, tp - win:])
    else:
        ckt, cvt = swa_cache
        main, kt_new, vt_new = _swa_step(proj, W["swa_qnorm"], W["sinks"], cos2, sin2, ckt, cvt, k_sh, v_sh,
                                         t_real, bb=STEP_SWA_SEQS)
        untransposed = lambda a: jnp.transpose(a.reshape(bsz, SWA_KV_HEADS, HEAD_DIM, -1), (0, 3, 1, 2))
        k_state, v_state = untransposed(kt_new), untransposed(vt_new)
    if not prompt:
        mem_o = _mem_attn(proj, MAIN_WIDTH // MEM_WIDTH, W["mem_qnorm"][1], mem_kt, mem_vt, 1, MEM_Q_TILE, mem_bb,
                          act_dtype)
    y = _out_mlp(h, flat(main), flat(mem_o), 1, W["w_out"], W["norm_mlp"][1], W["w_up"], W["w_down"], tm)
    return unflat(y)[:, :t_real], new_shift[None], new_wkv[None], k_state, v_state


def kernel(x_prompt, x_sample, state_rwkv_shift, state_rwkv_wkv, cache_swa_k, cache_swa_v, cache_mem_k,
           cache_mem_v, mem_prompt, norm_mix, norm_mlp, w_out, w_up, w_down, mem_norm, w_mem_kv, mem_qnorm,
           mem_knorm, w_in_a, shift_mu, w_w2, w0, w_a2, a0, w_g2, k_k, k_a, r_k, lnx_w, lnx_b, w_in_b,
           swa_qnorm, sinks, kv_norm, w_kv, swa_knorm):
    W = dict(norm_mix=norm_mix, norm_mlp=norm_mlp, w_out=w_out.astype(bf16), w_up=w_up.astype(bf16),
             w_down=w_down.astype(bf16), mem_qnorm=mem_qnorm, w_in_a=w_in_a[0].astype(bf16),
             shift_mu=shift_mu[0], w_w2=w_w2[0], w0=w0[0], w_a2=w_a2[0], a0=a0[0], w_g2=w_g2[0],
             k_k=k_k[0], k_a=k_a[0], r_k=r_k[0].reshape(-1), lnx_w=lnx_w[0], lnx_b=lnx_b[0],
             w_in_b=w_in_b[0].astype(bf16), swa_qnorm=swa_qnorm[0], sinks=sinks[0], kv_norm=kv_norm,
             w_kv=w_kv.astype(bf16), swa_knorm=swa_knorm)
    bp, tp, _ = x_prompt.shape
    bs, ts, _ = x_sample.shape
    depth = norm_mix.shape[0]

    mem_flat = mem_prompt.reshape(bp * N_MEM, D_MODEL)
    p_mem_k, p_mem_v = [], []
    for l in range(depth):
        kv = _norm_matmul(mem_flat, mem_norm[l], w_mem_kv[l].astype(bf16), ROW_TILE)
        mk, mv = _kv_post(kv, mem_knorm[l], N_MEM)
        p_mem_k.append(mk.reshape(bp, N_MEM, MEM_WIDTH))
        p_mem_v.append(mv.reshape(bp, N_MEM, MEM_WIDTH))
    transposed = lambda a: jnp.swapaxes(jnp.stack(a), 2, 3)
    y_p, p_shift, p_wkv, p_k, p_v = _trunk(
        x_prompt, jnp.arange(tp), transposed(p_mem_k), transposed(p_mem_v), jnp.zeros((bp, RWKV_COLS), f32),
        None, None, tp, W)
    mem_heads = lambda a: jnp.stack(a).reshape(depth, bp, N_MEM, MEM_HEADS, HEAD_DIM)
    cache_t = lambda a: jnp.transpose(a, (0, 1, 3, 4, 2)).reshape(depth, bs, MEM_WIDTH, N_MEM)

    tpad = -(-ts // SUBLANES) * SUBLANES
    x_s = jnp.pad(x_sample, ((0, 0), (0, tpad - ts), (0, 0)))
    win = cache_swa_k.shape[1]
    assert win == WINDOW
    swa_t = lambda a: jnp.transpose(a, (0, 2, 3, 1)).reshape(bs, KV_WIDTH, win)
    y_s, s_shift, s_wkv, s_k, s_v = _trunk(
        x_s, PAST_LEN + jnp.arange(tpad), cache_t(cache_mem_k), cache_t(cache_mem_v),
        state_rwkv_shift[0], state_rwkv_wkv[0],
        (swa_t(cache_swa_k), swa_t(cache_swa_v)), ts, W)

    return (y_p, y_s, p_shift, p_wkv, p_k, p_v, mem_heads(p_mem_k), mem_heads(p_mem_v),
            s_shift, s_wkv, s_k, s_v)
```
